```python
import jax
import jax.numpy as jnp
from jax import lax
import numpy as np

D_MODEL = 1024
BATCH = 16
SEQ = 2048
DEPTH = 2

GRID_W = 64
CTX_LEN = 256
CONV_DIM = 512
CONV_K = 31
ML_HEADS = 4
ML_HEAD_DIM = 128
ML_DIM = ML_HEADS * ML_HEAD_DIM
ML_CHUNK = 128
MLA_HEADS = 8
QK_NOPE = 64
QK_ROPE = 32
V_HEAD = 64
Q_LORA = 768
KV_LORA = 256
ROPE_THETA = 10000.0
ROPE_FREQ = QK_ROPE // 4
MLA_SCALE = (QK_NOPE + QK_ROPE) ** -0.5
Q_BLOCK = 128
D_FF = 4 * D_MODEL
N_BRANCH = 3
LN_EPS = 1e-5

COLS = (2 * CONV_DIM, 3 * ML_DIM, ML_DIM, 4 * ML_HEADS, Q_LORA, KV_LORA, QK_ROPE, N_BRANCH * D_MODEL)
IN_SPLITS = tuple(int(s) for s in np.cumsum(COLS)[:-1])
N_IN = int(sum(COLS))
OFF_GIF = COLS[0] + COLS[1] + COLS[2]

kernel_name = 'hybrid_conv_mlstm_mla_block'


def layer_norm(x, g, b):
    xf = x.astype(jnp.float32)
    mu = jnp.mean(xf, -1, keepdims=True)
    var = jnp.mean(jnp.square(xf - mu), -1, keepdims=True)
    return ((xf - mu) * lax.rsqrt(var + LN_EPS) * g + b).astype(x.dtype)


def rms_norm(x, g):
    xf = x.astype(jnp.float32)
    return (xf * lax.rsqrt(jnp.mean(jnp.square(xf), -1, keepdims=True) + LN_EPS) * g).astype(x.dtype)


def axial_rope_tables(n_tokens):
    rows = n_tokens // GRID_W
    rr, cc = jnp.meshgrid(jnp.arange(rows, dtype=jnp.float32), jnp.arange(GRID_W, dtype=jnp.float32), indexing='ij')
    inv = ROPE_THETA ** (-jnp.arange(ROPE_FREQ, dtype=jnp.float32) / ROPE_FREQ)
    ang = jnp.stack([rr.reshape(-1), cc.reshape(-1)], -1)[..., None] * inv
    return jnp.cos(ang), jnp.sin(ang)


def apply_axial_rope(x, cos, sin):
    shp = x.shape
    xf = x.astype(jnp.float32).reshape(shp[:-1] + (2, 2, ROPE_FREQ))
    x1, x2 = xf[..., 0, :], xf[..., 1, :]
    out = jnp.stack([x1 * cos - x2 * sin, x2 * cos + x1 * sin], -2)
    return out.reshape(shp).astype(x.dtype)


def conformer_conv(a, w_dw, b_dw, g, b, w_pw):
    val, gate = jnp.split(a, 2, axis=-1)
    h = val * jax.nn.sigmoid(gate)
    h = lax.conv_general_dilated(h, w_dw[:, None, :].astype(h.dtype), (1,), ((CONV_K // 2, CONV_K // 2),),
                                 dimension_numbers=('NWC', 'WIO', 'NWC'), feature_group_count=CONV_DIM) + b_dw
    return jax.nn.silu(layer_norm(h, g, b)) @ w_pw


def mlstm_inputs(qkv, g_if):
    bsz, t, _ = qkv.shape
    def heads(z):
        return z.reshape(bsz, t, ML_HEADS, ML_HEAD_DIM).transpose(0, 2, 1, 3)
    q, k, v = jnp.split(qkv, 3, axis=-1)
    g = g_if.astype(jnp.float32).reshape(bsz, t, 4, ML_HEADS).transpose(2, 0, 3, 1)
    fwd = (g[0], jax.nn.log_sigmoid(g[1]))
    bwd = (g[2], jax.nn.log_sigmoid(g[3]))
    return heads(q), heads(k) * ML_HEAD_DIM ** -0.5, heads(v), fwd, bwd


def mlstm_init_state(bsz):
    return (jnp.zeros((bsz, ML_HEADS, ML_HEAD_DIM, ML_HEAD_DIM), jnp.float32),
            jnp.zeros((bsz, ML_HEADS, ML_HEAD_DIM), jnp.float32),
            jnp.zeros((bsz, ML_HEADS), jnp.float32))


def mlstm_chunkwise(q, k, v, logi, logf, state):
    bsz, nh, t, dh = q.shape
    nc = t // ML_CHUNK
    def to_chunks(z):
        return jnp.moveaxis(z.reshape(z.shape[:2] + (nc, ML_CHUNK) + z.shape[3:]), 2, 0)
    causal = jnp.tril(jnp.ones((ML_CHUNK, ML_CHUNK), bool))

    def step(carry, xs):
        cmat, nvec, m = carry
        qb, kb, vb, li, lf = xs
        qf, kf, vf = qb.astype(jnp.float32), kb.astype(jnp.float32), vb.astype(jnp.float32)
        bcum = jnp.cumsum(lf, -1)
        log_d = jnp.where(causal, bcum[..., :, None] - bcum[..., None, :] + li[..., None, :], -jnp.inf)
        inter = bcum + m[..., None]
        m_row = jnp.maximum(inter, jnp.max(log_d, -1))
        a_inter = jnp.exp(inter - m_row)
        s = jnp.einsum('bhld,bhsd->bhls', qf, kf) * jnp.exp(log_d - m_row[..., None])
        num = a_inter[..., None] * jnp.einsum('bhld,bhde->bhle', qf, cmat) + jnp.einsum('bhls,bhse->bhle', s, vf)
        den = a_inter * jnp.einsum('bhld,bhd->bhl', qf, nvec) + jnp.sum(s, -1)
        h = num / jnp.maximum(jnp.abs(den), jnp.exp(-m_row))[..., None]
        b_last = bcum[..., -1]
        w_src = b_last[..., None] - bcum + li
        m_new = jnp.maximum(b_last + m, jnp.max(w_src, -1))
        a_state = jnp.exp(b_last + m - m_new)
        w = jnp.exp(w_src - m_new[..., None])
        c_new = a_state[..., None, None] * cmat + jnp.einsum('bhs,bhsd,bhse->bhde', w, kf, vf)
        n_new = a_state[..., None] * nvec + jnp.einsum('bhs,bhsd->bhd', w, kf)
        return (c_new, n_new, m_new), h

    state, hs = lax.scan(step, state, (to_chunks(q), to_chunks(k), to_chunks(v), to_chunks(logi), to_chunks(logf)))
    return jnp.moveaxis(hs, 0, 2).reshape(bsz, nh, t, dh), state


def flip_t(z):
    return jnp.flip(z, axis=2)


def mlstm_bidir(q, k, v, fwd, bwd, st_f, st_b):
    h_f, st_f = mlstm_chunkwise(q, k, v, fwd[0], fwd[1], st_f)
    h_b, st_b = mlstm_chunkwise(flip_t(q), flip_t(k), flip_t(v), flip_t(bwd[0]), flip_t(bwd[1]), st_b)
    return h_f + flip_t(h_b), st_f, st_b


def mlstm_out(h, o_pre, g, w):
    bsz, nh, t, dh = h.shape
    h = h.transpose(0, 2, 1, 3)
    mu = jnp.mean(h, -1, keepdims=True)
    var = jnp.mean(jnp.square(h - mu), -1, keepdims=True)
    h = ((h - mu) * lax.rsqrt(var + LN_EPS)).reshape(bsz, t, ML_DIM) * g
    return (jax.nn.sigmoid(o_pre) * h.astype(o_pre.dtype)) @ w


def mla_project(cq, ckv, kr, qn_g, w_uq, kvn_g, w_ukv, rope):
    bsz, t, _ = cq.shape
    q = (rms_norm(cq, qn_g) @ w_uq).reshape(bsz, t, MLA_HEADS, QK_NOPE + QK_ROPE)
    kv = (rms_norm(ckv, kvn_g) @ w_ukv).reshape(bsz, t, MLA_HEADS, QK_NOPE + V_HEAD)
    qn, qr = q[..., :QK_NOPE], q[..., QK_NOPE:]
    kn, v = kv[..., :QK_NOPE], kv[..., QK_NOPE:]
    if rope is not None:
        cos, sin = rope
        qr = apply_axial_rope(qr, cos[:, None], sin[:, None])
        kr = apply_axial_rope(kr, cos, sin)
    return qn, qr, kn, kr, v


def mla_attend(qn, qr, kn, kr, v):
    s = jnp.einsum('bqhd,bkhd->bhqk', qn, kn) + jnp.einsum('bqhr,bkr->bhqk', qr, kr)
    p = jax.nn.softmax(s.astype(jnp.float32) * MLA_SCALE, axis=-1).astype(v.dtype)
    return jnp.einsum('bhqk,bkhd->bqhd', p, v)


def mla_blockwise(qn, qr, kn, kr, v):
    bsz, t = qn.shape[:2]
    nb = t // Q_BLOCK
    def blocks(z):
        return jnp.moveaxis(z.reshape((bsz, nb, Q_BLOCK) + z.shape[2:]), 1, 0)
    o = lax.map(lambda qb: mla_attend(qb[0], qb[1], kn, kr, v), (blocks(qn), blocks(qr)))
    return jnp.moveaxis(o, 0, 1).reshape(bsz, t, MLA_HEADS * V_HEAD)


def gated_merge(gates, y_a, y_b, y_c, w_out, b_out):
    g_a, g_b, g_c = jnp.split(jax.nn.sigmoid(gates), N_BRANCH, axis=-1)
    return (g_a * y_a + g_b * y_b + g_c * y_c) @ w_out + b_out


def token_mixer(u, uc, rope, w_in, b_in, w_dw, b_dw, cn_g, cn_b, w_conv_out, ml_g, w_ml_out,
                qn_g, w_uq, kvn_g, w_ukv, w_mla_out, w_out, b_out, with_ctx):
    a, qkv, o_pre, g_if, cq, ckv, kr, gates = jnp.split(u @ w_in + b_in, IN_SPLITS, axis=-1)
    a_c, qkv_c, o_pre_c, g_if_c, cq_c, ckv_c, kr_c, gates_c = jnp.split(uc @ w_in + b_in, IN_SPLITS, axis=-1)
    bsz = u.shape[0]
    y_conv = conformer_conv(a, w_dw, b_dw, cn_g, cn_b, w_conv_out)
    q_c, k_c, v_c, fwd_c, bwd_c = mlstm_inputs(qkv_c, g_if_c)
    q_x, k_x, v_x, fwd_x, bwd_x = mlstm_inputs(qkv, g_if)
    s0 = mlstm_init_state(bsz)
    h_c, st_f, st_b = mlstm_bidir(q_c, k_c, v_c, fwd_c, bwd_c, s0, s0)
    h_x, _, _ = mlstm_bidir(q_x, k_x, v_x, fwd_x, bwd_x, st_f, st_b)
    y_ml = mlstm_out(h_x, o_pre, ml_g, w_ml_out)
    qn_c, qr_c, kn_c, kr_cc, vv_c = mla_project(cq_c, ckv_c, kr_c, qn_g, w_uq, kvn_g, w_ukv, None)
    qn_x, qr_x, kn_x, kr_x, vv_x = mla_project(cq, ckv, kr, qn_g, w_uq, kvn_g, w_ukv, rope)
    o_x = mla_blockwise(qn_x, qr_x, jnp.concatenate([kn_c, kn_x], 1), jnp.concatenate([kr_cc, kr_x], 1),
                        jnp.concatenate([vv_c, vv_x], 1))
    y = gated_merge(gates, y_conv, y_ml, o_x @ w_mla_out, w_out, b_out)
    if not with_ctx:
        return y, None
    y_conv_c = conformer_conv(a_c, w_dw, b_dw, cn_g, cn_b, w_conv_out)
    y_ml_c = mlstm_out(h_c, o_pre_c, ml_g, w_ml_out)
    o_c = mla_attend(qn_c, qr_c, kn_c, kr_cc, vv_c).reshape(bsz, uc.shape[1], MLA_HEADS * V_HEAD)
    yc = gated_merge(gates_c, y_conv_c, y_ml_c, o_c @ w_mla_out, w_out, b_out)
    return y, yc


def sq_relu_mlp(u, w1, b1, w2, b2):
    return jnp.square(jax.nn.relu(u @ w1 + b1)) @ w2 + b2


def setup_inputs(seed: int = 0) -> dict:
    key = jax.random.key(seed)
    ks = iter(jax.random.split(key, 40))
    def nrm(shape, std):
        return std * jax.random.normal(next(ks), shape, jnp.float32)
    L, D = DEPTH, D_MODEL
    beta = (8 * DEPTH) ** -0.25
    fb = jnp.linspace(3.0, 6.0, ML_HEADS, dtype=jnp.float32)
    b_in = nrm((L, N_IN), 0.01)
    b_in = b_in.at[:, OFF_GIF + ML_HEADS:OFF_GIF + 2 * ML_HEADS].add(fb)
    b_in = b_in.at[:, OFF_GIF + 3 * ML_HEADS:OFF_GIF + 4 * ML_HEADS].add(fb)
    mla_v = MLA_HEADS * V_HEAD
    return {
        'x': nrm((BATCH, SEQ, D), 1.0),
        'c': nrm((BATCH, D), 1.0),
        'ctx': nrm((BATCH, CTX_LEN, D), 1.0),
        'c_ctx': nrm((D,), 1.0),
        'w_mod': nrm((L, D, 6 * D), 0.5 * D ** -0.5),
        'b_mod': nrm((L, 6 * D), 0.01),
        'w_in': nrm((L, D, N_IN), D ** -0.5),
        'b_in': b_in,
        'w_dw': nrm((L, CONV_K, CONV_DIM), CONV_K ** -0.5),
        'b_dw': nrm((L, CONV_DIM), 0.01),
        'conv_norm_g': 1.0 + nrm((L, CONV_DIM), 0.02),
        'conv_norm_b': nrm((L, CONV_DIM), 0.01),
        'w_conv_out': nrm((L, CONV_DIM, D), beta * CONV_DIM ** -0.5),
        'mlstm_norm_g': 1.0 + nrm((L, ML_DIM), 0.02),
        'w_mlstm_out': nrm((L, ML_DIM, D), beta * ML_DIM ** -0.5),
        'q_norm_g': 1.0 + nrm((L, Q_LORA), 0.02),
        'w_uq': nrm((L, Q_LORA, MLA_HEADS * (QK_NOPE + QK_ROPE)), Q_LORA ** -0.5),
        'kv_norm_g': 1.0 + nrm((L, KV_LORA), 0.02),
        'w_ukv': nrm((L, KV_LORA, MLA_HEADS * (QK_NOPE + V_HEAD)), KV_LORA ** -0.5),
        'w_mla_out': nrm((L, mla_v, D), beta * mla_v ** -0.5),
        'w_out': nrm((L, D, D), beta * D ** -0.5),
        'b_out': nrm((L, D), 0.01),
        'ln1_g': 1.0 + nrm((L, D), 0.02),
        'ln1_b': nrm((L, D), 0.01),
        'w1': nrm((L, D, D_FF), D ** -0.5),
        'b1': nrm((L, D_FF), 0.01),
        'w2': nrm((L, D_FF, D), beta * D_FF ** -0.5),
        'b2': nrm((L, D), 0.01),
        'ln2_g': 1.0 + nrm((L, D), 0.02),
        'ln2_b': nrm((L, D), 0.01),
    }


def reference(x, c, ctx, c_ctx, w_mod, b_mod, w_in, b_in, w_dw, b_dw, conv_norm_g, conv_norm_b, w_conv_out,
              mlstm_norm_g, w_mlstm_out, q_norm_g, w_uq, kv_norm_g, w_ukv, w_mla_out, w_out, b_out,
              ln1_g, ln1_b, w1, b1, w2, b2, ln2_g, ln2_b):
    alpha = (2 * DEPTH) ** 0.25
    rope = axial_rope_tables(x.shape[1])
    for l in range(DEPTH):
        with_ctx = l < DEPTH - 1
        sh1, sc1, g1, sh2, sc2, g2 = jnp.split((jax.nn.silu(c) @ w_mod[l] + b_mod[l])[:, None, :], 6, axis=-1)
        csh1, csc1, cg1, csh2, csc2, cg2 = jnp.split(jax.nn.silu(c_ctx) @ w_mod[l] + b_mod[l], 6, axis=-1)
        u = x * (1.0 + sc1) + sh1
        uc = ctx * (1.0 + csc1) + csh1
        y, yc = token_mixer(u, uc, rope, w_in[l], b_in[l], w_dw[l], b_dw[l], conv_norm_g[l], conv_norm_b[l],
                            w_conv_out[l], mlstm_norm_g[l], w_mlstm_out[l], q_norm_g[l], w_uq[l], kv_norm_g[l],
                            w_ukv[l], w_mla_out[l], w_out[l], b_out[l], with_ctx)
        x = layer_norm(alpha * x + g1 * y, ln1_g[l], ln1_b[l])
        x = layer_norm(alpha * x + g2 * sq_relu_mlp(x * (1.0 + sc2) + sh2, w1[l], b1[l], w2[l], b2[l]), ln2_g[l], ln2_b[l])
        if with_ctx:
            ctx = layer_norm(alpha * ctx + cg1 * yc, ln1_g[l], ln1_b[l])
            ctx = layer_norm(alpha * ctx + cg2 * sq_relu_mlp(ctx * (1.0 + csc2) + csh2, w1[l], b1[l], w2[l], b2[l]),
                             ln2_g[l], ln2_b[l])
    return x
```

```python
import functools

import numpy as np
import jax
import jax.numpy as jnp
from jax import lax
from jax.experimental import pallas as pl
from jax.experimental.pallas import tpu as pltpu

F32 = jnp.float32
BF16 = jnp.bfloat16

D_MODEL = 1024
DEPTH = 2
GRID_W = 64
CONV_DIM = 512
CONV_K = 31
ML_HEADS = 4
ML_HEAD_DIM = 128
ML_DIM = ML_HEADS * ML_HEAD_DIM
ML_CHUNK = 128
MLA_HEADS = 8
QK_NOPE = 64
QK_ROPE = 32
V_HEAD = 64
Q_LORA = 768
KV_LORA = 256
ROPE_THETA = 10000.0
ROPE_FREQ = QK_ROPE // 4
MLA_SCALE = (QK_NOPE + QK_ROPE) ** -0.5
ML_SCALE = ML_HEAD_DIM ** -0.5
D_FF = 4 * D_MODEL
LN_EPS = 1e-5
ALPHA = (2 * DEPTH) ** 0.25

LANES = 128
HEAD_PAD = LANES
ROPE_HALF = QK_ROPE // 2
VMEM_LIMIT = 56 * 1024 * 1024

OFF_A = 0
OFF_QKV = OFF_A + 2 * CONV_DIM
OFF_O = OFF_QKV + 3 * ML_DIM
OFF_CQ = OFF_O + ML_DIM
OFF_CKV = OFF_CQ + Q_LORA
OFF_KR = OFF_CKV + KV_LORA
OFF_GIF = OFF_KR + HEAD_PAD
OFF_GATES = OFF_GIF + LANES
N_PACK = OFF_GATES + 3 * D_MODEL

_COLS = (2 * CONV_DIM, 3 * ML_DIM, ML_DIM, 4 * ML_HEADS, Q_LORA, KV_LORA, QK_ROPE, 3 * D_MODEL)
_SRC = tuple(int(s) for s in np.cumsum((0,) + _COLS))


def _const_spec(shape):
    nd = len(shape)
    return pl.BlockSpec(shape, lambda *_: (0,) * nd, pipeline_mode=pl.Buffered(1))


def _sigmoid(z):
    return 1.0 / (1.0 + jnp.exp(-z))


def _log_sigmoid(z):
    return jnp.minimum(z, 0.0) - jnp.log1p(jnp.exp(-jnp.abs(z)))


def _layer_norm(z, g, b):
    mu = jnp.mean(z, axis=-1, keepdims=True)
    zc = z - mu
    var = jnp.mean(zc * zc, axis=-1, keepdims=True)
    return zc * lax.rsqrt(var + LN_EPS) * g + b


def _mod_kernel(c_ref, w_ref, b_ref, o_ref):
    c = c_ref[...]
    s = (c * _sigmoid(c)).astype(BF16)
    o_ref[0] = jnp.dot(s, w_ref[0].astype(BF16), preferred_element_type=F32) + b_ref[0]


def _modulation(cc, w_mod, b_mod):
    nl, d, n = w_mod.shape
    r = cc.shape[0]
    tn = 1024
    return pl.pallas_call(
        _mod_kernel,
        grid=(nl, n // tn),
        in_specs=[
            pl.BlockSpec((r, d), lambda l, j: (0, 0)),
            pl.BlockSpec((1, d, tn), lambda l, j: (l, 0, j)),
            pl.BlockSpec((1, 1, tn), lambda l, j: (l, 0, j)),
        ],
        out_specs=pl.BlockSpec((1, r, tn), lambda l, j: (l, 0, j)),
        out_shape=jax.ShapeDtypeStruct((nl, r, n), F32),
        compiler_params=pltpu.CompilerParams(
            dimension_semantics=("arbitrary", "arbitrary"), vmem_limit_bytes=VMEM_LIMIT),
    )(cc, w_mod, b_mod.reshape(nl, 1, n))


def _rope(z, tc, tsm, tsp):
    return z * tc + pltpu.roll(z, HEAD_PAD - ROPE_HALF, 1) * tsm + pltpu.roll(z, ROPE_HALF, 1) * tsp


def _inproj_kernel(x_ref, sc_ref, sh_ref, w_ref, b_ref, qng_ref, wuq_ref, kvg_ref, wkn_ref, wv_ref,
                   tc_ref, tsm_ref, tsp_ref,
                   glu_ref, qkv_ref, so_ref, g_ref, gt_ref, gates_ref, q_ref, k_ref, v_ref):
    u = (x_ref[0] * (1.0 + sc_ref[0]) + sh_ref[0]).astype(BF16)

    def proj(off, n):
        return jnp.dot(u, w_ref[:, off:off + n], preferred_element_type=F32) + b_ref[:, off:off + n]

    a = proj(OFF_A, 2 * CONV_DIM)
    glu_ref[0] = (a[:, :CONV_DIM] * _sigmoid(a[:, CONV_DIM:])).astype(BF16)

    qkv_ref[0, :, :ML_DIM] = proj(OFF_QKV, ML_DIM).astype(BF16)
    qkv_ref[0, :, ML_DIM:2 * ML_DIM] = (proj(OFF_QKV + ML_DIM, ML_DIM) * ML_SCALE).astype(BF16)
    qkv_ref[0, :, 2 * ML_DIM:] = proj(OFF_QKV + 2 * ML_DIM, ML_DIM).astype(BF16)
    so_ref[0] = _sigmoid(proj(OFF_O, ML_DIM)).astype(BF16)
    for j in range(3):
        gates_ref[0, :, j * D_MODEL:(j + 1) * D_MODEL] = _sigmoid(
            proj(OFF_GATES + j * D_MODEL, D_MODEL)).astype(BF16)

    g = proj(OFF_GIF, LANES)
    g_ref[0] = g
    gt_ref[0] = g.T[:4 * ML_HEADS, :]

    tc, tsm, tsp = tc_ref[...], tsm_ref[...], tsp_ref[...]

    cq = proj(OFF_CQ, Q_LORA)
    nq = cq * lax.rsqrt(jnp.mean(cq * cq, axis=-1, keepdims=True) + LN_EPS) * qng_ref[...]
    q = jnp.dot(nq.astype(BF16), wuq_ref[...], preferred_element_type=F32)
    for h in range(MLA_HEADS):
        sl = slice(h * HEAD_PAD, (h + 1) * HEAD_PAD)
        q_ref[0, :, sl] = (_rope(q[:, sl], tc, tsm, tsp) * MLA_SCALE).astype(BF16)

    ckv = proj(OFF_CKV, KV_LORA)
    nkv = (ckv * lax.rsqrt(jnp.mean(ckv * ckv, axis=-1, keepdims=True) + LN_EPS) * kvg_ref[...]).astype(BF16)
    kn = jnp.dot(nkv, wkn_ref[...], preferred_element_type=F32)
    kr = _rope(proj(OFF_KR, HEAD_PAD), tc, tsm, tsp)
    for h in range(MLA_HEADS):
        sl = slice(h * HEAD_PAD, (h + 1) * HEAD_PAD)
        k_ref[0, :, sl] = (kn[:, sl] + kr).astype(BF16)
    v_ref[0] = jnp.dot(nkv, wv_ref[...], preferred_element_type=F32).astype(BF16)


def _in_projection(x, sc, sh, lw, tables, tm):
    bsz, t, d = x.shape
    tok = lambda n: pl.BlockSpec((1, tm, n), lambda b, i: (b, i, 0))
    vec = pl.BlockSpec((1, 1, d), lambda b, i: (b, 0, 0))
    tab = pl.BlockSpec((tm, HEAD_PAD), lambda b, i: (i, 0))
    hq = MLA_HEADS * HEAD_PAD
    out_shape = [
        jax.ShapeDtypeStruct((bsz, t, CONV_DIM), BF16),
        jax.ShapeDtypeStruct((bsz, t, 3 * ML_DIM), BF16),
        jax.ShapeDtypeStruct((bsz, t, ML_DIM), BF16),
        jax.ShapeDtypeStruct((bsz, t, LANES), F32),
        jax.ShapeDtypeStruct((bsz, 4 * ML_HEADS, t), F32),
        jax.ShapeDtypeStruct((bsz, t, 3 * D_MODEL), BF16),
        jax.ShapeDtypeStruct((bsz, t, hq), BF16),
        jax.ShapeDtypeStruct((bsz, t, hq), BF16),
        jax.ShapeDtypeStruct((bsz, t, MLA_HEADS * V_HEAD), BF16),
    ]
    out_specs = [tok(CONV_DIM), tok(3 * ML_DIM), tok(ML_DIM), tok(LANES),
                 pl.BlockSpec((1, 4 * ML_HEADS, tm), lambda b, i: (b, 0, i)),
                 tok(3 * D_MODEL), tok(hq), tok(hq), tok(MLA_HEADS * V_HEAD)]
    return pl.pallas_call(
        _inproj_kernel,
        grid=(bsz, t // tm),
        in_specs=[tok(d), vec, vec,
                  _const_spec((d, N_PACK)), _const_spec((1, N_PACK)),
                  _const_spec((1, Q_LORA)), _const_spec((Q_LORA, hq)),
                  _const_spec((1, KV_LORA)), _const_spec((KV_LORA, hq)),
                  _const_spec((KV_LORA, MLA_HEADS * V_HEAD)),
                  tab, tab, tab],
        out_specs=out_specs,
        out_shape=out_shape,
        compiler_params=pltpu.CompilerParams(
            dimension_semantics=("parallel", "arbitrary"), vmem_limit_bytes=VMEM_LIMIT),
    )(x, sc, sh, lw["w_pack"], lw["b_pack"], lw["qn_g"], lw["w_uq"], lw["kvn_g"], lw["w_kn"], lw["w_v"],
      *tables)


CONV_HALO = 16
CONV_TILE = 128


def _conv_kernel(h_ref, wdw_ref, bdw_ref, g_ref, b_ref, o_ref, hp, stage, cv, *, t):
    nt = t // CONV_TILE
    hp[0:CONV_HALO, :] = jnp.zeros((CONV_HALO, CONV_DIM), F32)
    hp[t + CONV_HALO:t + 2 * CONV_HALO, :] = jnp.zeros((CONV_HALO, CONV_DIM), F32)

    def fill(i, carry):
        r0 = pl.multiple_of(i * CONV_TILE, CONV_TILE)
        hp[pl.ds(r0 + CONV_HALO, CONV_TILE), :] = h_ref[0, pl.ds(r0, CONV_TILE), :].astype(F32)
        return carry

    lax.fori_loop(0, nt, fill, 0)

    def tile(i, carry):
        r0 = pl.multiple_of(i * CONV_TILE, CONV_TILE)
        stage[...] = hp[pl.ds(r0, CONV_TILE + 2 * CONV_HALO), :]
        for cg in range(CONV_DIM // LANES):
            sl = slice(cg * LANES, (cg + 1) * LANES)
            acc = jnp.zeros((CONV_TILE, LANES), F32)
            for k in range(CONV_K):
                off = CONV_HALO - CONV_K // 2 + k
                acc = acc + stage[off:off + CONV_TILE, sl] * wdw_ref[k:k + 1, sl]
            cv[:, sl] = acc + bdw_ref[:, sl]
        z = _layer_norm(cv[...], g_ref[...], b_ref[...])
        o_ref[0, pl.ds(r0, CONV_TILE), :] = (z * _sigmoid(z)).astype(BF16)
        return carry

    lax.fori_loop(0, nt, tile, 0)


def _conv_branch(glu, lw):
    bsz, t, _ = glu.shape
    return pl.pallas_call(
        functools.partial(_conv_kernel, t=t),
        grid=(bsz,),
        in_specs=[pl.BlockSpec((1, t, CONV_DIM), lambda b: (b, 0, 0)),
                  _const_spec((CONV_K + 1, CONV_DIM)), _const_spec((1, CONV_DIM)),
                  _const_spec((1, CONV_DIM)), _const_spec((1, CONV_DIM))],
        out_specs=pl.BlockSpec((1, t, CONV_DIM), lambda b: (b, 0, 0)),
        out_shape=jax.ShapeDtypeStruct((bsz, t, CONV_DIM), BF16),
        scratch_shapes=[pltpu.VMEM((t + 2 * CONV_HALO, CONV_DIM), F32),
                        pltpu.VMEM((CONV_TILE + 2 * CONV_HALO, CONV_DIM), F32),
                        pltpu.VMEM((CONV_TILE, CONV_DIM), F32)],
        compiler_params=pltpu.CompilerParams(
            dimension_semantics=("parallel",), vmem_limit_bytes=VMEM_LIMIT),
    )(glu, lw["w_dw"], lw["b_dw"], lw["cn_g"], lw["cn_b"])


def _mlstm_kernel(qkv_c_ref, g_c_ref, gt_c_ref, so_c_ref, qkv_x_ref, g_x_ref, gt_x_ref, so_x_ref, mlg_ref,
                  hn_c_ref, hn_x_ref, hf_c, hb_c, hf_x, hb_x, c_scr, m_scr, *, t_c, t_x):
    L = ML_CHUNK
    dh = ML_HEAD_DIM
    row = lax.broadcasted_iota(jnp.int32, (L, L), 0)
    col = lax.broadcasted_iota(jnp.int32, (L, L), 1)
    lower = col <= row
    upper = col >= row
    ones_aug = jnp.ones((L, dh), BF16)

    c_scr[...] = jnp.zeros(c_scr.shape, F32)
    m_scr[...] = jnp.zeros(m_scr.shape, F32)

    def step(qkv_ref, g_ref, gt_ref, hs_ref, r0, d):
        gc = g_ref[0, pl.ds(r0, L), :]
        gr = gt_ref[0, :, pl.ds(r0, L)]
        lfc = _log_sigmoid(gc)
        lfr = _log_sigmoid(gr)
        mask = lower if d == 0 else upper
        mask_t = upper if d == 0 else lower
        cumc = jnp.dot(mask.astype(F32), lfc, preferred_element_type=F32, precision=lax.Precision.HIGHEST)
        cumr = jnp.dot(lfr, mask_t.astype(F32), preferred_element_type=F32, precision=lax.Precision.HIGHEST)
        for h in range(ML_HEADS):
            idx = d * ML_HEADS + h
            ii = 2 * d * ML_HEADS + h
            fi = (2 * d + 1) * ML_HEADS + h
            li_r = gr[ii:ii + 1, :]
            li_c = gc[:, ii:ii + 1]
            bc_r = cumr[fi:fi + 1, :]
            bc_c = cumc[:, fi:fi + 1]
            b_last = jnp.sum(lfr[fi:fi + 1, :], axis=1, keepdims=True)
            m = m_scr[idx][0:1, 0:1]
            log_d = jnp.where(mask, bc_c - bc_r + li_r, -jnp.inf)
            inter = bc_c + m
            m_row = jnp.maximum(inter, jnp.max(log_d, axis=-1, keepdims=True))
            a_inter = jnp.exp(inter - m_row)
            dmat = jnp.exp(log_d - m_row)
            q = qkv_ref[0, pl.ds(r0, L), h * dh:(h + 1) * dh]
            k = qkv_ref[0, pl.ds(r0, L), ML_DIM + h * dh:ML_DIM + (h + 1) * dh]
            v = qkv_ref[0, pl.ds(r0, L), 2 * ML_DIM + h * dh:2 * ML_DIM + (h + 1) * dh]
            s = lax.dot_general(q, k, (((1,), (1,)), ((), ())), preferred_element_type=F32) * dmat
            caug = c_scr[idx]
            vaug = jnp.concatenate([v, ones_aug], axis=1)
            num_aug = (a_inter * jnp.dot(q, caug.astype(BF16), preferred_element_type=F32)
                       + jnp.dot(s.astype(BF16), vaug, preferred_element_type=F32))
            num = num_aug[:, :dh]
            den = num_aug[:, dh:]
            hs_ref[pl.ds(r0, L), h * dh:(h + 1) * dh] = num / jnp.maximum(jnp.abs(den), jnp.exp(-m_row))
            w_src_c = b_last - bc_c + li_c
            w_src_r = b_last - bc_r + li_r
            m_new = jnp.maximum(b_last + m, jnp.max(w_src_r, axis=1, keepdims=True))
            a_state = jnp.exp(b_last + m - m_new)
            w_c = jnp.exp(w_src_c - m_new)
            wv = (w_c * vaug.astype(F32)).astype(BF16)
            d_c = lax.dot_general(k, wv, (((0,), (0,)), ((), ())), preferred_element_type=F32)
            c_scr[idx] = a_state * caug + d_c
            m_scr[idx] = jnp.broadcast_to(m_new, m_scr.shape[1:])

    def scan(qkv_ref, g_ref, gt_ref, hf, hb, t):
        nc = t // L

        def body(i, carry):
            step(qkv_ref, g_ref, gt_ref, hf, pl.multiple_of(i * L, L), 0)
            step(qkv_ref, g_ref, gt_ref, hb, pl.multiple_of((nc - 1 - i) * L, L), 1)
            return carry

        lax.fori_loop(0, nc, body, 0)

    scan(qkv_c_ref, g_c_ref, gt_c_ref, hf_c, hb_c, t_c)
    scan(qkv_x_ref, g_x_ref, gt_x_ref, hf_x, hb_x, t_x)

    def finish(hf, hb, so_ref, out_ref, t):
        tile = 256

        def body(i, carry):
            r0 = pl.multiple_of(i * tile, tile)
            hsum = hf[pl.ds(r0, tile), :] + hb[pl.ds(r0, tile), :]
            for h in range(ML_HEADS):
                sl = slice(h * dh, (h + 1) * dh)
                z = hsum[:, sl]
                mu = jnp.mean(z, axis=-1, keepdims=True)
                zc = z - mu
                var = jnp.mean(zc * zc, axis=-1, keepdims=True)
                hn = zc * lax.rsqrt(var + LN_EPS) * mlg_ref[:, sl]
                out_ref[0, pl.ds(r0, tile), sl] = (so_ref[0, pl.ds(r0, tile), sl].astype(F32) * hn).astype(BF16)
            return carry

        lax.fori_loop(0, t // tile, body, 0)

    finish(hf_c, hb_c, so_c_ref, hn_c_ref, t_c)
    finish(hf_x, hb_x, so_x_ref, hn_x_ref, t_x)


def _mlstm_branch(pc, px, lw):
    bsz, t_c, _ = pc[0].shape
    t_x = px[0].shape[1]
    seq = lambda t, n: pl.BlockSpec((1, t, n), lambda b: (b, 0, 0))
    gts = lambda t: pl.BlockSpec((1, 4 * ML_HEADS, t), lambda b: (b, 0, 0))
    nst = 2 * ML_HEADS
    return pl.pallas_call(
        functools.partial(_mlstm_kernel, t_c=t_c, t_x=t_x),
        grid=(bsz,),
        in_specs=[seq(t_c, 3 * ML_DIM), seq(t_c, LANES), gts(t_c), seq(t_c, ML_DIM),
                  seq(t_x, 3 * ML_DIM), seq(t_x, LANES), gts(t_x), seq(t_x, ML_DIM),
                  _const_spec((1, ML_DIM))],
        out_specs=[seq(t_c, ML_DIM), seq(t_x, ML_DIM)],
        out_shape=[jax.ShapeDtypeStruct((bsz, t_c, ML_DIM), BF16),
                   jax.ShapeDtypeStruct((bsz, t_x, ML_DIM), BF16)],
        scratch_shapes=[pltpu.VMEM((t_c, ML_DIM), F32), pltpu.VMEM((t_c, ML_DIM), F32),
                        pltpu.VMEM((t_x, ML_DIM), F32), pltpu.VMEM((t_x, ML_DIM), F32),
                        pltpu.VMEM((nst, ML_HEAD_DIM, 2 * ML_HEAD_DIM), F32),
                        pltpu.VMEM((nst, 8, LANES), F32)],
        compiler_params=pltpu.CompilerParams(
            dimension_semantics=("parallel",), vmem_limit_bytes=VMEM_LIMIT),
    )(*pc, *px, lw["ml_g"])


def _attn_kernel(*refs, n_seg):
    q_ref = refs[0]
    k_refs = refs[1:1 + n_seg]
    v_refs = refs[1 + n_seg:1 + 2 * n_seg]
    o_ref = refs[1 + 2 * n_seg]
    tq = q_ref.shape[1]
    lane = lax.broadcasted_iota(jnp.int32, (tq, LANES), 1)
    for p in range(MLA_HEADS // 2):
        vs = slice(p * LANES, (p + 1) * LANES)
        outs = []
        for h in (2 * p, 2 * p + 1):
            hs = slice(h * HEAD_PAD, (h + 1) * HEAD_PAD)
            qh = q_ref[0, :, hs]
            ss = [lax.dot_general(qh, k_ref[0, :, hs], (((1,), (1,)), ((), ())), preferred_element_type=F32)
                  for k_ref in k_refs]
            m = functools.reduce(jnp.maximum, [jnp.max(s, axis=-1, keepdims=True) for s in ss])
            ps = [jnp.exp(s - m) for s in ss]
            denom = functools.reduce(jnp.add, [jnp.sum(pr, axis=-1, keepdims=True) for pr in ps])
            acc = functools.reduce(jnp.add, [jnp.dot(pr.astype(BF16), v_ref[0, :, vs], preferred_element_type=F32)
                                             for pr, v_ref in zip(ps, v_refs)])
            outs.append(acc * (1.0 / denom))
        o_ref[0, :, vs] = jnp.where(lane < V_HEAD, outs[0], outs[1]).astype(BF16)


def _attention(q, ks, vs, tq):
    bsz, t, hq = q.shape
    n_seg = len(ks)
    hv = MLA_HEADS * V_HEAD
    kspec = [pl.BlockSpec((1, k.shape[1], hq), lambda b, i: (b, 0, 0)) for k in ks]
    vspec = [pl.BlockSpec((1, v.shape[1], hv), lambda b, i: (b, 0, 0)) for v in vs]
    return pl.pallas_call(
        functools.partial(_attn_kernel, n_seg=n_seg),
        grid=(bsz, t // tq),
        in_specs=[pl.BlockSpec((1, tq, hq), lambda b, i: (b, i, 0))] + kspec + vspec,
        out_specs=pl.BlockSpec((1, tq, hv), lambda b, i: (b, i, 0)),
        out_shape=jax.ShapeDtypeStruct((bsz, t, hv), BF16),
        compiler_params=pltpu.CompilerParams(
            dimension_semantics=("parallel", "arbitrary"), vmem_limit_bytes=VMEM_LIMIT),
    )(q, *ks, *vs)


FF_CHUNK = 1024


def _merge_mlp_kernel(x_ref, hc_ref, hm_ref, oa_ref, gates_ref, g1_ref, sc2_ref, sh2_ref, g2_ref,
                      wc_ref, wm_ref, wa_ref, wo_ref, bo_ref, ln1g_ref, ln1b_ref,
                      w1_ref, b1_ref, w2_ref, b2_ref, ln2g_ref, ln2b_ref, o_ref):
    d = D_MODEL
    yc = jnp.dot(hc_ref[0], wc_ref[...], preferred_element_type=F32)
    ym = jnp.dot(hm_ref[0], wm_ref[...], preferred_element_type=F32)
    ya = jnp.dot(oa_ref[0], wa_ref[...], preferred_element_type=F32)
    mix = (gates_ref[0, :, 0:d].astype(F32) * yc + gates_ref[0, :, d:2 * d].astype(F32) * ym
           + gates_ref[0, :, 2 * d:3 * d].astype(F32) * ya)
    y = jnp.dot(mix.astype(BF16), wo_ref[...], preferred_element_type=F32) + bo_ref[...]
    x1 = _layer_norm(ALPHA * x_ref[0] + g1_ref[0] * y, ln1g_ref[...], ln1b_ref[...])
    u2 = (x1 * (1.0 + sc2_ref[0]) + sh2_ref[0]).astype(BF16)
    acc = jnp.zeros(x1.shape, F32)
    for j in range(D_FF // FF_CHUNK):
        sl = slice(j * FF_CHUNK, (j + 1) * FF_CHUNK)
        hdn = jnp.maximum(jnp.dot(u2, w1_ref[:, sl], preferred_element_type=F32) + b1_ref[:, sl], 0.0)
        acc = acc + jnp.dot((hdn * hdn).astype(BF16), w2_ref[sl, :], preferred_element_type=F32)
    mlp = acc + b2_ref[...]
    o_ref[0] = _layer_norm(ALPHA * x1 + g2_ref[0] * mlp, ln2g_ref[...], ln2b_ref[...])


def _merge_mlp(x, hc, hm, oa, gates, g1, sc2, sh2, g2, lw, tm):
    bsz, t, d = x.shape
    tok = lambda n: pl.BlockSpec((1, tm, n), lambda b, i: (b, i, 0))
    vec = pl.BlockSpec((1, 1, d), lambda b, i: (b, 0, 0))
    return pl.pallas_call(
        _merge_mlp_kernel,
        grid=(bsz, t // tm),
        in_specs=[tok(d), tok(CONV_DIM), tok(ML_DIM), tok(MLA_HEADS * V_HEAD), tok(3 * d), vec, vec, vec, vec,
                  _const_spec((CONV_DIM, d)), _const_spec((ML_DIM, d)), _const_spec((MLA_HEADS * V_HEAD, d)),
                  _const_spec((d, d)), _const_spec((1, d)), _const_spec((1, d)), _const_spec((1, d)),
                  _const_spec((d, D_FF)), _const_spec((1, D_FF)), _const_spec((D_FF, d)), _const_spec((1, d)),
                  _const_spec((1, d)), _const_spec((1, d))],
        out_specs=tok(d),
        out_shape=jax.ShapeDtypeStruct((bsz, t, d), F32),
        compiler_params=pltpu.CompilerParams(
            dimension_semantics=("parallel", "arbitrary"), vmem_limit_bytes=VMEM_LIMIT),
    )(x, hc, hm, oa, gates, g1, sc2, sh2, g2,
      lw["w_conv_out"], lw["w_ml_out"], lw["w_mla_out"], lw["w_out"], lw["b_out"], lw["ln1_g"], lw["ln1_b"],
      lw["w1"], lw["b1"], lw["w2"], lw["b2"], lw["ln2_g"], lw["ln2_b"])


def _head_block_perm():
    lanes = np.zeros((QK_ROPE,), np.int32)
    for a in range(2):
        for half in range(2):
            for f in range(ROPE_FREQ):
                lanes[a * 2 * ROPE_FREQ + half * ROPE_FREQ + f] = QK_NOPE + half * ROPE_HALF + a * ROPE_FREQ + f
    return lanes


def _layer_weights(l, w_in, b_in, w_dw, b_dw, cn_g, cn_b, w_conv_out, ml_g, w_ml_out, qn_g, w_uq, kvn_g, w_ukv,
                   w_mla_out, w_out, b_out, ln1_g, ln1_b, w1, b1, w2, b2, ln2_g, ln2_b):
    rope_lanes = _head_block_perm()
    wi, bi = w_in[l], b_in[l]
    d = D_MODEL

    def src(i):
        return wi[:, _SRC[i]:_SRC[i + 1]], bi[_SRC[i]:_SRC[i + 1]]

    (wa, ba), (wqkv, bqkv), (wo, bo), (wg, bg), (wcq, bcq), (wckv, bckv), (wkr, bkr), (wgt, bgt) = (
        src(i) for i in range(8))
    kr_w = jnp.zeros((d, HEAD_PAD), F32).at[:, rope_lanes].set(wkr)
    kr_b = jnp.zeros((HEAD_PAD,), F32).at[rope_lanes].set(bkr)
    g_w = jnp.zeros((d, LANES), F32).at[:, :4 * ML_HEADS].set(wg)
    g_b = jnp.zeros((LANES,), F32).at[:4 * ML_HEADS].set(bg)
    w_pack = jnp.concatenate([wa, wqkv, wo, wcq, wckv, kr_w, g_w, wgt], axis=1).astype(BF16)
    b_pack = jnp.concatenate([ba, bqkv, bo, bcq, bckv, kr_b, g_b, bgt])[None, :]

    wq = w_uq[l].reshape(Q_LORA, MLA_HEADS, QK_NOPE + QK_ROPE)
    wq_p = jnp.zeros((Q_LORA, MLA_HEADS, HEAD_PAD), F32)
    wq_p = wq_p.at[:, :, :QK_NOPE].set(wq[:, :, :QK_NOPE]).at[:, :, rope_lanes].set(wq[:, :, QK_NOPE:])
    wkv = w_ukv[l].reshape(KV_LORA, MLA_HEADS, QK_NOPE + V_HEAD)
    wkn_p = jnp.zeros((KV_LORA, MLA_HEADS, HEAD_PAD), F32).at[:, :, :QK_NOPE].set(wkv[:, :, :QK_NOPE])
    wv_p = wkv[:, :, QK_NOPE:]
    row = lambda z: z[l][None, :]
    return {
        "w_pack": w_pack, "b_pack": b_pack,
        "qn_g": row(qn_g), "w_uq": wq_p.reshape(Q_LORA, MLA_HEADS * HEAD_PAD).astype(BF16),
        "kvn_g": row(kvn_g), "w_kn": wkn_p.reshape(KV_LORA, MLA_HEADS * HEAD_PAD).astype(BF16),
        "w_v": wv_p.reshape(KV_LORA, MLA_HEADS * V_HEAD).astype(BF16),
        "w_dw": jnp.concatenate([w_dw[l], jnp.zeros((1, CONV_DIM), F32)], axis=0), "b_dw": row(b_dw),
        "cn_g": row(cn_g), "cn_b": row(cn_b),
        "ml_g": row(ml_g),
        "w_conv_out": w_conv_out[l].astype(BF16), "w_ml_out": w_ml_out[l].astype(BF16),
        "w_mla_out": w_mla_out[l].astype(BF16), "w_out": w_out[l].astype(BF16), "b_out": row(b_out),
        "ln1_g": row(ln1_g), "ln1_b": row(ln1_b), "w1": w1[l].astype(BF16), "b1": row(b1),
        "w2": w2[l].astype(BF16), "b2": row(b2), "ln2_g": row(ln2_g), "ln2_b": row(ln2_b),
    }


def _rope_tables(n_tokens):
    rows = n_tokens // GRID_W
    rr, cc = jnp.meshgrid(jnp.arange(rows, dtype=F32), jnp.arange(GRID_W, dtype=F32), indexing="ij")
    inv = ROPE_THETA ** (-jnp.arange(ROPE_FREQ, dtype=F32) / ROPE_FREQ)
    ang = jnp.stack([rr.reshape(-1), cc.reshape(-1)], -1)[..., None] * inv
    cos = jnp.cos(ang).reshape(n_tokens, ROPE_HALF)
    sin = jnp.sin(ang).reshape(n_tokens, ROPE_HALF)
    one = jnp.ones((n_tokens, QK_NOPE), F32)
    z_nope = jnp.zeros((n_tokens, QK_NOPE), F32)
    z_half = jnp.zeros((n_tokens, ROPE_HALF), F32)
    z_pad = jnp.zeros((n_tokens, HEAD_PAD - QK_NOPE - QK_ROPE), F32)
    tc = jnp.concatenate([one, cos, cos, z_pad], axis=1)
    tsm = jnp.concatenate([z_nope, -sin, z_half, z_pad], axis=1)
    tsp = jnp.concatenate([z_nope, z_half, sin, z_pad], axis=1)
    return tc, tsm, tsp


def _identity_tables(n_tokens):
    return (jnp.ones((n_tokens, HEAD_PAD), F32), jnp.zeros((n_tokens, HEAD_PAD), F32),
            jnp.zeros((n_tokens, HEAD_PAD), F32))


def kernel(x, c, ctx, c_ctx, w_mod, b_mod, w_in, b_in, w_dw, b_dw, conv_norm_g, conv_norm_b, w_conv_out,
           mlstm_norm_g, w_mlstm_out, q_norm_g, w_uq, kv_norm_g, w_ukv, w_mla_out, w_out, b_out,
           ln1_g, ln1_b, w1, b1, w2, b2, ln2_g, ln2_b):
    bsz, t_x, d = x.shape
    t_c = ctx.shape[1]
    n_rows = 24
    cc = jnp.zeros((n_rows, d), F32).at[:bsz].set(c).at[bsz].set(c_ctx)
    mod = _modulation(cc, w_mod, b_mod)
    rope_x = _rope_tables(t_x)
    rope_c = _identity_tables(t_c)

    for l in range(DEPTH):
        with_ctx = l < DEPTH - 1
        lw = _layer_weights(l, w_in, b_in, w_dw, b_dw, conv_norm_g, conv_norm_b, w_conv_out, mlstm_norm_g,
                            w_mlstm_out, q_norm_g, w_uq, kv_norm_g, w_ukv, w_mla_out, w_out, b_out,
                            ln1_g, ln1_b, w1, b1, w2, b2, ln2_g, ln2_b)
        mx = [mod[l, :bsz, i * d:(i + 1) * d][:, None, :] for i in range(6)]
        mc = [jnp.broadcast_to(mod[l, bsz, i * d:(i + 1) * d][None, None, :], (bsz, 1, d)) for i in range(6)]

        glu_x, qkv_x, so_x, g_x, gt_x, gates_x, q_x, k_x, v_x = _in_projection(x, mx[1], mx[0], lw, rope_x, 512)
        glu_c, qkv_c, so_c, g_c, gt_c, gates_c, q_c, k_c, v_c = _in_projection(ctx, mc[1], mc[0], lw, rope_c, 256)

        hc_x = _conv_branch(glu_x, lw)
        hm_c, hm_x = _mlstm_branch((qkv_c, g_c, gt_c, so_c), (qkv_x, g_x, gt_x, so_x), lw)
        oa_x = _attention(q_x, [k_c, k_x], [v_c, v_x], 256)
        x_new = _merge_mlp(x, hc_x, hm_x, oa_x, gates_x, mx[2], mx[4], mx[3], mx[5], lw, 256)
        if with_ctx:
            hc_c = _conv_branch(glu_c, lw)
            oa_c = _attention(q_c, [k_c], [v_c], 256)
            ctx = _merge_mlp(ctx, hc_c, hm_c, oa_c, gates_c, mc[2], mc[4], mc[3], mc[5], lw, 256)
        x = x_new
    return x
```

```python
import functools

import numpy as np
import jax
import jax.numpy as jnp
from jax import lax
from jax.experimental import pallas as pl
from jax.experimental.pallas import tpu as pltpu

F32 = jnp.float32
BF16 = jnp.bfloat16

D_MODEL = 1024
DEPTH = 2
GRID_W = 64
CONV_DIM = 512
CONV_K = 31
ML_HEADS = 4
ML_HEAD_DIM = 128
ML_DIM = ML_HEADS * ML_HEAD_DIM
ML_CHUNK = 128
MLA_HEADS = 8
QK_NOPE = 64
QK_ROPE = 32
V_HEAD = 64
Q_LORA = 768
KV_LORA = 256
ROPE_THETA = 10000.0
ROPE_FREQ = QK_ROPE // 4
MLA_SCALE = (QK_NOPE + QK_ROPE) ** -0.5
ML_SCALE = ML_HEAD_DIM ** -0.5
D_FF = 4 * D_MODEL
LN_EPS = 1e-5
ALPHA = (2 * DEPTH) ** 0.25

LANES = 128
HEAD_PAD = LANES
ROPE_HALF = QK_ROPE // 2
ONES_ROWS = 16
ML_AUG = ML_HEAD_DIM + ONES_ROWS
ATT_AUG = V_HEAD + ONES_ROWS
Q_SCALE = MLA_SCALE * float(np.log2(np.e))
VMEM_LIMIT = 56 * 1024 * 1024

OFF_A = 0
OFF_QKV = OFF_A + 2 * CONV_DIM
OFF_O = OFF_QKV + 3 * ML_DIM
OFF_CQ = OFF_O + ML_DIM
OFF_CKV = OFF_CQ + Q_LORA
OFF_KR = OFF_CKV + KV_LORA
OFF_GIF = OFF_KR + HEAD_PAD
OFF_GATES = OFF_GIF + LANES
N_PACK = OFF_GATES + 3 * D_MODEL

_COLS = (2 * CONV_DIM, 3 * ML_DIM, ML_DIM, 4 * ML_HEADS, Q_LORA, KV_LORA, QK_ROPE, 3 * D_MODEL)
_SRC = tuple(int(s) for s in np.cumsum((0,) + _COLS))


def _const_spec(shape):
    nd = len(shape)
    return pl.BlockSpec(shape, lambda *_: (0,) * nd, pipeline_mode=pl.Buffered(1))


def _sigmoid(z):
    return 1.0 / (1.0 + jnp.exp(-z))


def _log_sigmoid(z):
    return jnp.minimum(z, 0.0) - jnp.log1p(jnp.exp(-jnp.abs(z)))


def _layer_norm(z, g, b):
    mu = jnp.mean(z, axis=-1, keepdims=True)
    zc = z - mu
    var = jnp.mean(zc * zc, axis=-1, keepdims=True)
    return zc * lax.rsqrt(var + LN_EPS) * g + b


def _mod_kernel(c_ref, w_ref, b_ref, o_ref):
    c = c_ref[...]
    s = (c * _sigmoid(c)).astype(BF16)
    o_ref[0] = jnp.dot(s, w_ref[0].astype(BF16), preferred_element_type=F32) + b_ref[0]


def _modulation(cc, w_mod, b_mod):
    nl, d, n = w_mod.shape
    r = cc.shape[0]
    tn = 1024
    return pl.pallas_call(
        _mod_kernel,
        name="modulation",
        grid=(nl, n // tn),
        in_specs=[
            pl.BlockSpec((r, d), lambda l, j: (0, 0)),
            pl.BlockSpec((1, d, tn), lambda l, j: (l, 0, j)),
            pl.BlockSpec((1, 1, tn), lambda l, j: (l, 0, j)),
        ],
        out_specs=pl.BlockSpec((1, r, tn), lambda l, j: (l, 0, j)),
        out_shape=jax.ShapeDtypeStruct((nl, r, n), F32),
        compiler_params=pltpu.CompilerParams(
            dimension_semantics=("arbitrary", "arbitrary"), vmem_limit_bytes=VMEM_LIMIT),
    )(cc, w_mod, b_mod.reshape(nl, 1, n))


def _rope(z, tc, tsm, tsp):
    return z * tc + pltpu.roll(z, HEAD_PAD - ROPE_HALF, 1) * tsm + pltpu.roll(z, ROPE_HALF, 1) * tsp


def _inproj_kernel(x_ref, sc_ref, sh_ref, w_ref, b_ref, qng_ref, wuq_ref, kvg_ref, wkn_ref, wv_ref,
                   tc_ref, tsm_ref, tsp_ref,
                   glu_ref, qk_ref, vtm_ref, so_ref, gt_ref, gates_ref, q_ref, k_ref, vta_ref):
    u = (x_ref[0] * (1.0 + sc_ref[0]) + sh_ref[0]).astype(BF16)
    tm = u.shape[0]
    ones_rows = jnp.ones((ONES_ROWS, tm), BF16)

    def proj(off, n):
        return jnp.dot(u, w_ref[:, off:off + n], preferred_element_type=F32) + b_ref[:, off:off + n]

    a = proj(OFF_A, 2 * CONV_DIM)
    glu_ref[0] = (a[:, :CONV_DIM] * _sigmoid(a[:, CONV_DIM:])).astype(BF16)

    qk_ref[0, :, :ML_DIM] = proj(OFF_QKV, ML_DIM).astype(BF16)
    qk_ref[0, :, ML_DIM:] = (proj(OFF_QKV + ML_DIM, ML_DIM) * ML_SCALE).astype(BF16)
    vt = proj(OFF_QKV + 2 * ML_DIM, ML_DIM).T.astype(BF16)
    gt = proj(OFF_GIF, LANES).T
    for c in range(tm // ML_CHUNK):
        cs = slice(c * ML_CHUNK, (c + 1) * ML_CHUNK)
        gt_ref[0, c] = gt[:4 * ML_HEADS, cs]
        for h in range(ML_HEADS):
            vtm_ref[0, c, h * ML_AUG:h * ML_AUG + ML_HEAD_DIM, :] = vt[h * ML_HEAD_DIM:(h + 1) * ML_HEAD_DIM, cs]
            vtm_ref[0, c, h * ML_AUG + ML_HEAD_DIM:(h + 1) * ML_AUG, :] = ones_rows[:, cs]
    so_ref[0] = _sigmoid(proj(OFF_O, ML_DIM)).astype(BF16)
    for j in range(3):
        gates_ref[0, :, j * D_MODEL:(j + 1) * D_MODEL] = _sigmoid(
            proj(OFF_GATES + j * D_MODEL, D_MODEL)).astype(BF16)

    tc, tsm, tsp = tc_ref[...], tsm_ref[...], tsp_ref[...]

    cq = proj(OFF_CQ, Q_LORA)
    nq = cq * lax.rsqrt(jnp.mean(cq * cq, axis=-1, keepdims=True) + LN_EPS) * qng_ref[...]
    q = jnp.dot(nq.astype(BF16), wuq_ref[...], preferred_element_type=F32)
    for h in range(MLA_HEADS):
        sl = slice(h * HEAD_PAD, (h + 1) * HEAD_PAD)
        q_ref[0, :, sl] = (_rope(q[:, sl], tc, tsm, tsp) * Q_SCALE).astype(BF16)

    ckv = proj(OFF_CKV, KV_LORA)
    nkv = (ckv * lax.rsqrt(jnp.mean(ckv * ckv, axis=-1, keepdims=True) + LN_EPS) * kvg_ref[...]).astype(BF16)
    kn = jnp.dot(nkv, wkn_ref[...], preferred_element_type=F32)
    kr = _rope(proj(OFF_KR, HEAD_PAD), tc, tsm, tsp)
    for h in range(MLA_HEADS):
        sl = slice(h * HEAD_PAD, (h + 1) * HEAD_PAD)
        k_ref[0, :, sl] = (kn[:, sl] + kr).astype(BF16)
    vta = lax.dot_general(wv_ref[...], nkv, (((1,), (1,)), ((), ())),
                          preferred_element_type=F32).astype(BF16)
    for h in range(MLA_HEADS):
        vta_ref[0, h * ATT_AUG:h * ATT_AUG + V_HEAD, :] = vta[h * V_HEAD:(h + 1) * V_HEAD, :]
        vta_ref[0, h * ATT_AUG + V_HEAD:(h + 1) * ATT_AUG, :] = ones_rows


def _in_projection(x, sc, sh, lw, tables, tm):
    bsz, t, d = x.shape
    tok = lambda n: pl.BlockSpec((1, tm, n), lambda b, i: (b, i, 0))
    vec = pl.BlockSpec((1, 1, d), lambda b, i: (b, 0, 0))
    tab = pl.BlockSpec((tm, HEAD_PAD), lambda b, i: (i, 0))
    hq = MLA_HEADS * HEAD_PAD
    nc, cpt = t // ML_CHUNK, tm // ML_CHUNK
    out_shape = [
        jax.ShapeDtypeStruct((bsz, t, CONV_DIM), BF16),
        jax.ShapeDtypeStruct((bsz, t, 2 * ML_DIM), BF16),
        jax.ShapeDtypeStruct((bsz, nc, ML_HEADS * ML_AUG, ML_CHUNK), BF16),
        jax.ShapeDtypeStruct((bsz, t, ML_DIM), BF16),
        jax.ShapeDtypeStruct((bsz, nc, 4 * ML_HEADS, ML_CHUNK), F32),
        jax.ShapeDtypeStruct((bsz, t, 3 * D_MODEL), BF16),
        jax.ShapeDtypeStruct((bsz, t, hq), BF16),
        jax.ShapeDtypeStruct((bsz, t, hq), BF16),
        jax.ShapeDtypeStruct((bsz, MLA_HEADS * ATT_AUG, t), BF16),
    ]
    out_specs = [tok(CONV_DIM), tok(2 * ML_DIM),
                 pl.BlockSpec((1, cpt, ML_HEADS * ML_AUG, ML_CHUNK), lambda b, i: (b, i, 0, 0)),
                 tok(ML_DIM),
                 pl.BlockSpec((1, cpt, 4 * ML_HEADS, ML_CHUNK), lambda b, i: (b, i, 0, 0)),
                 tok(3 * D_MODEL), tok(hq), tok(hq),
                 pl.BlockSpec((1, MLA_HEADS * ATT_AUG, tm), lambda b, i: (b, 0, i))]
    return pl.pallas_call(
        _inproj_kernel,
        name="in_projection",
        grid=(bsz, t // tm),
        in_specs=[tok(d), vec, vec,
                  _const_spec((d, N_PACK)), _const_spec((1, N_PACK)),
                  _const_spec((1, Q_LORA)), _const_spec((Q_LORA, hq)),
                  _const_spec((1, KV_LORA)), _const_spec((KV_LORA, hq)),
                  _const_spec((MLA_HEADS * V_HEAD, KV_LORA)),
                  tab, tab, tab],
        out_specs=out_specs,
        out_shape=out_shape,
        compiler_params=pltpu.CompilerParams(
            dimension_semantics=("parallel", "arbitrary"), vmem_limit_bytes=VMEM_LIMIT),
    )(x, sc, sh, lw["w_pack"], lw["b_pack"], lw["qn_g"], lw["w_uq"], lw["kvn_g"], lw["w_kn"], lw["w_v"],
      *tables)


CONV_HALO = 16
CONV_TILE = 128


def _conv_kernel(h_ref, wdw_ref, bdw_ref, g_ref, b_ref, o_ref, hp, stage, cv, *, t):
    nt = t // CONV_TILE
    hp[0:CONV_HALO, :] = jnp.zeros((CONV_HALO, CONV_DIM), F32)
    hp[t + CONV_HALO:t + 2 * CONV_HALO, :] = jnp.zeros((CONV_HALO, CONV_DIM), F32)

    def fill(i, carry):
        r0 = pl.multiple_of(i * CONV_TILE, CONV_TILE)
        hp[pl.ds(r0 + CONV_HALO, CONV_TILE), :] = h_ref[0, pl.ds(r0, CONV_TILE), :].astype(F32)
        return carry

    lax.fori_loop(0, nt, fill, 0)

    def tile(i, carry):
        r0 = pl.multiple_of(i * CONV_TILE, CONV_TILE)
        stage[...] = hp[pl.ds(r0, CONV_TILE + 2 * CONV_HALO), :]
        for cg in range(CONV_DIM // LANES):
            sl = slice(cg * LANES, (cg + 1) * LANES)
            acc = jnp.zeros((CONV_TILE, LANES), F32)
            for k in range(CONV_K):
                off = CONV_HALO - CONV_K // 2 + k
                acc = acc + stage[off:off + CONV_TILE, sl] * wdw_ref[k:k + 1, sl]
            cv[:, sl] = acc + bdw_ref[:, sl]
        z = _layer_norm(cv[...], g_ref[...], b_ref[...])
        o_ref[0, pl.ds(r0, CONV_TILE), :] = (z * _sigmoid(z)).astype(BF16)
        return carry

    lax.fori_loop(0, nt, tile, 0)


def _conv_branch(glu, lw):
    bsz, t, _ = glu.shape
    return pl.pallas_call(
        functools.partial(_conv_kernel, t=t),
        name="conv",
        grid=(bsz,),
        in_specs=[pl.BlockSpec((1, t, CONV_DIM), lambda b: (b, 0, 0)),
                  _const_spec((CONV_K + 1, CONV_DIM)), _const_spec((1, CONV_DIM)),
                  _const_spec((1, CONV_DIM)), _const_spec((1, CONV_DIM))],
        out_specs=pl.BlockSpec((1, t, CONV_DIM), lambda b: (b, 0, 0)),
        out_shape=jax.ShapeDtypeStruct((bsz, t, CONV_DIM), BF16),
        scratch_shapes=[pltpu.VMEM((t + 2 * CONV_HALO, CONV_DIM), F32),
                        pltpu.VMEM((CONV_TILE + 2 * CONV_HALO, CONV_DIM), F32),
                        pltpu.VMEM((CONV_TILE, CONV_DIM), F32)],
        compiler_params=pltpu.CompilerParams(
            dimension_semantics=("parallel",), vmem_limit_bytes=VMEM_LIMIT),
    )(glu, lw["w_dw"], lw["b_dw"], lw["cn_g"], lw["cn_b"])


N_CHAIN = 2 * ML_HEADS


def _mlstm_kernel(qk_c_ref, vt_c_ref, gt_c_ref, so_c_ref, qk_x_ref, vt_x_ref, gt_x_ref, so_x_ref, mlg_ref,
                  hn_c_ref, hn_x_ref, hf_c, hb_c, hf_x, hb_x, c_scr, m_scr, *, nc_c, nc_x):
    L = ML_CHUNK
    dh = ML_HEAD_DIM
    row = lax.broadcasted_iota(jnp.int32, (L, L), 0)
    col = lax.broadcasted_iota(jnp.int32, (L, L), 1)
    upper = row <= col
    lower = row >= col
    tri2 = jnp.concatenate([upper.astype(F32), lower.astype(F32)], axis=1)
    is_fwd = lax.broadcasted_iota(jnp.int32, (N_CHAIN, L), 0) < ML_HEADS

    c_scr[...] = jnp.zeros(c_scr.shape, F32)
    m_scr[...] = jnp.zeros(m_scr.shape, F32)

    def iteration(qk_ref, vt_ref, gt_ref, hf, hb, cf, cb):
        gf = gt_ref[0, cf]
        gb = gt_ref[0, cb]
        li = jnp.where(is_fwd, gf[:N_CHAIN], gb[:N_CHAIN])
        lf = _log_sigmoid(jnp.where(is_fwd, gf[N_CHAIN:], gb[N_CHAIN:]))
        cum2 = jnp.dot(lf, tri2, preferred_element_type=F32, precision=lax.Precision.HIGHEST)
        bc = jnp.where(is_fwd, cum2[:, :L], cum2[:, L:])
        r = li - bc
        b_last = jnp.sum(lf, axis=1, keepdims=True)
        m = m_scr[:, 0:1]
        w_src = b_last + r
        m_new = jnp.maximum(b_last + m, jnp.max(w_src, axis=1, keepdims=True))
        a_state = jnp.exp(b_last + m - m_new)
        w = jnp.exp(w_src - m_new)
        inter = bc + m
        r_cols = jnp.concatenate([r, jnp.zeros((L - N_CHAIN, L), F32)], axis=0).T
        m_scr[...] = jnp.broadcast_to(m_new, m_scr.shape)
        for j in range(N_CHAIN):
            d, h = divmod(j, ML_HEADS)
            cc = cf if d == 0 else cb
            mask = upper if d == 0 else lower
            r0 = pl.multiple_of(cc * L, L)
            log_d = jnp.where(mask, bc[j:j + 1, :] + r_cols[:, j:j + 1], -jnp.inf)
            m_row = jnp.maximum(inter[j:j + 1, :], jnp.max(log_d, axis=0, keepdims=True))
            a_inter = jnp.exp(inter[j:j + 1, :] - m_row)
            dmat = jnp.exp(log_d - m_row)
            q = qk_ref[0, pl.ds(r0, L), h * dh:(h + 1) * dh]
            k = qk_ref[0, pl.ds(r0, L), ML_DIM + h * dh:ML_DIM + (h + 1) * dh]
            vt = vt_ref[0, cc, h * ML_AUG:(h + 1) * ML_AUG, :]
            ct = c_scr[j]
            both = lax.dot_general(jnp.concatenate([k, ct.astype(BF16)], axis=0), q, (((1,), (1,)), ((), ())),
                                   preferred_element_type=F32)
            pt = (both[:L] * dmat).astype(BF16)
            num_aug = a_inter * both[L:] + jnp.dot(vt, pt, preferred_element_type=F32)
            den = num_aug[dh:dh + 1, :]
            hs = hf if d == 0 else hb
            hs[cc, h * dh:(h + 1) * dh, :] = num_aug[:dh] / jnp.maximum(jnp.abs(den), jnp.exp(-m_row))
            wv = (vt.astype(F32) * w[j:j + 1, :]).astype(BF16)
            c_scr[j] = a_state[j:j + 1, :] * ct + jnp.dot(wv, k, preferred_element_type=F32)

    def scan(qk_ref, vt_ref, gt_ref, hf, hb, nc):
        def body(i, carry):
            iteration(qk_ref, vt_ref, gt_ref, hf, hb, i, nc - 1 - i)
            return carry

        lax.fori_loop(0, nc, body, 0)

    scan(qk_c_ref, vt_c_ref, gt_c_ref, hf_c, hb_c, nc_c)
    scan(qk_x_ref, vt_x_ref, gt_x_ref, hf_x, hb_x, nc_x)

    def finish(hf, hb, so_ref, out_ref, nc):
        def body(c, carry):
            r0 = pl.multiple_of(c * L, L)
            hsum = hf[c] + hb[c]
            for h in range(ML_HEADS):
                sl = slice(h * dh, (h + 1) * dh)
                z = hsum[sl, :]
                mu = jnp.mean(z, axis=0, keepdims=True)
                zc = z - mu
                var = jnp.mean(zc * zc, axis=0, keepdims=True)
                hn = (zc * lax.rsqrt(var + LN_EPS)).T * mlg_ref[:, sl]
                out_ref[0, pl.ds(r0, L), sl] = (so_ref[0, pl.ds(r0, L), sl].astype(F32) * hn).astype(BF16)
            return carry

        lax.fori_loop(0, nc, body, 0)

    finish(hf_c, hb_c, so_c_ref, hn_c_ref, nc_c)
    finish(hf_x, hb_x, so_x_ref, hn_x_ref, nc_x)


def _mlstm_branch(pc, px, lw):
    bsz, t_c, _ = pc[0].shape
    t_x = px[0].shape[1]
    nc_c, nc_x = t_c // ML_CHUNK, t_x // ML_CHUNK
    seq = lambda t, n: pl.BlockSpec((1, t, n), lambda b: (b, 0, 0))
    chunked = lambda nc, n: pl.BlockSpec((1, nc, n, ML_CHUNK), lambda b: (b, 0, 0, 0))
    ins = lambda t, nc: [seq(t, 2 * ML_DIM), chunked(nc, ML_HEADS * ML_AUG), chunked(nc, 4 * ML_HEADS),
                         seq(t, ML_DIM)]
    hbuf = lambda nc: pltpu.VMEM((nc, ML_DIM, ML_CHUNK), F32)
    return pl.pallas_call(
        functools.partial(_mlstm_kernel, nc_c=nc_c, nc_x=nc_x),
        name="mlstm",
        grid=(bsz,),
        in_specs=ins(t_c, nc_c) + ins(t_x, nc_x) + [_const_spec((1, ML_DIM))],
        out_specs=[seq(t_c, ML_DIM), seq(t_x, ML_DIM)],
        out_shape=[jax.ShapeDtypeStruct((bsz, t_c, ML_DIM), BF16),
                   jax.ShapeDtypeStruct((bsz, t_x, ML_DIM), BF16)],
        scratch_shapes=[hbuf(nc_c), hbuf(nc_c), hbuf(nc_x), hbuf(nc_x),
                        pltpu.VMEM((N_CHAIN, ML_AUG, ML_HEAD_DIM), F32),
                        pltpu.VMEM((N_CHAIN, LANES), F32)],
        compiler_params=pltpu.CompilerParams(
            dimension_semantics=("parallel",), vmem_limit_bytes=VMEM_LIMIT),
    )(*pc, *px, lw["ml_g"])


def _attn_kernel(*refs, n_seg):
    q_ref = refs[0]
    k_refs = refs[1:1 + n_seg]
    vt_refs = refs[1 + n_seg:1 + 2 * n_seg]
    o_ref = refs[1 + 2 * n_seg]
    def scores(h):
        hs = slice(h * HEAD_PAD, (h + 1) * HEAD_PAD)
        qh = q_ref[0, :, hs]
        return [lax.dot_general(k_ref[0, :, hs], qh, (((1,), (1,)), ((), ())), preferred_element_type=F32)
                for k_ref in k_refs]

    outs = []
    nxt = scores(0)
    for h in range(MLA_HEADS):
        sts, nxt = nxt, (scores(h + 1) if h + 1 < MLA_HEADS else None)
        m = functools.reduce(jnp.maximum, [jnp.max(st, axis=0, keepdims=True) for st in sts])
        o_aug = functools.reduce(jnp.add, [
            jnp.dot(vt_ref[0, h * ATT_AUG:(h + 1) * ATT_AUG, :], jnp.exp2(st - m).astype(BF16),
                    preferred_element_type=F32)
            for st, vt_ref in zip(sts, vt_refs)])
        outs.append(o_aug[:V_HEAD] * (1.0 / o_aug[V_HEAD:V_HEAD + 1]))
    o_ref[0] = jnp.concatenate(outs, axis=0).T.astype(BF16)


def _attention(q, ks, vts, tq):
    bsz, t, hq = q.shape
    n_seg = len(ks)
    hv = MLA_HEADS * V_HEAD
    kspec = [pl.BlockSpec((1, k.shape[1], hq), lambda b, i: (b, 0, 0)) for k in ks]
    vspec = [pl.BlockSpec((1, MLA_HEADS * ATT_AUG, vt.shape[2]), lambda b, i: (b, 0, 0)) for vt in vts]
    return pl.pallas_call(
        functools.partial(_attn_kernel, n_seg=n_seg),
        name="attention",
        grid=(bsz, t // tq),
        in_specs=[pl.BlockSpec((1, tq, hq), lambda b, i: (b, i, 0))] + kspec + vspec,
        out_specs=pl.BlockSpec((1, tq, hv), lambda b, i: (b, i, 0)),
        out_shape=jax.ShapeDtypeStruct((bsz, t, hv), BF16),
        compiler_params=pltpu.CompilerParams(
            dimension_semantics=("parallel", "arbitrary"), vmem_limit_bytes=VMEM_LIMIT),
    )(q, *ks, *vts)


FF_CHUNK = 1024


def _merge_mlp_kernel(x_ref, hc_ref, hm_ref, oa_ref, gates_ref, g1_ref, sc2_ref, sh2_ref, g2_ref,
                      wc_ref, wm_ref, wa_ref, wo_ref, bo_ref, ln1g_ref, ln1b_ref,
                      w1_ref, b1_ref, w2_ref, b2_ref, ln2g_ref, ln2b_ref, o_ref):
    d = D_MODEL
    yc = jnp.dot(hc_ref[0], wc_ref[...], preferred_element_type=F32)
    ym = jnp.dot(hm_ref[0], wm_ref[...], preferred_element_type=F32)
    ya = jnp.dot(oa_ref[0], wa_ref[...], preferred_element_type=F32)
    mix = (gates_ref[0, :, 0:d].astype(F32) * yc + gates_ref[0, :, d:2 * d].astype(F32) * ym
           + gates_ref[0, :, 2 * d:3 * d].astype(F32) * ya)
    y = jnp.dot(mix.astype(BF16), wo_ref[...], preferred_element_type=F32) + bo_ref[...]
    x1 = _layer_norm(ALPHA * x_ref[0] + g1_ref[0] * y, ln1g_ref[...], ln1b_ref[...])
    u2 = (x1 * (1.0 + sc2_ref[0]) + sh2_ref[0]).astype(BF16)
    acc = jnp.zeros(x1.shape, F32)
    for j in range(D_FF // FF_CHUNK):
        sl = slice(j * FF_CHUNK, (j + 1) * FF_CHUNK)
        hdn = jnp.maximum(jnp.dot(u2, w1_ref[:, sl], preferred_element_type=F32) + b1_ref[:, sl], 0.0)
        acc = acc + jnp.dot((hdn * hdn).astype(BF16), w2_ref[sl, :], preferred_element_type=F32)
    mlp = acc + b2_ref[...]
    o_ref[0] = _layer_norm(ALPHA * x1 + g2_ref[0] * mlp, ln2g_ref[...], ln2b_ref[...])


def _merge_mlp(x, hc, hm, oa, gates, g1, sc2, sh2, g2, lw, tm):
    bsz, t, d = x.shape
    tok = lambda n: pl.BlockSpec((1, tm, n), lambda b, i: (b, i, 0))
    vec = pl.BlockSpec((1, 1, d), lambda b, i: (b, 0, 0))
    return pl.pallas_call(
        _merge_mlp_kernel,
        name="merge_mlp",
        grid=(bsz, t // tm),
        in_specs=[tok(d), tok(CONV_DIM), tok(ML_DIM), tok(MLA_HEADS * V_HEAD), tok(3 * d), vec, vec, vec, vec,
                  _const_spec((CONV_DIM, d)), _const_spec((ML_DIM, d)), _const_spec((MLA_HEADS * V_HEAD, d)),
                  _const_spec((d, d)), _const_spec((1, d)), _const_spec((1, d)), _const_spec((1, d)),
                  _const_spec((d, D_FF)), _const_spec((1, D_FF)), _const_spec((D_FF, d)), _const_spec((1, d)),
                  _const_spec((1, d)), _const_spec((1, d))],
        out_specs=tok(d),
        out_shape=jax.ShapeDtypeStruct((bsz, t, d), F32),
        compiler_params=pltpu.CompilerParams(
            dimension_semantics=("parallel", "arbitrary"), vmem_limit_bytes=VMEM_LIMIT),
    )(x, hc, hm, oa, gates, g1, sc2, sh2, g2,
      lw["w_conv_out"], lw["w_ml_out"], lw["w_mla_out"], lw["w_out"], lw["b_out"], lw["ln1_g"], lw["ln1_b"],
      lw["w1"], lw["b1"], lw["w2"], lw["b2"], lw["ln2_g"], lw["ln2_b"])


def _head_block_perm():
    lanes = np.zeros((QK_ROPE,), np.int32)
    for a in range(2):
        for half in range(2):
            for f in range(ROPE_FREQ):
                lanes[a * 2 * ROPE_FREQ + half * ROPE_FREQ + f] = QK_NOPE + half * ROPE_HALF + a * ROPE_FREQ + f
    return lanes


def _layer_weights(l, w_in, b_in, w_dw, b_dw, cn_g, cn_b, w_conv_out, ml_g, w_ml_out, qn_g, w_uq, kvn_g, w_ukv,
                   w_mla_out, w_out, b_out, ln1_g, ln1_b, w1, b1, w2, b2, ln2_g, ln2_b):
    rope_lanes = _head_block_perm()
    wi, bi = w_in[l], b_in[l]
    d = D_MODEL

    def src(i):
        return wi[:, _SRC[i]:_SRC[i + 1]], bi[_SRC[i]:_SRC[i + 1]]

    (wa, ba), (wqkv, bqkv), (wo, bo), (wg, bg), (wcq, bcq), (wckv, bckv), (wkr, bkr), (wgt, bgt) = (
        src(i) for i in range(8))
    kr_w = jnp.zeros((d, HEAD_PAD), F32).at[:, rope_lanes].set(wkr)
    kr_b = jnp.zeros((HEAD_PAD,), F32).at[rope_lanes].set(bkr)
    gate_cols = np.arange(4 * ML_HEADS).reshape(4, ML_HEADS)[[0, 2, 1, 3]].reshape(-1)
    g_w = jnp.zeros((d, LANES), F32).at[:, :4 * ML_HEADS].set(wg[:, gate_cols])
    g_b = jnp.zeros((LANES,), F32).at[:4 * ML_HEADS].set(bg[gate_cols])
    w_pack = jnp.concatenate([wa, wqkv, wo, wcq, wckv, kr_w, g_w, wgt], axis=1).astype(BF16)
    b_pack = jnp.concatenate([ba, bqkv, bo, bcq, bckv, kr_b, g_b, bgt])[None, :]

    wq = w_uq[l].reshape(Q_LORA, MLA_HEADS, QK_NOPE + QK_ROPE)
    wq_p = jnp.zeros((Q_LORA, MLA_HEADS, HEAD_PAD), F32)
    wq_p = wq_p.at[:, :, :QK_NOPE].set(wq[:, :, :QK_NOPE]).at[:, :, rope_lanes].set(wq[:, :, QK_NOPE:])
    wkv = w_ukv[l].reshape(KV_LORA, MLA_HEADS, QK_NOPE + V_HEAD)
    wkn_p = jnp.zeros((KV_LORA, MLA_HEADS, HEAD_PAD), F32).at[:, :, :QK_NOPE].set(wkv[:, :, :QK_NOPE])
    wv_p = wkv[:, :, QK_NOPE:]
    row = lambda z: z[l][None, :]
    return {
        "w_pack": w_pack, "b_pack": b_pack,
        "qn_g": row(qn_g), "w_uq": wq_p.reshape(Q_LORA, MLA_HEADS * HEAD_PAD).astype(BF16),
        "kvn_g": row(kvn_g), "w_kn": wkn_p.reshape(KV_LORA, MLA_HEADS * HEAD_PAD).astype(BF16),
        "w_v": wv_p.reshape(KV_LORA, MLA_HEADS * V_HEAD).T.astype(BF16),
        "w_dw": jnp.concatenate([w_dw[l], jnp.zeros((1, CONV_DIM), F32)], axis=0), "b_dw": row(b_dw),
        "cn_g": row(cn_g), "cn_b": row(cn_b),
        "ml_g": row(ml_g),
        "w_conv_out": w_conv_out[l].astype(BF16), "w_ml_out": w_ml_out[l].astype(BF16),
        "w_mla_out": w_mla_out[l].astype(BF16), "w_out": w_out[l].astype(BF16), "b_out": row(b_out),
        "ln1_g": row(ln1_g), "ln1_b": row(ln1_b), "w1": w1[l].astype(BF16), "b1": row(b1),
        "w2": w2[l].astype(BF16), "b2": row(b2), "ln2_g": row(ln2_g), "ln2_b": row(ln2_b),
    }


def _rope_tables(n_tokens):
    rows = n_tokens // GRID_W
    rr, cc = jnp.meshgrid(jnp.arange(rows, dtype=F32), jnp.arange(GRID_W, dtype=F32), indexing="ij")
    inv = ROPE_THETA ** (-jnp.arange(ROPE_FREQ, dtype=F32) / ROPE_FREQ)
    ang = jnp.stack([rr.reshape(-1), cc.reshape(-1)], -1)[..., None] * inv
    cos = jnp.cos(ang).reshape(n_tokens, ROPE_HALF)
    sin = jnp.sin(ang).reshape(n_tokens, ROPE_HALF)
    one = jnp.ones((n_tokens, QK_NOPE), F32)
    z_nope = jnp.zeros((n_tokens, QK_NOPE), F32)
    z_half = jnp.zeros((n_tokens, ROPE_HALF), F32)
    z_pad = jnp.zeros((n_tokens, HEAD_PAD - QK_NOPE - QK_ROPE), F32)
    tc = jnp.concatenate([one, cos, cos, z_pad], axis=1)
    tsm = jnp.concatenate([z_nope, -sin, z_half, z_pad], axis=1)
    tsp = jnp.concatenate([z_nope, z_half, sin, z_pad], axis=1)
    return tc, tsm, tsp


def _identity_tables(n_tokens):
    return (jnp.ones((n_tokens, HEAD_PAD), F32), jnp.zeros((n_tokens, HEAD_PAD), F32),
            jnp.zeros((n_tokens, HEAD_PAD), F32))


def kernel(x, c, ctx, c_ctx, w_mod, b_mod, w_in, b_in, w_dw, b_dw, conv_norm_g, conv_norm_b, w_conv_out,
           mlstm_norm_g, w_mlstm_out, q_norm_g, w_uq, kv_norm_g, w_ukv, w_mla_out, w_out, b_out,
           ln1_g, ln1_b, w1, b1, w2, b2, ln2_g, ln2_b):
    bsz, t_x, d = x.shape
    t_c = ctx.shape[1]
    n_rows = 24
    cc = jnp.zeros((n_rows, d), F32).at[:bsz].set(c).at[bsz].set(c_ctx)
    mod = _modulation(cc, w_mod, b_mod)
    rope_x = _rope_tables(t_x)
    rope_c = _identity_tables(t_c)

    for l in range(DEPTH):
        with_ctx = l < DEPTH - 1
        lw = _layer_weights(l, w_in, b_in, w_dw, b_dw, conv_norm_g, conv_norm_b, w_conv_out, mlstm_norm_g,
                            w_mlstm_out, q_norm_g, w_uq, kv_norm_g, w_ukv, w_mla_out, w_out, b_out,
                            ln1_g, ln1_b, w1, b1, w2, b2, ln2_g, ln2_b)
        mx = [mod[l, :bsz, i * d:(i + 1) * d][:, None, :] for i in range(6)]
        mc = [jnp.broadcast_to(mod[l, bsz, i * d:(i + 1) * d][None, None, :], (bsz, 1, d)) for i in range(6)]

        glu_x, qk_x, vtm_x, so_x, gt_x, gates_x, q_x, k_x, vta_x = _in_projection(x, mx[1], mx[0], lw, rope_x, 512)
        glu_c, qk_c, vtm_c, so_c, gt_c, gates_c, q_c, k_c, vta_c = _in_projection(ctx, mc[1], mc[0], lw, rope_c, 256)

        hc_x = _conv_branch(glu_x, lw)
        hm_c, hm_x = _mlstm_branch((qk_c, vtm_c, gt_c, so_c), (qk_x, vtm_x, gt_x, so_x), lw)
        oa_x = _attention(q_x, [k_c, k_x], [vta_c, vta_x], 512)
        x_new = _merge_mlp(x, hc_x, hm_x, oa_x, gates_x, mx[2], mx[4], mx[3], mx[5], lw, 256)
        if with_ctx:
            hc_c = _conv_branch(glu_c, lw)
            oa_c = _attention(q_c, [k_c], [vta_c], 256)
            ctx = _merge_mlp(ctx, hc_c, hm_c, oa_c, gates_c, mc[2], mc[4], mc[3], mc[5], lw, 256)
        x = x_new
    return x
```

```python
import functools

import numpy as np
import jax
import jax.numpy as jnp
from jax import lax
from jax.experimental import pallas as pl
from jax.experimental.pallas import tpu as pltpu

F32 = jnp.float32
BF16 = jnp.bfloat16

D_MODEL = 1024
DEPTH = 2
GRID_W = 64
CONV_DIM = 512
CONV_K = 31
ML_HEADS = 4
ML_HEAD_DIM = 128
ML_DIM = ML_HEADS * ML_HEAD_DIM
ML_CHUNK = 128
MLA_HEADS = 8
QK_NOPE = 64
QK_ROPE = 32
V_HEAD = 64
Q_LORA = 768
KV_LORA = 256
ROPE_THETA = 10000.0
ROPE_FREQ = QK_ROPE // 4
MLA_SCALE = (QK_NOPE + QK_ROPE) ** -0.5
ML_SCALE = ML_HEAD_DIM ** -0.5
D_FF = 4 * D_MODEL
LN_EPS = 1e-5
ALPHA = (2 * DEPTH) ** 0.25

LANES = 128
SUBLANES = 8
HEAD_PAD = LANES
ROPE_HALF = QK_ROPE // 2
ONES_ROWS = 16
ML_AUG = ML_HEAD_DIM + ONES_ROWS
ATT_AUG = V_HEAD + ONES_ROWS
Q_SCALE = MLA_SCALE * float(np.log2(np.e))
VMEM_LIMIT = 56 * 1024 * 1024

OFF_A = 0
OFF_QKV = OFF_A + 2 * CONV_DIM
OFF_O = OFF_QKV + 3 * ML_DIM
OFF_CQ = OFF_O + ML_DIM
OFF_CKV = OFF_CQ + Q_LORA
OFF_KR = OFF_CKV + KV_LORA
OFF_GIF = OFF_KR + HEAD_PAD
OFF_GATES = OFF_GIF + LANES
N_PACK = OFF_GATES + 3 * D_MODEL

_COLS = (2 * CONV_DIM, 3 * ML_DIM, ML_DIM, 4 * ML_HEADS, Q_LORA, KV_LORA, QK_ROPE, 3 * D_MODEL)
_SRC = tuple(int(s) for s in np.cumsum((0,) + _COLS))


def _const_spec(shape):
    nd = len(shape)
    return pl.BlockSpec(shape, lambda *_: (0,) * nd, pipeline_mode=pl.Buffered(1))


def _sigmoid(z):
    return 1.0 / (1.0 + jnp.exp(-z))


def _log_sigmoid(z):
    return jnp.minimum(z, 0.0) - jnp.log1p(jnp.exp(-jnp.abs(z)))


def _layer_norm(z, g, b):
    mu = jnp.mean(z, axis=-1, keepdims=True)
    zc = z - mu
    var = jnp.mean(zc * zc, axis=-1, keepdims=True)
    return zc * lax.rsqrt(var + LN_EPS) * g + b


def _mod_kernel(c_ref, w_ref, b_ref, o_ref):
    c = c_ref[...]
    s = (c * _sigmoid(c)).astype(BF16)
    o_ref[0] = jnp.dot(s, w_ref[0].astype(BF16), preferred_element_type=F32) + b_ref[0]


def _modulation(cc, w_mod, b_mod):
    nl, d, n = w_mod.shape
    r = cc.shape[0]
    tn = 1024
    return pl.pallas_call(
        _mod_kernel,
        name="modulation",
        grid=(nl, n // tn),
        in_specs=[
            pl.BlockSpec((r, d), lambda l, j: (0, 0)),
            pl.BlockSpec((1, d, tn), lambda l, j: (l, 0, j)),
            pl.BlockSpec((1, 1, tn), lambda l, j: (l, 0, j)),
        ],
        out_specs=pl.BlockSpec((1, r, tn), lambda l, j: (l, 0, j)),
        out_shape=jax.ShapeDtypeStruct((nl, r, n), F32),
        compiler_params=pltpu.CompilerParams(
            dimension_semantics=("arbitrary", "arbitrary"), vmem_limit_bytes=VMEM_LIMIT),
    )(cc, w_mod, b_mod.reshape(nl, 1, n))


def _rope(z, tc, tsm, tsp):
    return z * tc + pltpu.roll(z, HEAD_PAD - ROPE_HALF, 1) * tsm + pltpu.roll(z, ROPE_HALF, 1) * tsp


def _inproj_kernel(x_ref, sc_ref, sh_ref, w_ref, b_ref, qng_ref, wuq_ref, kvg_ref, wkn_ref, wv_ref,
                   tc_ref, tsm_ref, tsp_ref,
                   glu_ref, qk_ref, vtm_ref, so_ref, gt_ref, gates_ref, q_ref, k_ref, vta_ref):
    u = (x_ref[0] * (1.0 + sc_ref[0]) + sh_ref[0]).astype(BF16)
    tm = u.shape[0]
    ones_rows = jnp.ones((ONES_ROWS, tm), BF16)

    def proj(off, n):
        return jnp.dot(u, w_ref[:, off:off + n], preferred_element_type=F32) + b_ref[:, off:off + n]

    a = proj(OFF_A, 2 * CONV_DIM)
    glu_ref[0] = (a[:, :CONV_DIM] * _sigmoid(a[:, CONV_DIM:])).astype(BF16)

    qk_ref[0, :, :ML_DIM] = proj(OFF_QKV, ML_DIM).astype(BF16)
    qk_ref[0, :, ML_DIM:] = (proj(OFF_QKV + ML_DIM, ML_DIM) * ML_SCALE).astype(BF16)
    vt = proj(OFF_QKV + 2 * ML_DIM, ML_DIM).T.astype(BF16)
    gt = proj(OFF_GIF, LANES).T
    for c in range(tm // ML_CHUNK):
        cs = slice(c * ML_CHUNK, (c + 1) * ML_CHUNK)
        gt_ref[0, c] = gt[:4 * ML_HEADS, cs]
        for h in range(ML_HEADS):
            vtm_ref[0, c, h * ML_AUG:h * ML_AUG + ML_HEAD_DIM, :] = vt[h * ML_HEAD_DIM:(h + 1) * ML_HEAD_DIM, cs]
            vtm_ref[0, c, h * ML_AUG + ML_HEAD_DIM:(h + 1) * ML_AUG, :] = ones_rows[:, cs]
    so_ref[0] = _sigmoid(proj(OFF_O, ML_DIM)).astype(BF16)
    for j in range(3):
        gates_ref[0, :, j * D_MODEL:(j + 1) * D_MODEL] = _sigmoid(
            proj(OFF_GATES + j * D_MODEL, D_MODEL)).astype(BF16)

    tc, tsm, tsp = tc_ref[...], tsm_ref[...], tsp_ref[...]

    cq = proj(OFF_CQ, Q_LORA)
    nq = cq * lax.rsqrt(jnp.mean(cq * cq, axis=-1, keepdims=True) + LN_EPS) * qng_ref[...]
    q = jnp.dot(nq.astype(BF16), wuq_ref[...], preferred_element_type=F32)
    for h in range(MLA_HEADS):
        sl = slice(h * HEAD_PAD, (h + 1) * HEAD_PAD)
        q_ref[0, :, sl] = (_rope(q[:, sl], tc, tsm, tsp) * Q_SCALE).astype(BF16)

    ckv = proj(OFF_CKV, KV_LORA)
    nkv = (ckv * lax.rsqrt(jnp.mean(ckv * ckv, axis=-1, keepdims=True) + LN_EPS) * kvg_ref[...]).astype(BF16)
    kn = jnp.dot(nkv, wkn_ref[...], preferred_element_type=F32)
    kr = _rope(proj(OFF_KR, HEAD_PAD), tc, tsm, tsp)
    for h in range(MLA_HEADS):
        sl = slice(h * HEAD_PAD, (h + 1) * HEAD_PAD)
        k_ref[0, :, sl] = (kn[:, sl] + kr).astype(BF16)
    vta = lax.dot_general(wv_ref[...], nkv, (((1,), (1,)), ((), ())),
                          preferred_element_type=F32).astype(BF16)
    for h in range(MLA_HEADS):
        vta_ref[0, h * ATT_AUG:h * ATT_AUG + V_HEAD, :] = vta[h * V_HEAD:(h + 1) * V_HEAD, :]
        vta_ref[0, h * ATT_AUG + V_HEAD:(h + 1) * ATT_AUG, :] = ones_rows


def _in_projection(x, sc, sh, lw, tables, tm):
    bsz, t, d = x.shape
    tok = lambda n: pl.BlockSpec((1, tm, n), lambda b, i: (b, i, 0))
    vec = pl.BlockSpec((1, 1, d), lambda b, i: (b, 0, 0))
    tab = pl.BlockSpec((tm, HEAD_PAD), lambda b, i: (i, 0))
    hq = MLA_HEADS * HEAD_PAD
    nc, cpt = t // ML_CHUNK, tm // ML_CHUNK
    out_shape = [
        jax.ShapeDtypeStruct((bsz, t, CONV_DIM), BF16),
        jax.ShapeDtypeStruct((bsz, t, 2 * ML_DIM), BF16),
        jax.ShapeDtypeStruct((bsz, nc, ML_HEADS * ML_AUG, ML_CHUNK), BF16),
        jax.ShapeDtypeStruct((bsz, t, ML_DIM), BF16),
        jax.ShapeDtypeStruct((bsz, nc, 4 * ML_HEADS, ML_CHUNK), F32),
        jax.ShapeDtypeStruct((bsz, t, 3 * D_MODEL), BF16),
        jax.ShapeDtypeStruct((bsz, t, hq), BF16),
        jax.ShapeDtypeStruct((bsz, t, hq), BF16),
        jax.ShapeDtypeStruct((bsz, MLA_HEADS * ATT_AUG, t), BF16),
    ]
    out_specs = [tok(CONV_DIM), tok(2 * ML_DIM),
                 pl.BlockSpec((1, cpt, ML_HEADS * ML_AUG, ML_CHUNK), lambda b, i: (b, i, 0, 0)),
                 tok(ML_DIM),
                 pl.BlockSpec((1, cpt, 4 * ML_HEADS, ML_CHUNK), lambda b, i: (b, i, 0, 0)),
                 tok(3 * D_MODEL), tok(hq), tok(hq),
                 pl.BlockSpec((1, MLA_HEADS * ATT_AUG, tm), lambda b, i: (b, 0, i))]
    return pl.pallas_call(
        _inproj_kernel,
        name="in_projection",
        grid=(bsz, t // tm),
        in_specs=[tok(d), vec, vec,
                  _const_spec((d, N_PACK)), _const_spec((1, N_PACK)),
                  _const_spec((1, Q_LORA)), _const_spec((Q_LORA, hq)),
                  _const_spec((1, KV_LORA)), _const_spec((KV_LORA, hq)),
                  _const_spec((MLA_HEADS * V_HEAD, KV_LORA)),
                  tab, tab, tab],
        out_specs=out_specs,
        out_shape=out_shape,
        compiler_params=pltpu.CompilerParams(
            dimension_semantics=("parallel", "arbitrary"), vmem_limit_bytes=VMEM_LIMIT),
    )(x, sc, sh, lw["w_pack"], lw["b_pack"], lw["qn_g"], lw["w_uq"], lw["kvn_g"], lw["w_kn"], lw["w_v"],
      *tables)


CONV_HALO = 16
CONV_TILE = 128


def _conv_kernel(h_ref, wdw_ref, bdw_ref, g_ref, b_ref, o_ref, hp, stage, cv, shifted, *, t):
    nt = t // CONV_TILE
    ncg = CONV_DIM // LANES
    first = CONV_HALO - CONV_K // 2
    span = CONV_TILE + (CONV_K // SUBLANES) * SUBLANES
    for cg in range(ncg):
        hp[cg, 0:CONV_HALO, :] = jnp.zeros((CONV_HALO, LANES), F32)
        hp[cg, t + CONV_HALO:t + 2 * CONV_HALO, :] = jnp.zeros((CONV_HALO, LANES), F32)

    def fill(i, carry):
        r0 = pl.multiple_of(i * CONV_TILE, CONV_TILE)
        hx = h_ref[0, pl.ds(r0, CONV_TILE), :].astype(F32)
        for cg in range(ncg):
            hp[cg, pl.ds(r0 + CONV_HALO, CONV_TILE), :] = hx[:, cg * LANES:(cg + 1) * LANES]
        return carry

    lax.fori_loop(0, nt, fill, 0)

    def tile(i, carry):
        r0 = pl.multiple_of(i * CONV_TILE, CONV_TILE)

        def group(cg, inner):
            stage[...] = hp[cg, pl.ds(r0, CONV_TILE + 2 * CONV_HALO), :]
            acc = jnp.zeros((CONV_TILE, LANES), F32)
            for r in range(SUBLANES):
                shifted[r] = stage[r:r + span, :]
                for k in range(CONV_K):
                    if (first + k) % SUBLANES == r:
                        a = first + k - r
                        acc = acc + shifted[r, a:a + CONV_TILE, :] * wdw_ref[cg, k:k + 1, :]
            cv[cg] = acc + bdw_ref[cg]
            return inner

        lax.fori_loop(0, ncg, group, 0)
        z = jnp.concatenate([cv[cg] for cg in range(ncg)], axis=1)
        z = _layer_norm(z, g_ref[...], b_ref[...])
        o_ref[0, pl.ds(r0, CONV_TILE), :] = (z * _sigmoid(z)).astype(BF16)
        return carry

    lax.fori_loop(0, nt, tile, 0)


def _conv_branch(glu, lw):
    bsz, t, _ = glu.shape
    ncg = CONV_DIM // LANES
    return pl.pallas_call(
        functools.partial(_conv_kernel, t=t),
        name="conv",
        grid=(bsz,),
        in_specs=[pl.BlockSpec((1, t, CONV_DIM), lambda b: (b, 0, 0)),
                  _const_spec((ncg, CONV_K + 1, LANES)), _const_spec((ncg, 1, LANES)),
                  _const_spec((1, CONV_DIM)), _const_spec((1, CONV_DIM))],
        out_specs=pl.BlockSpec((1, t, CONV_DIM), lambda b: (b, 0, 0)),
        out_shape=jax.ShapeDtypeStruct((bsz, t, CONV_DIM), BF16),
        scratch_shapes=[pltpu.VMEM((ncg, t + 2 * CONV_HALO, LANES), F32),
                        pltpu.VMEM((CONV_TILE + 2 * CONV_HALO, LANES), F32),
                        pltpu.VMEM((ncg, CONV_TILE, LANES), F32),
                        pltpu.VMEM((SUBLANES, CONV_TILE + (CONV_K // SUBLANES) * SUBLANES, LANES), F32)],
        compiler_params=pltpu.CompilerParams(
            dimension_semantics=("parallel",), vmem_limit_bytes=VMEM_LIMIT),
    )(glu, lw["w_dw"], lw["b_dw"], lw["cn_g"], lw["cn_b"])


N_CHAIN = 2 * ML_HEADS


def _mlstm_kernel(qk_c_ref, vt_c_ref, gt_c_ref, so_c_ref, qk_x_ref, vt_x_ref, gt_x_ref, so_x_ref, mlg_ref,
                  hn_c_ref, hn_x_ref, hf_c, hb_c, hf_x, hb_x, c_scr, m_scr, *, nc_c, nc_x):
    L = ML_CHUNK
    dh = ML_HEAD_DIM
    row = lax.broadcasted_iota(jnp.int32, (L, L), 0)
    col = lax.broadcasted_iota(jnp.int32, (L, L), 1)
    upper = row <= col
    lower = row >= col
    tri2 = jnp.concatenate([upper.astype(F32), lower.astype(F32)], axis=1)
    is_fwd = lax.broadcasted_iota(jnp.int32, (N_CHAIN, L), 0) < ML_HEADS

    c_scr[...] = jnp.zeros(c_scr.shape, F32)
    m_scr[...] = jnp.zeros(m_scr.shape, F32)

    def iteration(qk_ref, vt_ref, gt_ref, hf, hb, cf, cb):
        gf = gt_ref[0, cf]
        gb = gt_ref[0, cb]
        li = jnp.where(is_fwd, gf[:N_CHAIN], gb[:N_CHAIN])
        lf = _log_sigmoid(jnp.where(is_fwd, gf[N_CHAIN:], gb[N_CHAIN:]))
        cum2 = jnp.dot(lf, tri2, preferred_element_type=F32, precision=lax.Precision.HIGHEST)
        bc = jnp.where(is_fwd, cum2[:, :L], cum2[:, L:])
        r = li - bc
        b_last = jnp.sum(lf, axis=1, keepdims=True)
        m = m_scr[:, 0:1]
        w_src = b_last + r
        m_new = jnp.maximum(b_last + m, jnp.max(w_src, axis=1, keepdims=True))
        a_state = jnp.exp(b_last + m - m_new)
        w = jnp.exp(w_src - m_new)
        inter = bc + m
        r_cols = jnp.concatenate([r, jnp.zeros((L - N_CHAIN, L), F32)], axis=0).T
        m_scr[...] = jnp.broadcast_to(m_new, m_scr.shape)
        for j in range(N_CHAIN):
            d, h = divmod(j, ML_HEADS)
            cc = cf if d == 0 else cb
            mask = upper if d == 0 else lower
            r0 = pl.multiple_of(cc * L, L)
            log_d = jnp.where(mask, bc[j:j + 1, :] + r_cols[:, j:j + 1], -jnp.inf)
            m_row = jnp.maximum(inter[j:j + 1, :], jnp.max(log_d, axis=0, keepdims=True))
            a_inter = jnp.exp(inter[j:j + 1, :] - m_row)
            dmat = jnp.exp(log_d - m_row)
            q = qk_ref[0, pl.ds(r0, L), h * dh:(h + 1) * dh]
            k = qk_ref[0, pl.ds(r0, L), ML_DIM + h * dh:ML_DIM + (h + 1) * dh]
            vt = vt_ref[0, cc, h * ML_AUG:(h + 1) * ML_AUG, :]
            ct = c_scr[j]
            both = lax.dot_general(jnp.concatenate([k, ct.astype(BF16)], axis=0), q, (((1,), (1,)), ((), ())),
                                   preferred_element_type=F32)
            pt = (both[:L] * dmat).astype(BF16)
            num_aug = a_inter * both[L:] + jnp.dot(vt, pt, preferred_element_type=F32)
            den = num_aug[dh:dh + 1, :]
            hs = hf if d == 0 else hb
            hs[cc, h * dh:(h + 1) * dh, :] = num_aug[:dh] / jnp.maximum(jnp.abs(den), jnp.exp(-m_row))
            wv = (vt.astype(F32) * w[j:j + 1, :]).astype(BF16)
            c_scr[j] = a_state[j:j + 1, :] * ct + jnp.dot(wv, k, preferred_element_type=F32)

    def scan(qk_ref, vt_ref, gt_ref, hf, hb, nc):
        def body(i, carry):
            iteration(qk_ref, vt_ref, gt_ref, hf, hb, i, nc - 1 - i)
            return carry

        lax.fori_loop(0, nc, body, 0, unroll=2)

    scan(qk_c_ref, vt_c_ref, gt_c_ref, hf_c, hb_c, nc_c)
    scan(qk_x_ref, vt_x_ref, gt_x_ref, hf_x, hb_x, nc_x)

    def finish(hf, hb, so_ref, out_ref, nc):
        def body(c, carry):
            r0 = pl.multiple_of(c * L, L)
            hsum = hf[c] + hb[c]
            for h in range(ML_HEADS):
                sl = slice(h * dh, (h + 1) * dh)
                z = hsum[sl, :]
                mu = jnp.mean(z, axis=0, keepdims=True)
                zc = z - mu
                var = jnp.mean(zc * zc, axis=0, keepdims=True)
                hn = (zc * lax.rsqrt(var + LN_EPS)).T * mlg_ref[:, sl]
                out_ref[0, pl.ds(r0, L), sl] = (so_ref[0, pl.ds(r0, L), sl].astype(F32) * hn).astype(BF16)
            return carry

        lax.fori_loop(0, nc, body, 0)

    finish(hf_c, hb_c, so_c_ref, hn_c_ref, nc_c)
    finish(hf_x, hb_x, so_x_ref, hn_x_ref, nc_x)


def _mlstm_branch(pc, px, lw):
    bsz, t_c, _ = pc[0].shape
    t_x = px[0].shape[1]
    nc_c, nc_x = t_c // ML_CHUNK, t_x // ML_CHUNK
    seq = lambda t, n: pl.BlockSpec((1, t, n), lambda b: (b, 0, 0))
    chunked = lambda nc, n: pl.BlockSpec((1, nc, n, ML_CHUNK), lambda b: (b, 0, 0, 0))
    ins = lambda t, nc: [seq(t, 2 * ML_DIM), chunked(nc, ML_HEADS * ML_AUG), chunked(nc, 4 * ML_HEADS),
                         seq(t, ML_DIM)]
    hbuf = lambda nc: pltpu.VMEM((nc, ML_DIM, ML_CHUNK), F32)
    return pl.pallas_call(
        functools.partial(_mlstm_kernel, nc_c=nc_c, nc_x=nc_x),
        name="mlstm",
        grid=(bsz,),
        in_specs=ins(t_c, nc_c) + ins(t_x, nc_x) + [_const_spec((1, ML_DIM))],
        out_specs=[seq(t_c, ML_DIM), seq(t_x, ML_DIM)],
        out_shape=[jax.ShapeDtypeStruct((bsz, t_c, ML_DIM), BF16),
                   jax.ShapeDtypeStruct((bsz, t_x, ML_DIM), BF16)],
        scratch_shapes=[hbuf(nc_c), hbuf(nc_c), hbuf(nc_x), hbuf(nc_x),
                        pltpu.VMEM((N_CHAIN, ML_AUG, ML_HEAD_DIM), F32),
                        pltpu.VMEM((N_CHAIN, LANES), F32)],
        compiler_params=pltpu.CompilerParams(
            dimension_semantics=("parallel",), vmem_limit_bytes=VMEM_LIMIT),
    )(*pc, *px, lw["ml_g"])


ATT_KEYS = 256
ATT_QG = 256
ATT_LOOKAHEAD = 4


def _attn_kernel(*refs, n_seg):
    q_ref = refs[0]
    k_refs = refs[1:1 + n_seg]
    vt_refs = refs[1 + n_seg:1 + 2 * n_seg]
    o_ref = refs[1 + 2 * n_seg]
    n_qg = q_ref.shape[1] // ATT_QG
    chunks = [(k_ref, vt_ref, j) for k_ref, vt_ref in zip(k_refs, vt_refs)
              for j in range(k_ref.shape[1] // ATT_KEYS)]
    steps = [(ci, h, g) for ci in range(len(chunks)) for h in range(MLA_HEADS) for g in range(n_qg)]

    def scores(ci, h, g):
        k_ref, _, j = chunks[ci]
        hs = slice(h * HEAD_PAD, (h + 1) * HEAD_PAD)
        return lax.dot_general(k_ref[0, j * ATT_KEYS:(j + 1) * ATT_KEYS, hs], q_ref[0, g * ATT_QG:(g + 1) * ATT_QG, hs],
                               (((1,), (1,)), ((), ())), preferred_element_type=F32)

    state = {}
    pending = {}
    for idx in range(len(steps) + ATT_LOOKAHEAD):
        if idx < len(steps):
            pending[idx] = scores(*steps[idx])
        if idx < ATT_LOOKAHEAD:
            continue
        ci, h, g = steps[idx - ATT_LOOKAHEAD]
        s = pending.pop(idx - ATT_LOOKAHEAD)
        _, vt_ref, j = chunks[ci]
        vtj = vt_ref[0, h * ATT_AUG:(h + 1) * ATT_AUG, j * ATT_KEYS:(j + 1) * ATT_KEYS]
        cm = jnp.max(s, axis=0, keepdims=True)
        if ci == 0:
            m_new = cm
            acc = jnp.dot(vtj, jnp.exp2(s - m_new).astype(BF16), preferred_element_type=F32)
        else:
            m_old, acc_old = state[h, g]
            m_new = jnp.maximum(m_old, cm)
            acc = acc_old * jnp.exp2(m_old - m_new) + jnp.dot(
                vtj, jnp.exp2(s - m_new).astype(BF16), preferred_element_type=F32)
        state[h, g] = (m_new, acc)
    outs = []
    for h in range(MLA_HEADS):
        accs = [state[h, g][1] for g in range(n_qg)]
        outs.append(jnp.concatenate([a[:V_HEAD] * (1.0 / a[V_HEAD:V_HEAD + 1]) for a in accs], axis=1))
    o_ref[0] = jnp.concatenate(outs, axis=0).T.astype(BF16)


def _attention(q, ks, vts, tq):
    bsz, t, hq = q.shape
    n_seg = len(ks)
    hv = MLA_HEADS * V_HEAD
    kspec = [pl.BlockSpec((1, k.shape[1], hq), lambda b, i: (b, 0, 0)) for k in ks]
    vspec = [pl.BlockSpec((1, MLA_HEADS * ATT_AUG, vt.shape[2]), lambda b, i: (b, 0, 0)) for vt in vts]
    return pl.pallas_call(
        functools.partial(_attn_kernel, n_seg=n_seg),
        name="attention",
        grid=(bsz, t // tq),
        in_specs=[pl.BlockSpec((1, tq, hq), lambda b, i: (b, i, 0))] + kspec + vspec,
        out_specs=pl.BlockSpec((1, tq, hv), lambda b, i: (b, i, 0)),
        out_shape=jax.ShapeDtypeStruct((bsz, t, hv), BF16),
        compiler_params=pltpu.CompilerParams(
            dimension_semantics=("parallel", "arbitrary"), vmem_limit_bytes=VMEM_LIMIT),
    )(q, *ks, *vts)


FF_CHUNK = 1024


def _merge_mlp_kernel(x_ref, hc_ref, hm_ref, oa_ref, gates_ref, g1_ref, sc2_ref, sh2_ref, g2_ref,
                      wc_ref, wm_ref, wa_ref, wo_ref, bo_ref, ln1g_ref, ln1b_ref,
                      w1_ref, b1_ref, w2_ref, b2_ref, ln2g_ref, ln2b_ref, o_ref):
    d = D_MODEL
    yc = jnp.dot(hc_ref[0], wc_ref[...], preferred_element_type=F32)
    ym = jnp.dot(hm_ref[0], wm_ref[...], preferred_element_type=F32)
    ya = jnp.dot(oa_ref[0], wa_ref[...], preferred_element_type=F32)
    mix = (gates_ref[0, :, 0:d].astype(F32) * yc + gates_ref[0, :, d:2 * d].astype(F32) * ym
           + gates_ref[0, :, 2 * d:3 * d].astype(F32) * ya)
    y = jnp.dot(mix.astype(BF16), wo_ref[...], preferred_element_type=F32) + bo_ref[...]
    x1 = _layer_norm(ALPHA * x_ref[0] + g1_ref[0] * y, ln1g_ref[...], ln1b_ref[...])
    u2 = (x1 * (1.0 + sc2_ref[0]) + sh2_ref[0]).astype(BF16)
    acc = jnp.zeros(x1.shape, F32)
    for j in range(D_FF // FF_CHUNK):
        sl = slice(j * FF_CHUNK, (j + 1) * FF_CHUNK)
        hdn = jnp.maximum(jnp.dot(u2, w1_ref[:, sl], preferred_element_type=F32) + b1_ref[:, sl], 0.0)
        acc = acc + jnp.dot((hdn * hdn).astype(BF16), w2_ref[sl, :], preferred_element_type=F32)
    mlp = acc + b2_ref[...]
    o_ref[0] = _layer_norm(ALPHA * x1 + g2_ref[0] * mlp, ln2g_ref[...], ln2b_ref[...])


def _merge_mlp(x, hc, hm, oa, gates, g1, sc2, sh2, g2, lw, tm):
    bsz, t, d = x.shape
    tok = lambda n: pl.BlockSpec((1, tm, n), lambda b, i: (b, i, 0))
    vec = pl.BlockSpec((1, 1, d), lambda b, i: (b, 0, 0))
    return pl.pallas_call(
        _merge_mlp_kernel,
        name="merge_mlp",
        grid=(bsz, t // tm),
        in_specs=[tok(d), tok(CONV_DIM), tok(ML_DIM), tok(MLA_HEADS * V_HEAD), tok(3 * d), vec, vec, vec, vec,
                  _const_spec((CONV_DIM, d)), _const_spec((ML_DIM, d)), _const_spec((MLA_HEADS * V_HEAD, d)),
                  _const_spec((d, d)), _const_spec((1, d)), _const_spec((1, d)), _const_spec((1, d)),
                  _const_spec((d, D_FF)), _const_spec((1, D_FF)), _const_spec((D_FF, d)), _const_spec((1, d)),
                  _const_spec((1, d)), _const_spec((1, d))],
        out_specs=tok(d),
        out_shape=jax.ShapeDtypeStruct((bsz, t, d), F32),
        compiler_params=pltpu.CompilerParams(
            dimension_semantics=("parallel", "arbitrary"), vmem_limit_bytes=VMEM_LIMIT),
    )(x, hc, hm, oa, gates, g1, sc2, sh2, g2,
      lw["w_conv_out"], lw["w_ml_out"], lw["w_mla_out"], lw["w_out"], lw["b_out"], lw["ln1_g"], lw["ln1_b"],
      lw["w1"], lw["b1"], lw["w2"], lw["b2"], lw["ln2_g"], lw["ln2_b"])


def _head_block_perm():
    lanes = np.zeros((QK_ROPE,), np.int32)
    for a in range(2):
        for half in range(2):
            for f in range(ROPE_FREQ):
                lanes[a * 2 * ROPE_FREQ + half * ROPE_FREQ + f] = QK_NOPE + half * ROPE_HALF + a * ROPE_FREQ + f
    return lanes


def _layer_weights(l, w_in, b_in, w_dw, b_dw, cn_g, cn_b, w_conv_out, ml_g, w_ml_out, qn_g, w_uq, kvn_g, w_ukv,
                   w_mla_out, w_out, b_out, ln1_g, ln1_b, w1, b1, w2, b2, ln2_g, ln2_b):
    rope_lanes = _head_block_perm()
    wi, bi = w_in[l], b_in[l]
    d = D_MODEL

    def src(i):
        return wi[:, _SRC[i]:_SRC[i + 1]], bi[_SRC[i]:_SRC[i + 1]]

    (wa, ba), (wqkv, bqkv), (wo, bo), (wg, bg), (wcq, bcq), (wckv, bckv), (wkr, bkr), (wgt, bgt) = (
        src(i) for i in range(8))
    kr_w = jnp.zeros((d, HEAD_PAD), F32).at[:, rope_lanes].set(wkr)
    kr_b = jnp.zeros((HEAD_PAD,), F32).at[rope_lanes].set(bkr)
    gate_cols = np.arange(4 * ML_HEADS).reshape(4, ML_HEADS)[[0, 2, 1, 3]].reshape(-1)
    g_w = jnp.zeros((d, LANES), F32).at[:, :4 * ML_HEADS].set(wg[:, gate_cols])
    g_b = jnp.zeros((LANES,), F32).at[:4 * ML_HEADS].set(bg[gate_cols])
    w_pack = jnp.concatenate([wa, wqkv, wo, wcq, wckv, kr_w, g_w, wgt], axis=1).astype(BF16)
    b_pack = jnp.concatenate([ba, bqkv, bo, bcq, bckv, kr_b, g_b, bgt])[None, :]

    wq = w_uq[l].reshape(Q_LORA, MLA_HEADS, QK_NOPE + QK_ROPE)
    wq_p = jnp.zeros((Q_LORA, MLA_HEADS, HEAD_PAD), F32)
    wq_p = wq_p.at[:, :, :QK_NOPE].set(wq[:, :, :QK_NOPE]).at[:, :, rope_lanes].set(wq[:, :, QK_NOPE:])
    wkv = w_ukv[l].reshape(KV_LORA, MLA_HEADS, QK_NOPE + V_HEAD)
    wkn_p = jnp.zeros((KV_LORA, MLA_HEADS, HEAD_PAD), F32).at[:, :, :QK_NOPE].set(wkv[:, :, :QK_NOPE])
    wv_p = wkv[:, :, QK_NOPE:]
    row = lambda z: z[l][None, :]
    return {
        "w_pack": w_pack, "b_pack": b_pack,
        "qn_g": row(qn_g), "w_uq": wq_p.reshape(Q_LORA, MLA_HEADS * HEAD_PAD).astype(BF16),
        "kvn_g": row(kvn_g), "w_kn": wkn_p.reshape(KV_LORA, MLA_HEADS * HEAD_PAD).astype(BF16),
        "w_v": wv_p.reshape(KV_LORA, MLA_HEADS * V_HEAD).T.astype(BF16),
        "w_dw": jnp.pad(w_dw[l], ((0, 1), (0, 0))).reshape(CONV_K + 1, CONV_DIM // LANES, LANES).transpose(1, 0, 2),
        "b_dw": b_dw[l].reshape(CONV_DIM // LANES, 1, LANES),
        "cn_g": row(cn_g), "cn_b": row(cn_b),
        "ml_g": row(ml_g),
        "w_conv_out": w_conv_out[l].astype(BF16), "w_ml_out": w_ml_out[l].astype(BF16),
        "w_mla_out": w_mla_out[l].astype(BF16), "w_out": w_out[l].astype(BF16), "b_out": row(b_out),
        "ln1_g": row(ln1_g), "ln1_b": row(ln1_b), "w1": w1[l].astype(BF16), "b1": row(b1),
        "w2": w2[l].astype(BF16), "b2": row(b2), "ln2_g": row(ln2_g), "ln2_b": row(ln2_b),
    }


def _rope_tables(n_tokens):
    rows = n_tokens // GRID_W
    rr, cc = jnp.meshgrid(jnp.arange(rows, dtype=F32), jnp.arange(GRID_W, dtype=F32), indexing="ij")
    inv = ROPE_THETA ** (-jnp.arange(ROPE_FREQ, dtype=F32) / ROPE_FREQ)
    ang = jnp.stack([rr.reshape(-1), cc.reshape(-1)], -1)[..., None] * inv
    cos = jnp.cos(ang).reshape(n_tokens, ROPE_HALF)
    sin = jnp.sin(ang).reshape(n_tokens, ROPE_HALF)
    one = jnp.ones((n_tokens, QK_NOPE), F32)
    z_nope = jnp.zeros((n_tokens, QK_NOPE), F32)
    z_half = jnp.zeros((n_tokens, ROPE_HALF), F32)
    z_pad = jnp.zeros((n_tokens, HEAD_PAD - QK_NOPE - QK_ROPE), F32)
    tc = jnp.concatenate([one, cos, cos, z_pad], axis=1)
    tsm = jnp.concatenate([z_nope, -sin, z_half, z_pad], axis=1)
    tsp = jnp.concatenate([z_nope, z_half, sin, z_pad], axis=1)
    return tc, tsm, tsp


def _identity_tables(n_tokens):
    return (jnp.ones((n_tokens, HEAD_PAD), F32), jnp.zeros((n_tokens, HEAD_PAD), F32),
            jnp.zeros((n_tokens, HEAD_PAD), F32))


def kernel(x, c, ctx, c_ctx, w_mod, b_mod, w_in, b_in, w_dw, b_dw, conv_norm_g, conv_norm_b, w_conv_out,
           mlstm_norm_g, w_mlstm_out, q_norm_g, w_uq, kv_norm_g, w_ukv, w_mla_out, w_out, b_out,
           ln1_g, ln1_b, w1, b1, w2, b2, ln2_g, ln2_b):
    bsz, t_x, d = x.shape
    t_c = ctx.shape[1]
    n_rows = 24
    cc = jnp.zeros((n_rows, d), F32).at[:bsz].set(c).at[bsz].set(c_ctx)
    mod = _modulation(cc, w_mod, b_mod)
    rope_x = _rope_tables(t_x)
    rope_c = _identity_tables(t_c)

    for l in range(DEPTH):
        with_ctx = l < DEPTH - 1
        lw = _layer_weights(l, w_in, b_in, w_dw, b_dw, conv_norm_g, conv_norm_b, w_conv_out, mlstm_norm_g,
                            w_mlstm_out, q_norm_g, w_uq, kv_norm_g, w_ukv, w_mla_out, w_out, b_out,
                            ln1_g, ln1_b, w1, b1, w2, b2, ln2_g, ln2_b)
        mx = [mod[l, :bsz, i * d:(i + 1) * d][:, None, :] for i in range(6)]
        mc = [jnp.broadcast_to(mod[l, bsz, i * d:(i + 1) * d][None, None, :], (bsz, 1, d)) for i in range(6)]

        glu_x, qk_x, vtm_x, so_x, gt_x, gates_x, q_x, k_x, vta_x = _in_projection(x, mx[1], mx[0], lw, rope_x, 512)
        glu_c, qk_c, vtm_c, so_c, gt_c, gates_c, q_c, k_c, vta_c = _in_projection(ctx, mc[1], mc[0], lw, rope_c, 256)

        hc_x = _conv_branch(glu_x, lw)
        hm_c, hm_x = _mlstm_branch((qk_c, vtm_c, gt_c, so_c), (qk_x, vtm_x, gt_x, so_x), lw)
        oa_x = _attention(q_x, [k_c, k_x], [vta_c, vta_x], 512)
        x_new = _merge_mlp(x, hc_x, hm_x, oa_x, gates_x, mx[2], mx[4], mx[3], mx[5], lw, 512)
        if with_ctx:
            hc_c = _conv_branch(glu_c, lw)
            oa_c = _attention(q_c, [k_c], [vta_c], 256)
            ctx = _merge_mlp(ctx, hc_c, hm_c, oa_c, gates_c, mc[2], mc[4], mc[3], mc[5], lw, 256)
        x = x_new
    return x
```

```python
import functools

import numpy as np
import jax
import jax.numpy as jnp
from jax import lax
from jax.experimental import pallas as pl
from jax.experimental.pallas import tpu as pltpu

F32 = jnp.float32
BF16 = jnp.bfloat16

D_MODEL = 1024
DEPTH = 2
GRID_W = 64
CONV_DIM = 512
CONV_K = 31
ML_HEADS = 4
ML_HEAD_DIM = 128
ML_DIM = ML_HEADS * ML_HEAD_DIM
ML_CHUNK = 128
MLA_HEADS = 8
QK_NOPE = 64
QK_ROPE = 32
V_HEAD = 64
Q_LORA = 768
KV_LORA = 256
ROPE_THETA = 10000.0
ROPE_FREQ = QK_ROPE // 4
MLA_SCALE = (QK_NOPE + QK_ROPE) ** -0.5
ML_SCALE = ML_HEAD_DIM ** -0.5
D_FF = 4 * D_MODEL
LN_EPS = 1e-5
ALPHA = (2 * DEPTH) ** 0.25
N_MOD = 6
MOD_SH1, MOD_SC1, MOD_G1, MOD_SH2, MOD_SC2, MOD_G2 = range(N_MOD)

LANES = 128
SUBLANES = 8
HEAD_PAD = LANES
ROPE_HALF = QK_ROPE // 2
ONES_ROWS = 16
ML_AUG = ML_HEAD_DIM + ONES_ROWS
ATT_AUG = V_HEAD + ONES_ROWS
Q_SCALE = MLA_SCALE * float(np.log2(np.e))
VMEM_LIMIT = 56 * 1024 * 1024

OFF_A = 0
OFF_QKV = OFF_A + 2 * CONV_DIM
OFF_O = OFF_QKV + 3 * ML_DIM
OFF_CQ = OFF_O + ML_DIM
OFF_CKV = OFF_CQ + Q_LORA
OFF_KR = OFF_CKV + KV_LORA
OFF_GIF = OFF_KR + HEAD_PAD
OFF_GATES = OFF_GIF + LANES
N_PACK = OFF_GATES + 3 * D_MODEL

_COLS = (2 * CONV_DIM, 3 * ML_DIM, ML_DIM, 4 * ML_HEADS, Q_LORA, KV_LORA, QK_ROPE, 3 * D_MODEL)
_SRC = tuple(int(s) for s in np.cumsum((0,) + _COLS))


def _layer_spec(l, shape):
    nd = len(shape)
    return pl.BlockSpec((None,) + tuple(shape), lambda *_: (l,) + (0,) * nd, pipeline_mode=pl.Buffered(1))


def _mod_spec(l, k, row=None):
    block = (None, None, None, 1, D_MODEL)
    if row is None:
        return pl.BlockSpec(block, lambda b, i: (l, b, k, 0, 0))
    return pl.BlockSpec(block, lambda b, i: (l, row, k, 0, 0))


def _sigmoid(z):
    return 1.0 / (1.0 + jnp.exp(-z))


def _log_sigmoid(z):
    return jnp.minimum(z, 0.0) - jnp.log1p(jnp.exp(-jnp.abs(z)))


def _layer_norm(z, g, b):
    mu = jnp.mean(z, axis=-1, keepdims=True)
    zc = z - mu
    var = jnp.mean(zc * zc, axis=-1, keepdims=True)
    return zc * lax.rsqrt(var + LN_EPS) * g + b


def _mod_kernel(c_ref, w_ref, b_ref, o_ref):
    c = c_ref[...]
    s = (c * _sigmoid(c)).astype(BF16)
    o_ref[0] = jnp.dot(s, w_ref[0].astype(BF16), preferred_element_type=F32) + b_ref[0]


def _modulation(cc, w_mod, b_mod):
    nl, d, n = w_mod.shape
    r = cc.shape[0]
    tn = 1024
    return pl.pallas_call(
        _mod_kernel,
        name="modulation",
        grid=(nl, n // tn),
        in_specs=[
            pl.BlockSpec((r, d), lambda l, j: (0, 0)),
            pl.BlockSpec((1, d, tn), lambda l, j: (l, 0, j)),
            pl.BlockSpec((1, 1, tn), lambda l, j: (l, 0, j)),
        ],
        out_specs=pl.BlockSpec((1, r, tn), lambda l, j: (l, 0, j)),
        out_shape=jax.ShapeDtypeStruct((nl, r, n), F32),
        compiler_params=pltpu.CompilerParams(
            dimension_semantics=("arbitrary", "arbitrary"), vmem_limit_bytes=VMEM_LIMIT),
    )(cc, w_mod, b_mod.reshape(nl, 1, n))


def _rope(z, tc, tsm, tsp):
    return z * tc + pltpu.roll(z, HEAD_PAD - ROPE_HALF, 1) * tsm + pltpu.roll(z, ROPE_HALF, 1) * tsp


def _inproj_kernel(x_ref, sc_ref, sh_ref, w_ref, b_ref, qng_ref, wuq_ref, kvg_ref, wkn_ref, wv_ref,
                   tc_ref, tsm_ref, tsp_ref,
                   glu_ref, qk_ref, vtm_ref, so_ref, gt_ref, gates_ref, q_ref, k_ref, vta_ref):
    u = (x_ref[0] * (1.0 + sc_ref[...]) + sh_ref[...]).astype(BF16)
    tm = u.shape[0]
    ones_rows = jnp.ones((ONES_ROWS, tm), BF16)

    def proj(off, n):
        return jnp.dot(u, w_ref[:, off:off + n], preferred_element_type=F32) + b_ref[:, off:off + n]

    a = proj(OFF_A, 2 * CONV_DIM)
    glu_ref[0] = (a[:, :CONV_DIM] * _sigmoid(a[:, CONV_DIM:])).astype(BF16)

    qk_ref[0, :, :ML_DIM] = proj(OFF_QKV, ML_DIM).astype(BF16)
    qk_ref[0, :, ML_DIM:] = (proj(OFF_QKV + ML_DIM, ML_DIM) * ML_SCALE).astype(BF16)
    vt = proj(OFF_QKV + 2 * ML_DIM, ML_DIM).T.astype(BF16)
    gt = proj(OFF_GIF, LANES).T
    for c in range(tm // ML_CHUNK):
        cs = slice(c * ML_CHUNK, (c + 1) * ML_CHUNK)
        gt_ref[0, c] = gt[:4 * ML_HEADS, cs]
        for h in range(ML_HEADS):
            vtm_ref[0, c, h * ML_AUG:h * ML_AUG + ML_HEAD_DIM, :] = vt[h * ML_HEAD_DIM:(h + 1) * ML_HEAD_DIM, cs]
            vtm_ref[0, c, h * ML_AUG + ML_HEAD_DIM:(h + 1) * ML_AUG, :] = ones_rows[:, cs]
    so_ref[0] = _sigmoid(proj(OFF_O, ML_DIM)).astype(BF16)
    for j in range(3):
        gates_ref[0, :, j * D_MODEL:(j + 1) * D_MODEL] = _sigmoid(
            proj(OFF_GATES + j * D_MODEL, D_MODEL)).astype(BF16)

    tc, tsm, tsp = tc_ref[...], tsm_ref[...], tsp_ref[...]

    cq = proj(OFF_CQ, Q_LORA)
    nq = cq * lax.rsqrt(jnp.mean(cq * cq, axis=-1, keepdims=True) + LN_EPS) * qng_ref[...]
    q = jnp.dot(nq.astype(BF16), wuq_ref[...], preferred_element_type=F32)
    for h in range(MLA_HEADS):
        sl = slice(h * HEAD_PAD, (h + 1) * HEAD_PAD)
        q_ref[0, :, sl] = (_rope(q[:, sl], tc, tsm, tsp) * Q_SCALE).astype(BF16)

    ckv = proj(OFF_CKV, KV_LORA)
    nkv = (ckv * lax.rsqrt(jnp.mean(ckv * ckv, axis=-1, keepdims=True) + LN_EPS) * kvg_ref[...]).astype(BF16)
    kn = jnp.dot(nkv, wkn_ref[...], preferred_element_type=F32)
    kr = _rope(proj(OFF_KR, HEAD_PAD), tc, tsm, tsp)
    for h in range(MLA_HEADS):
        sl = slice(h * HEAD_PAD, (h + 1) * HEAD_PAD)
        k_ref[0, :, sl] = (kn[:, sl] + kr).astype(BF16)
    vta = lax.dot_general(wv_ref[...], nkv, (((1,), (1,)), ((), ())),
                          preferred_element_type=F32).astype(BF16)
    for h in range(MLA_HEADS):
        vta_ref[0, h * ATT_AUG:h * ATT_AUG + V_HEAD, :] = vta[h * V_HEAD:(h + 1) * V_HEAD, :]
        vta_ref[0, h * ATT_AUG + V_HEAD:(h + 1) * ATT_AUG, :] = ones_rows


def _in_projection(x, mod, l, mod_row, wts, tables, tm):
    bsz, t, d = x.shape
    tok = lambda n: pl.BlockSpec((1, tm, n), lambda b, i: (b, i, 0))
    tab = pl.BlockSpec((tm, HEAD_PAD), lambda b, i: (i, 0))
    hq = MLA_HEADS * HEAD_PAD
    nc, cpt = t // ML_CHUNK, tm // ML_CHUNK
    out_shape = [
        jax.ShapeDtypeStruct((bsz, t, CONV_DIM), BF16),
        jax.ShapeDtypeStruct((bsz, t, 2 * ML_DIM), BF16),
        jax.ShapeDtypeStruct((bsz, nc, ML_HEADS * ML_AUG, ML_CHUNK), BF16),
        jax.ShapeDtypeStruct((bsz, t, ML_DIM), BF16),
        jax.ShapeDtypeStruct((bsz, nc, 4 * ML_HEADS, ML_CHUNK), F32),
        jax.ShapeDtypeStruct((bsz, t, 3 * D_MODEL), BF16),
        jax.ShapeDtypeStruct((bsz, t, hq), BF16),
        jax.ShapeDtypeStruct((bsz, t, hq), BF16),
        jax.ShapeDtypeStruct((bsz, MLA_HEADS * ATT_AUG, t), BF16),
    ]
    out_specs = [tok(CONV_DIM), tok(2 * ML_DIM),
                 pl.BlockSpec((1, cpt, ML_HEADS * ML_AUG, ML_CHUNK), lambda b, i: (b, i, 0, 0)),
                 tok(ML_DIM),
                 pl.BlockSpec((1, cpt, 4 * ML_HEADS, ML_CHUNK), lambda b, i: (b, i, 0, 0)),
                 tok(3 * D_MODEL), tok(hq), tok(hq),
                 pl.BlockSpec((1, MLA_HEADS * ATT_AUG, tm), lambda b, i: (b, 0, i))]
    return pl.pallas_call(
        _inproj_kernel,
        name="in_projection",
        grid=(bsz, t // tm),
        in_specs=[tok(d), _mod_spec(l, MOD_SC1, mod_row), _mod_spec(l, MOD_SH1, mod_row),
                  _layer_spec(l, (d, N_PACK)), _layer_spec(l, (1, N_PACK)),
                  _layer_spec(l, (1, Q_LORA)), _layer_spec(l, (Q_LORA, hq)),
                  _layer_spec(l, (1, KV_LORA)), _layer_spec(l, (KV_LORA, hq)),
                  _layer_spec(l, (MLA_HEADS * V_HEAD, KV_LORA)),
                  tab, tab, tab],
        out_specs=out_specs,
        out_shape=out_shape,
        compiler_params=pltpu.CompilerParams(
            dimension_semantics=("parallel", "arbitrary"), vmem_limit_bytes=VMEM_LIMIT),
    )(x, mod, mod, wts["w_pack"], wts["b_pack"], wts["qn_g"], wts["w_uq"], wts["kvn_g"], wts["w_kn"], wts["w_v"],
      *tables)


CONV_HALO = 16
CONV_TILE = 128


def _conv_kernel(h_ref, wdw_ref, bdw_ref, g_ref, b_ref, o_ref, hp, stage, cv, shifted, *, t):
    nt = t // CONV_TILE
    ncg = CONV_DIM // LANES
    first = CONV_HALO - CONV_K // 2
    span = CONV_TILE + (CONV_K // SUBLANES) * SUBLANES
    for cg in range(ncg):
        hp[cg, 0:CONV_HALO, :] = jnp.zeros((CONV_HALO, LANES), F32)
        hp[cg, t + CONV_HALO:t + 2 * CONV_HALO, :] = jnp.zeros((CONV_HALO, LANES), F32)

    def fill(i, carry):
        r0 = pl.multiple_of(i * CONV_TILE, CONV_TILE)
        hx = h_ref[0, pl.ds(r0, CONV_TILE), :].astype(F32)
        for cg in range(ncg):
            hp[cg, pl.ds(r0 + CONV_HALO, CONV_TILE), :] = hx[:, cg * LANES:(cg + 1) * LANES]
        return carry

    lax.fori_loop(0, nt, fill, 0)

    def tile(i, carry):
        r0 = pl.multiple_of(i * CONV_TILE, CONV_TILE)

        def group(cg, inner):
            stage[...] = hp[cg, pl.ds(r0, CONV_TILE + 2 * CONV_HALO), :]
            acc = jnp.zeros((CONV_TILE, LANES), F32)
            for r in range(SUBLANES):
                shifted[r] = stage[r:r + span, :]
                for k in range(CONV_K):
                    if (first + k) % SUBLANES == r:
                        a = first + k - r
                        acc = acc + shifted[r, a:a + CONV_TILE, :] * wdw_ref[cg, k:k + 1, :]
            cv[cg] = acc + bdw_ref[cg]
            return inner

        lax.fori_loop(0, ncg, group, 0)
        z = jnp.concatenate([cv[cg] for cg in range(ncg)], axis=1)
        z = _layer_norm(z, g_ref[...], b_ref[...])
        o_ref[0, pl.ds(r0, CONV_TILE), :] = (z * _sigmoid(z)).astype(BF16)
        return carry

    lax.fori_loop(0, nt, tile, 0)


def _conv_branch(glu, l, wts):
    bsz, t, _ = glu.shape
    ncg = CONV_DIM // LANES
    return pl.pallas_call(
        functools.partial(_conv_kernel, t=t),
        name="conv",
        grid=(bsz,),
        in_specs=[pl.BlockSpec((1, t, CONV_DIM), lambda b: (b, 0, 0)),
                  _layer_spec(l, (ncg, CONV_K + 1, LANES)), _layer_spec(l, (ncg, 1, LANES)),
                  _layer_spec(l, (1, CONV_DIM)), _layer_spec(l, (1, CONV_DIM))],
        out_specs=pl.BlockSpec((1, t, CONV_DIM), lambda b: (b, 0, 0)),
        out_shape=jax.ShapeDtypeStruct((bsz, t, CONV_DIM), BF16),
        scratch_shapes=[pltpu.VMEM((ncg, t + 2 * CONV_HALO, LANES), F32),
                        pltpu.VMEM((CONV_TILE + 2 * CONV_HALO, LANES), F32),
                        pltpu.VMEM((ncg, CONV_TILE, LANES), F32),
                        pltpu.VMEM((SUBLANES, CONV_TILE + (CONV_K // SUBLANES) * SUBLANES, LANES), F32)],
        compiler_params=pltpu.CompilerParams(
            dimension_semantics=("parallel",), vmem_limit_bytes=VMEM_LIMIT),
    )(glu, wts["w_dw"], wts["b_dw"], wts["cn_g"], wts["cn_b"])


N_CHAIN = 2 * ML_HEADS


def _mlstm_kernel(qk_c_ref, vt_c_ref, gt_c_ref, so_c_ref, qk_x_ref, vt_x_ref, gt_x_ref, so_x_ref, mlg_ref,
                  hn_c_ref, hn_x_ref, hf_c, hb_c, hf_x, hb_x, c_scr, m_scr, *, nc_c, nc_x):
    L = ML_CHUNK
    dh = ML_HEAD_DIM
    row = lax.broadcasted_iota(jnp.int32, (L, L), 0)
    col = lax.broadcasted_iota(jnp.int32, (L, L), 1)
    upper = row <= col
    lower = row >= col
    tri2 = jnp.concatenate([upper.astype(F32), lower.astype(F32)], axis=1)
    is_fwd = lax.broadcasted_iota(jnp.int32, (N_CHAIN, L), 0) < ML_HEADS

    c_scr[...] = jnp.zeros(c_scr.shape, F32)
    m_scr[...] = jnp.zeros(m_scr.shape, F32)

    def iteration(qk_ref, vt_ref, gt_ref, hf, hb, cf, cb):
        gf = gt_ref[0, cf]
        gb = gt_ref[0, cb]
        li = jnp.where(is_fwd, gf[:N_CHAIN], gb[:N_CHAIN])
        lf = _log_sigmoid(jnp.where(is_fwd, gf[N_CHAIN:], gb[N_CHAIN:]))
        cum2 = jnp.dot(lf, tri2, preferred_element_type=F32, precision=lax.Precision.HIGHEST)
        bc = jnp.where(is_fwd, cum2[:, :L], cum2[:, L:])
        r = li - bc
        b_last = jnp.sum(lf, axis=1, keepdims=True)
        m = m_scr[:, 0:1]
        w_src = b_last + r
        m_new = jnp.maximum(b_last + m, jnp.max(w_src, axis=1, keepdims=True))
        a_state = jnp.exp(b_last + m - m_new)
        w = jnp.exp(w_src - m_new)
        inter = bc + m
        r_cols = jnp.concatenate([r, jnp.zeros((L - N_CHAIN, L), F32)], axis=0).T
        m_scr[...] = jnp.broadcast_to(m_new, m_scr.shape)
        for j in range(N_CHAIN):
            d, h = divmod(j, ML_HEADS)
            cc = cf if d == 0 else cb
            mask = upper if d == 0 else lower
            r0 = pl.multiple_of(cc * L, L)
            log_d = jnp.where(mask, bc[j:j + 1, :] + r_cols[:, j:j + 1], -jnp.inf)
            m_row = jnp.maximum(inter[j:j + 1, :], jnp.max(log_d, axis=0, keepdims=True))
            a_inter = jnp.exp(inter[j:j + 1, :] - m_row)
            dmat = jnp.exp(log_d - m_row)
            q = qk_ref[0, pl.ds(r0, L), h * dh:(h + 1) * dh]
            k = qk_ref[0, pl.ds(r0, L), ML_DIM + h * dh:ML_DIM + (h + 1) * dh]
            vt = vt_ref[0, cc, h * ML_AUG:(h + 1) * ML_AUG, :]
            ct = c_scr[j]
            both = lax.dot_general(jnp.concatenate([k, ct.astype(BF16)], axis=0), q, (((1,), (1,)), ((), ())),
                                   preferred_element_type=F32)
            pt = (both[:L] * dmat).astype(BF16)
            num_aug = a_inter * both[L:] + jnp.dot(vt, pt, preferred_element_type=F32)
            den = num_aug[dh:dh + 1, :]
            hs = hf if d == 0 else hb
            hs[cc, h * dh:(h + 1) * dh, :] = num_aug[:dh] / jnp.maximum(jnp.abs(den), jnp.exp(-m_row))
            wv = (vt.astype(F32) * w[j:j + 1, :]).astype(BF16)
            c_scr[j] = a_state[j:j + 1, :] * ct + jnp.dot(wv, k, preferred_element_type=F32)

    def scan(qk_ref, vt_ref, gt_ref, hf, hb, nc):
        def body(i, carry):
            iteration(qk_ref, vt_ref, gt_ref, hf, hb, i, nc - 1 - i)
            return carry

        lax.fori_loop(0, nc, body, 0, unroll=2)

    scan(qk_c_ref, vt_c_ref, gt_c_ref, hf_c, hb_c, nc_c)
    scan(qk_x_ref, vt_x_ref, gt_x_ref, hf_x, hb_x, nc_x)

    def finish(hf, hb, so_ref, out_ref, nc):
        def body(c, carry):
            r0 = pl.multiple_of(c * L, L)
            hsum = hf[c] + hb[c]
            for h in range(ML_HEADS):
                sl = slice(h * dh, (h + 1) * dh)
                z = hsum[sl, :]
                mu = jnp.mean(z, axis=0, keepdims=True)
                zc = z - mu
                var = jnp.mean(zc * zc, axis=0, keepdims=True)
                hn = (zc * lax.rsqrt(var + LN_EPS)).T * mlg_ref[:, sl]
                out_ref[0, pl.ds(r0, L), sl] = (so_ref[0, pl.ds(r0, L), sl].astype(F32) * hn).astype(BF16)
            return carry

        lax.fori_loop(0, nc, body, 0)

    finish(hf_c, hb_c, so_c_ref, hn_c_ref, nc_c)
    finish(hf_x, hb_x, so_x_ref, hn_x_ref, nc_x)


def _mlstm_branch(pc, px, l, wts):
    bsz, t_c, _ = pc[0].shape
    t_x = px[0].shape[1]
    nc_c, nc_x = t_c // ML_CHUNK, t_x // ML_CHUNK
    seq = lambda t, n: pl.BlockSpec((1, t, n), lambda b: (b, 0, 0))
    chunked = lambda nc, n: pl.BlockSpec((1, nc, n, ML_CHUNK), lambda b: (b, 0, 0, 0))
    ins = lambda t, nc: [seq(t, 2 * ML_DIM), chunked(nc, ML_HEADS * ML_AUG), chunked(nc, 4 * ML_HEADS),
                         seq(t, ML_DIM)]
    hbuf = lambda nc: pltpu.VMEM((nc, ML_DIM, ML_CHUNK), F32)
    return pl.pallas_call(
        functools.partial(_mlstm_kernel, nc_c=nc_c, nc_x=nc_x),
        name="mlstm",
        grid=(bsz,),
        in_specs=ins(t_c, nc_c) + ins(t_x, nc_x) + [_layer_spec(l, (1, ML_DIM))],
        out_specs=[seq(t_c, ML_DIM), seq(t_x, ML_DIM)],
        out_shape=[jax.ShapeDtypeStruct((bsz, t_c, ML_DIM), BF16),
                   jax.ShapeDtypeStruct((bsz, t_x, ML_DIM), BF16)],
        scratch_shapes=[hbuf(nc_c), hbuf(nc_c), hbuf(nc_x), hbuf(nc_x),
                        pltpu.VMEM((N_CHAIN, ML_AUG, ML_HEAD_DIM), F32),
                        pltpu.VMEM((N_CHAIN, LANES), F32)],
        compiler_params=pltpu.CompilerParams(
            dimension_semantics=("parallel",), vmem_limit_bytes=VMEM_LIMIT),
    )(*pc, *px, wts["ml_g"])


ATT_KEYS = 256
ATT_QG = 256
ATT_LOOKAHEAD = 4


def _attn_kernel(*refs, n_seg):
    q_ref = refs[0]
    k_refs = refs[1:1 + n_seg]
    vt_refs = refs[1 + n_seg:1 + 2 * n_seg]
    o_ref = refs[1 + 2 * n_seg]
    n_qg = q_ref.shape[1] // ATT_QG
    chunks = [(k_ref, vt_ref, j) for k_ref, vt_ref in zip(k_refs, vt_refs)
              for j in range(k_ref.shape[1] // ATT_KEYS)]
    steps = [(ci, h, g) for ci in range(len(chunks)) for h in range(MLA_HEADS) for g in range(n_qg)]

    def scores(ci, h, g):
        k_ref, _, j = chunks[ci]
        hs = slice(h * HEAD_PAD, (h + 1) * HEAD_PAD)
        return lax.dot_general(k_ref[0, j * ATT_KEYS:(j + 1) * ATT_KEYS, hs], q_ref[0, g * ATT_QG:(g + 1) * ATT_QG, hs],
                               (((1,), (1,)), ((), ())), preferred_element_type=F32)

    state = {}
    pending = {}
    for idx in range(len(steps) + ATT_LOOKAHEAD):
        if idx < len(steps):
            pending[idx] = scores(*steps[idx])
        if idx < ATT_LOOKAHEAD:
            continue
        ci, h, g = steps[idx - ATT_LOOKAHEAD]
        s = pending.pop(idx - ATT_LOOKAHEAD)
        _, vt_ref, j = chunks[ci]
        vtj = vt_ref[0, h * ATT_AUG:(h + 1) * ATT_AUG, j * ATT_KEYS:(j + 1) * ATT_KEYS]
        cm = jnp.max(s, axis=0, keepdims=True)
        if ci == 0:
            m_new = cm
            acc = jnp.dot(vtj, jnp.exp2(s - m_new).astype(BF16), preferred_element_type=F32)
        else:
            m_old, acc_old = state[h, g]
            m_new = jnp.maximum(m_old, cm)
            acc = acc_old * jnp.exp2(m_old - m_new) + jnp.dot(
                vtj, jnp.exp2(s - m_new).astype(BF16), preferred_element_type=F32)
        state[h, g] = (m_new, acc)
    outs = []
    for h in range(MLA_HEADS):
        accs = [state[h, g][1] for g in range(n_qg)]
        outs.append(jnp.concatenate([a[:V_HEAD] * (1.0 / a[V_HEAD:V_HEAD + 1]) for a in accs], axis=1))
    o_ref[0] = jnp.concatenate(outs, axis=0).T.astype(BF16)


def _attention(q, ks, vts, tq):
    bsz, t, hq = q.shape
    n_seg = len(ks)
    hv = MLA_HEADS * V_HEAD
    kspec = [pl.BlockSpec((1, k.shape[1], hq), lambda b, i: (b, 0, 0)) for k in ks]
    vspec = [pl.BlockSpec((1, MLA_HEADS * ATT_AUG, vt.shape[2]), lambda b, i: (b, 0, 0)) for vt in vts]
    return pl.pallas_call(
        functools.partial(_attn_kernel, n_seg=n_seg),
        name="attention",
        grid=(bsz, t // tq),
        in_specs=[pl.BlockSpec((1, tq, hq), lambda b, i: (b, i, 0))] + kspec + vspec,
        out_specs=pl.BlockSpec((1, tq, hv), lambda b, i: (b, i, 0)),
        out_shape=jax.ShapeDtypeStruct((bsz, t, hv), BF16),
        compiler_params=pltpu.CompilerParams(
            dimension_semantics=("parallel", "arbitrary"), vmem_limit_bytes=VMEM_LIMIT),
    )(q, *ks, *vts)


FF_CHUNK = 1024


def _merge_mlp_kernel(x_ref, hc_ref, hm_ref, oa_ref, gates_ref, g1_ref, sc2_ref, sh2_ref, g2_ref,
                      wc_ref, wm_ref, wa_ref, wo_ref, bo_ref, ln1g_ref, ln1b_ref,
                      w1_ref, b1_ref, w2_ref, b2_ref, ln2g_ref, ln2b_ref, o_ref):
    d = D_MODEL
    yc = jnp.dot(hc_ref[0], wc_ref[...], preferred_element_type=F32)
    ym = jnp.dot(hm_ref[0], wm_ref[...], preferred_element_type=F32)
    ya = jnp.dot(oa_ref[0], wa_ref[...], preferred_element_type=F32)
    mix = (gates_ref[0, :, 0:d].astype(F32) * yc + gates_ref[0, :, d:2 * d].astype(F32) * ym
           + gates_ref[0, :, 2 * d:3 * d].astype(F32) * ya)
    y = jnp.dot(mix.astype(BF16), wo_ref[...], preferred_element_type=F32) + bo_ref[...]
    x1 = _layer_norm(ALPHA * x_ref[0] + g1_ref[...] * y, ln1g_ref[...], ln1b_ref[...])
    u2 = (x1 * (1.0 + sc2_ref[...]) + sh2_ref[...]).astype(BF16)
    acc = jnp.zeros(x1.shape, F32)
    for j in range(D_FF // FF_CHUNK):
        sl = slice(j * FF_CHUNK, (j + 1) * FF_CHUNK)
        hdn = jnp.maximum(jnp.dot(u2, w1_ref[:, sl], preferred_element_type=F32) + b1_ref[:, sl], 0.0)
        acc = acc + jnp.dot((hdn * hdn).astype(BF16), w2_ref[sl, :], preferred_element_type=F32)
    mlp = acc + b2_ref[...]
    o_ref[0] = _layer_norm(ALPHA * x1 + g2_ref[...] * mlp, ln2g_ref[...], ln2b_ref[...])


def _merge_mlp(x, hc, hm, oa, gates, mod, l, mod_row, wts, tm):
    bsz, t, d = x.shape
    tok = lambda n: pl.BlockSpec((1, tm, n), lambda b, i: (b, i, 0))
    mods = [_mod_spec(l, k, mod_row) for k in (MOD_G1, MOD_SC2, MOD_SH2, MOD_G2)]
    return pl.pallas_call(
        _merge_mlp_kernel,
        name="merge_mlp",
        grid=(bsz, t // tm),
        in_specs=[tok(d), tok(CONV_DIM), tok(ML_DIM), tok(MLA_HEADS * V_HEAD), tok(3 * d)] + mods + [
            _layer_spec(l, (CONV_DIM, d)), _layer_spec(l, (ML_DIM, d)), _layer_spec(l, (MLA_HEADS * V_HEAD, d)),
            _layer_spec(l, (d, d)), _layer_spec(l, (1, d)), _layer_spec(l, (1, d)), _layer_spec(l, (1, d)),
            _layer_spec(l, (d, D_FF)), _layer_spec(l, (1, D_FF)), _layer_spec(l, (D_FF, d)), _layer_spec(l, (1, d)),
            _layer_spec(l, (1, d)), _layer_spec(l, (1, d))],
        out_specs=tok(d),
        out_shape=jax.ShapeDtypeStruct((bsz, t, d), F32),
        compiler_params=pltpu.CompilerParams(
            dimension_semantics=("parallel", "arbitrary"), vmem_limit_bytes=VMEM_LIMIT),
    )(x, hc, hm, oa, gates, mod, mod, mod, mod,
      wts["w_conv_out"], wts["w_ml_out"], wts["w_mla_out"], wts["w_out"], wts["b_out"], wts["ln1_g"], wts["ln1_b"],
      wts["w1"], wts["b1"], wts["w2"], wts["b2"], wts["ln2_g"], wts["ln2_b"])


def _rope_runs():
    runs = []
    for half in range(2):
        for a in range(2):
            start = a * 2 * ROPE_FREQ + half * ROPE_FREQ
            runs.append(slice(start, start + ROPE_FREQ))
    return runs


def _pack_weights(w_in, b_in, w_dw, b_dw, cn_g, cn_b, w_conv_out, ml_g, w_ml_out, qn_g, w_uq, kvn_g, w_ukv,
                  w_mla_out, w_out, b_out, ln1_g, ln1_b, w1, b1, w2, b2, ln2_g, ln2_b):
    nl = w_in.shape[0]

    def packed(z):
        a, qkv, o, g, cq, ckv, kr, gates = (z[..., _SRC[i]:_SRC[i + 1]] for i in range(8))
        zeros = lambda n: jnp.zeros(z.shape[:-1] + (n,), z.dtype)
        kr_block = [zeros(QK_NOPE)] + [kr[..., s] for s in _rope_runs()] + [zeros(HEAD_PAD - QK_NOPE - QK_ROPE)]
        g_block = [g[..., i * ML_HEADS:(i + 1) * ML_HEADS] for i in (0, 2, 1, 3)] + [zeros(LANES - 4 * ML_HEADS)]
        return jnp.concatenate([a, qkv, o, cq, ckv] + kr_block + g_block + [gates], axis=-1)

    wq = w_uq.reshape(nl, Q_LORA, MLA_HEADS, QK_NOPE + QK_ROPE)
    wq_p = jnp.concatenate([wq[..., :QK_NOPE]] + [wq[..., QK_NOPE:][..., s] for s in _rope_runs()]
                           + [jnp.zeros(wq.shape[:-1] + (HEAD_PAD - QK_NOPE - QK_ROPE,), F32)], axis=-1)
    wkv = w_ukv.reshape(nl, KV_LORA, MLA_HEADS, QK_NOPE + V_HEAD)
    wkn_p = jnp.concatenate([wkv[..., :QK_NOPE], jnp.zeros(wkv.shape[:-1] + (HEAD_PAD - QK_NOPE,), F32)], axis=-1)
    wv_t = wkv[..., QK_NOPE:].reshape(nl, KV_LORA, MLA_HEADS * V_HEAD).transpose(0, 2, 1)
    ncg = CONV_DIM // LANES
    row = lambda z: z[:, None, :]
    return {
        "w_pack": packed(w_in).astype(BF16), "b_pack": row(packed(b_in)),
        "qn_g": row(qn_g), "w_uq": wq_p.reshape(nl, Q_LORA, MLA_HEADS * HEAD_PAD).astype(BF16),
        "kvn_g": row(kvn_g), "w_kn": wkn_p.reshape(nl, KV_LORA, MLA_HEADS * HEAD_PAD).astype(BF16),
        "w_v": wv_t.astype(BF16),
        "w_dw": jnp.pad(w_dw, ((0, 0), (0, 1), (0, 0))).reshape(nl, CONV_K + 1, ncg, LANES).transpose(0, 2, 1, 3),
        "b_dw": b_dw.reshape(nl, ncg, 1, LANES),
        "cn_g": row(cn_g), "cn_b": row(cn_b), "ml_g": row(ml_g),
        "w_conv_out": w_conv_out.astype(BF16), "w_ml_out": w_ml_out.astype(BF16),
        "w_mla_out": w_mla_out.astype(BF16), "w_out": w_out.astype(BF16), "b_out": row(b_out),
        "ln1_g": row(ln1_g), "ln1_b": row(ln1_b), "w1": w1.astype(BF16), "b1": row(b1),
        "w2": w2.astype(BF16), "b2": row(b2), "ln2_g": row(ln2_g), "ln2_b": row(ln2_b),
    }


def _rope_tables(n_tokens):
    rows = n_tokens // GRID_W
    rr, cc = np.meshgrid(np.arange(rows, dtype=np.float32), np.arange(GRID_W, dtype=np.float32), indexing="ij")
    inv = (np.float32(ROPE_THETA) ** (-np.arange(ROPE_FREQ, dtype=np.float32) / np.float32(ROPE_FREQ))).astype(np.float32)
    ang = np.stack([rr.reshape(-1), cc.reshape(-1)], -1)[..., None] * inv
    cos = np.cos(ang).astype(np.float32).reshape(n_tokens, ROPE_HALF)
    sin = np.sin(ang).astype(np.float32).reshape(n_tokens, ROPE_HALF)
    one = np.ones((n_tokens, QK_NOPE), np.float32)
    z_nope = np.zeros((n_tokens, QK_NOPE), np.float32)
    z_half = np.zeros((n_tokens, ROPE_HALF), np.float32)
    z_pad = np.zeros((n_tokens, HEAD_PAD - QK_NOPE - QK_ROPE), np.float32)
    tc = np.concatenate([one, cos, cos, z_pad], axis=1)
    tsm = np.concatenate([z_nope, -sin, z_half, z_pad], axis=1)
    tsp = np.concatenate([z_nope, z_half, sin, z_pad], axis=1)
    return tc, tsm, tsp


def _identity_tables(n_tokens):
    return (np.ones((n_tokens, HEAD_PAD), np.float32), np.zeros((n_tokens, HEAD_PAD), np.float32),
            np.zeros((n_tokens, HEAD_PAD), np.float32))


def kernel(x, c, ctx, c_ctx, w_mod, b_mod, w_in, b_in, w_dw, b_dw, conv_norm_g, conv_norm_b, w_conv_out,
           mlstm_norm_g, w_mlstm_out, q_norm_g, w_uq, kv_norm_g, w_ukv, w_mla_out, w_out, b_out,
           ln1_g, ln1_b, w1, b1, w2, b2, ln2_g, ln2_b):
    bsz, t_x, d = x.shape
    t_c = ctx.shape[1]
    n_rows = 24
    cc = jnp.concatenate([c, c_ctx[None, :], jnp.zeros((n_rows - bsz - 1, d), F32)], axis=0)
    mod = _modulation(cc, w_mod, b_mod).reshape(DEPTH, n_rows, N_MOD, 1, d)
    wts = _pack_weights(w_in, b_in, w_dw, b_dw, conv_norm_g, conv_norm_b, w_conv_out, mlstm_norm_g,
                        w_mlstm_out, q_norm_g, w_uq, kv_norm_g, w_ukv, w_mla_out, w_out, b_out,
                        ln1_g, ln1_b, w1, b1, w2, b2, ln2_g, ln2_b)
    rope_x = _rope_tables(t_x)
    rope_c = _identity_tables(t_c)
    ctx_row = bsz

    for l in range(DEPTH):
        with_ctx = l < DEPTH - 1
        glu_x, qk_x, vtm_x, so_x, gt_x, gates_x, q_x, k_x, vta_x = _in_projection(x, mod, l, None, wts, rope_x, 512)
        glu_c, qk_c, vtm_c, so_c, gt_c, gates_c, q_c, k_c, vta_c = _in_projection(
            ctx, mod, l, ctx_row, wts, rope_c, 256)

        hm_c, hm_x = _mlstm_branch((qk_c, vtm_c, gt_c, so_c), (qk_x, vtm_x, gt_x, so_x), l, wts)
        oa_x = _attention(q_x, [k_c, k_x], [vta_c, vta_x], 512)
        hc_x = _conv_branch(glu_x, l, wts)
        x_new = _merge_mlp(x, hc_x, hm_x, oa_x, gates_x, mod, l, None, wts, 512)
        if with_ctx:
            hc_c = _conv_branch(glu_c, l, wts)
            oa_c = _attention(q_c, [k_c], [vta_c], 256)
            ctx = _merge_mlp(ctx, hc_c, hm_c, oa_c, gates_c, mod, l, ctx_row, wts, 256)
        x = x_new
    return x
```

```python
import functools

import numpy as np
import jax
import jax.numpy as jnp
from jax import lax
from jax.experimental import pallas as pl
from jax.experimental.pallas import tpu as pltpu

F32 = jnp.float32
BF16 = jnp.bfloat16

D_MODEL = 1024
DEPTH = 2
GRID_W = 64
CONV_DIM = 512
CONV_K = 31
ML_HEADS = 4
ML_HEAD_DIM = 128
ML_DIM = ML_HEADS * ML_HEAD_DIM
ML_CHUNK = 256
MLA_HEADS = 8
QK_NOPE = 64
QK_ROPE = 32
V_HEAD = 64
Q_LORA = 768
KV_LORA = 256
ROPE_THETA = 10000.0
ROPE_FREQ = QK_ROPE // 4
MLA_SCALE = (QK_NOPE + QK_ROPE) ** -0.5
ML_SCALE = ML_HEAD_DIM ** -0.5
D_FF = 4 * D_MODEL
LN_EPS = 1e-5
ALPHA = (2 * DEPTH) ** 0.25
N_MOD = 6
MOD_SH1, MOD_SC1, MOD_G1, MOD_SH2, MOD_SC2, MOD_G2 = range(N_MOD)

LANES = 128
SUBLANES = 8
HEAD_PAD = LANES
ROPE_HALF = QK_ROPE // 2
ONES_ROWS = 16
ML_AUG = ML_HEAD_DIM + ONES_ROWS
ATT_AUG = V_HEAD + ONES_ROWS
Q_SCALE = MLA_SCALE * float(np.log2(np.e))
VMEM_LIMIT = 56 * 1024 * 1024

OFF_A = 0
OFF_QKV = OFF_A + 2 * CONV_DIM
OFF_O = OFF_QKV + 3 * ML_DIM
OFF_CQ = OFF_O + ML_DIM
OFF_CKV = OFF_CQ + Q_LORA
OFF_KR = OFF_CKV + KV_LORA
OFF_GIF = OFF_KR + HEAD_PAD
OFF_GATES = OFF_GIF + LANES
N_PACK = OFF_GATES + 3 * D_MODEL

_COLS = (2 * CONV_DIM, 3 * ML_DIM, ML_DIM, 4 * ML_HEADS, Q_LORA, KV_LORA, QK_ROPE, 3 * D_MODEL)
_SRC = tuple(int(s) for s in np.cumsum((0,) + _COLS))


def _layer_spec(l, shape):
    nd = len(shape)
    return pl.BlockSpec((None,) + tuple(shape), lambda *_: (l,) + (0,) * nd, pipeline_mode=pl.Buffered(1))


def _mod_spec(l, k, row=None):
    block = (None, None, None, 1, D_MODEL)
    if row is None:
        return pl.BlockSpec(block, lambda b, i: (l, b, k, 0, 0))
    return pl.BlockSpec(block, lambda b, i: (l, row, k, 0, 0))


def _sigmoid(z):
    return 1.0 / (1.0 + jnp.exp(-z))


def _log_sigmoid(z):
    return jnp.minimum(z, 0.0) - jnp.log1p(jnp.exp(-jnp.abs(z)))


def _layer_norm(z, g, b):
    mu = jnp.mean(z, axis=-1, keepdims=True)
    zc = z - mu
    var = jnp.mean(zc * zc, axis=-1, keepdims=True)
    return zc * lax.rsqrt(var + LN_EPS) * g + b


def _mod_kernel(c_ref, w_ref, b_ref, o_ref):
    c = c_ref[...]
    s = (c * _sigmoid(c)).astype(BF16)
    o_ref[0] = jnp.dot(s, w_ref[0].astype(BF16), preferred_element_type=F32) + b_ref[0]


def _modulation(cc, w_mod, b_mod):
    nl, d, n = w_mod.shape
    r = cc.shape[0]
    tn = 1024
    return pl.pallas_call(
        _mod_kernel,
        name="modulation",
        grid=(nl, n // tn),
        in_specs=[
            pl.BlockSpec((r, d), lambda l, j: (0, 0)),
            pl.BlockSpec((1, d, tn), lambda l, j: (l, 0, j)),
            pl.BlockSpec((1, 1, tn), lambda l, j: (l, 0, j)),
        ],
        out_specs=pl.BlockSpec((1, r, tn), lambda l, j: (l, 0, j)),
        out_shape=jax.ShapeDtypeStruct((nl, r, n), F32),
        compiler_params=pltpu.CompilerParams(
            dimension_semantics=("arbitrary", "arbitrary"), vmem_limit_bytes=VMEM_LIMIT),
    )(cc, w_mod, b_mod.reshape(nl, 1, n))


def _rope(z, tc, tsm, tsp):
    return z * tc + pltpu.roll(z, HEAD_PAD - ROPE_HALF, 1) * tsm + pltpu.roll(z, ROPE_HALF, 1) * tsp


def _inproj_kernel(x_ref, sc_ref, sh_ref, w_ref, b_ref, qng_ref, wuq_ref, kvg_ref, wkn_ref, wv_ref,
                   tc_ref, tsm_ref, tsp_ref,
                   glu_ref, qk_ref, vtm_ref, so_ref, gt_ref, gates_ref, q_ref, k_ref, vta_ref):
    u = (x_ref[0] * (1.0 + sc_ref[...]) + sh_ref[...]).astype(BF16)
    tm = u.shape[0]
    ones_rows = jnp.ones((ONES_ROWS, tm), BF16)

    def proj(off, n):
        return jnp.dot(u, w_ref[:, off:off + n], preferred_element_type=F32) + b_ref[:, off:off + n]

    cq = proj(OFF_CQ, Q_LORA)
    nq = (cq * lax.rsqrt(jnp.mean(cq * cq, axis=-1, keepdims=True) + LN_EPS) * qng_ref[...]).astype(BF16)
    ckv = proj(OFF_CKV, KV_LORA)
    nkv = (ckv * lax.rsqrt(jnp.mean(ckv * ckv, axis=-1, keepdims=True) + LN_EPS) * kvg_ref[...]).astype(BF16)

    a = proj(OFF_A, 2 * CONV_DIM)
    glu_ref[0] = (a[:, :CONV_DIM] * _sigmoid(a[:, CONV_DIM:])).astype(BF16)

    qk_ref[0, :, :ML_DIM] = proj(OFF_QKV, ML_DIM).astype(BF16)
    qk_ref[0, :, ML_DIM:] = (proj(OFF_QKV + ML_DIM, ML_DIM) * ML_SCALE).astype(BF16)
    vt = proj(OFF_QKV + 2 * ML_DIM, ML_DIM).T.astype(BF16)
    gt = proj(OFF_GIF, LANES).T
    for c in range(tm // ML_CHUNK):
        cs = slice(c * ML_CHUNK, (c + 1) * ML_CHUNK)
        gt_ref[0, c] = gt[:4 * ML_HEADS, cs]
        for h in range(ML_HEADS):
            vtm_ref[0, c, h * ML_AUG:h * ML_AUG + ML_HEAD_DIM, :] = vt[h * ML_HEAD_DIM:(h + 1) * ML_HEAD_DIM, cs]
            vtm_ref[0, c, h * ML_AUG + ML_HEAD_DIM:(h + 1) * ML_AUG, :] = ones_rows[:, cs]
    so_ref[0] = _sigmoid(proj(OFF_O, ML_DIM)).astype(BF16)
    for j in range(3):
        gates_ref[0, :, j * D_MODEL:(j + 1) * D_MODEL] = _sigmoid(
            proj(OFF_GATES + j * D_MODEL, D_MODEL)).astype(BF16)

    tc, tsm, tsp = tc_ref[...], tsm_ref[...], tsp_ref[...]

    q = jnp.dot(nq, wuq_ref[...], preferred_element_type=F32)
    for h in range(MLA_HEADS):
        sl = slice(h * HEAD_PAD, (h + 1) * HEAD_PAD)
        q_ref[0, :, sl] = (_rope(q[:, sl], tc, tsm, tsp) * Q_SCALE).astype(BF16)

    kn = jnp.dot(nkv, wkn_ref[...], preferred_element_type=F32)
    kr = _rope(proj(OFF_KR, HEAD_PAD), tc, tsm, tsp)
    for h in range(MLA_HEADS):
        sl = slice(h * HEAD_PAD, (h + 1) * HEAD_PAD)
        k_ref[0, :, sl] = (kn[:, sl] + kr).astype(BF16)
    vta = lax.dot_general(wv_ref[...], nkv, (((1,), (1,)), ((), ())),
                          preferred_element_type=F32).astype(BF16)
    for h in range(MLA_HEADS):
        vta_ref[0, h * ATT_AUG:h * ATT_AUG + V_HEAD, :] = vta[h * V_HEAD:(h + 1) * V_HEAD, :]
        vta_ref[0, h * ATT_AUG + V_HEAD:(h + 1) * ATT_AUG, :] = ones_rows


def _in_projection(x, mod, l, mod_row, wts, tables, tm):
    bsz, t, d = x.shape
    tok = lambda n: pl.BlockSpec((1, tm, n), lambda b, i: (b, i, 0))
    tab = pl.BlockSpec((tm, HEAD_PAD), lambda b, i: (i, 0))
    hq = MLA_HEADS * HEAD_PAD
    nc, cpt = t // ML_CHUNK, tm // ML_CHUNK
    out_shape = [
        jax.ShapeDtypeStruct((bsz, t, CONV_DIM), BF16),
        jax.ShapeDtypeStruct((bsz, t, 2 * ML_DIM), BF16),
        jax.ShapeDtypeStruct((bsz, nc, ML_HEADS * ML_AUG, ML_CHUNK), BF16),
        jax.ShapeDtypeStruct((bsz, t, ML_DIM), BF16),
        jax.ShapeDtypeStruct((bsz, nc, 4 * ML_HEADS, ML_CHUNK), F32),
        jax.ShapeDtypeStruct((bsz, t, 3 * D_MODEL), BF16),
        jax.ShapeDtypeStruct((bsz, t, hq), BF16),
        jax.ShapeDtypeStruct((bsz, t, hq), BF16),
        jax.ShapeDtypeStruct((bsz, MLA_HEADS * ATT_AUG, t), BF16),
    ]
    out_specs = [tok(CONV_DIM), tok(2 * ML_DIM),
                 pl.BlockSpec((1, cpt, ML_HEADS * ML_AUG, ML_CHUNK), lambda b, i: (b, i, 0, 0)),
                 tok(ML_DIM),
                 pl.BlockSpec((1, cpt, 4 * ML_HEADS, ML_CHUNK), lambda b, i: (b, i, 0, 0)),
                 tok(3 * D_MODEL), tok(hq), tok(hq),
                 pl.BlockSpec((1, MLA_HEADS * ATT_AUG, tm), lambda b, i: (b, 0, i))]
    return pl.pallas_call(
        _inproj_kernel,
        name="in_projection",
        grid=(bsz, t // tm),
        in_specs=[tok(d), _mod_spec(l, MOD_SC1, mod_row), _mod_spec(l, MOD_SH1, mod_row),
                  _layer_spec(l, (d, N_PACK)), _layer_spec(l, (1, N_PACK)),
                  _layer_spec(l, (1, Q_LORA)), _layer_spec(l, (Q_LORA, hq)),
                  _layer_spec(l, (1, KV_LORA)), _layer_spec(l, (KV_LORA, hq)),
                  _layer_spec(l, (MLA_HEADS * V_HEAD, KV_LORA)),
                  tab, tab, tab],
        out_specs=out_specs,
        out_shape=out_shape,
        compiler_params=pltpu.CompilerParams(
            dimension_semantics=("parallel", "arbitrary"), vmem_limit_bytes=VMEM_LIMIT),
    )(x, mod, mod, wts["w_pack"], wts["b_pack"], wts["qn_g"], wts["w_uq"], wts["kvn_g"], wts["w_kn"], wts["w_v"],
      *tables)


CONV_HALO = 16
CONV_TILE = 128


def _conv_kernel(h_ref, wdw_ref, bdw_ref, g_ref, b_ref, o_ref, hp, stage, cv, shifted, *, t):
    nt = t // CONV_TILE
    ncg = CONV_DIM // LANES
    first = CONV_HALO - CONV_K // 2
    span = CONV_TILE + (CONV_K // SUBLANES) * SUBLANES
    for cg in range(ncg):
        hp[cg, 0:CONV_HALO, :] = jnp.zeros((CONV_HALO, LANES), F32)
        hp[cg, t + CONV_HALO:t + 2 * CONV_HALO, :] = jnp.zeros((CONV_HALO, LANES), F32)

    def fill(i, carry):
        r0 = pl.multiple_of(i * CONV_TILE, CONV_TILE)
        hx = h_ref[0, pl.ds(r0, CONV_TILE), :].astype(F32)
        for cg in range(ncg):
            hp[cg, pl.ds(r0 + CONV_HALO, CONV_TILE), :] = hx[:, cg * LANES:(cg + 1) * LANES]
        return carry

    lax.fori_loop(0, nt, fill, 0)

    def tile(i, carry):
        r0 = pl.multiple_of(i * CONV_TILE, CONV_TILE)

        def group(cg, inner):
            stage[...] = hp[cg, pl.ds(r0, CONV_TILE + 2 * CONV_HALO), :]
            acc = jnp.zeros((CONV_TILE, LANES), F32)
            for r in range(SUBLANES):
                shifted[r] = stage[r:r + span, :]
                for k in range(CONV_K):
                    if (first + k) % SUBLANES == r:
                        a = first + k - r
                        acc = acc + shifted[r, a:a + CONV_TILE, :] * wdw_ref[cg, k:k + 1, :]
            cv[cg] = acc + bdw_ref[cg]
            return inner

        lax.fori_loop(0, ncg, group, 0)
        z = jnp.concatenate([cv[cg] for cg in range(ncg)], axis=1)
        z = _layer_norm(z, g_ref[...], b_ref[...])
        o_ref[0, pl.ds(r0, CONV_TILE), :] = (z * _sigmoid(z)).astype(BF16)
        return carry

    lax.fori_loop(0, nt, tile, 0)


def _conv_branch(glu, l, wts):
    bsz, t, _ = glu.shape
    ncg = CONV_DIM // LANES
    return pl.pallas_call(
        functools.partial(_conv_kernel, t=t),
        name="conv",
        grid=(bsz,),
        in_specs=[pl.BlockSpec((1, t, CONV_DIM), lambda b: (b, 0, 0)),
                  _layer_spec(l, (ncg, CONV_K + 1, LANES)), _layer_spec(l, (ncg, 1, LANES)),
                  _layer_spec(l, (1, CONV_DIM)), _layer_spec(l, (1, CONV_DIM))],
        out_specs=pl.BlockSpec((1, t, CONV_DIM), lambda b: (b, 0, 0)),
        out_shape=jax.ShapeDtypeStruct((bsz, t, CONV_DIM), BF16),
        scratch_shapes=[pltpu.VMEM((ncg, t + 2 * CONV_HALO, LANES), F32),
                        pltpu.VMEM((CONV_TILE + 2 * CONV_HALO, LANES), F32),
                        pltpu.VMEM((ncg, CONV_TILE, LANES), F32),
                        pltpu.VMEM((SUBLANES, CONV_TILE + (CONV_K // SUBLANES) * SUBLANES, LANES), F32)],
        compiler_params=pltpu.CompilerParams(
            dimension_semantics=("parallel",), vmem_limit_bytes=VMEM_LIMIT),
    )(glu, wts["w_dw"], wts["b_dw"], wts["cn_g"], wts["cn_b"])


N_CHAIN = 2 * ML_HEADS


def _mlstm_kernel(qk_c_ref, vt_c_ref, gt_c_ref, so_c_ref, qk_x_ref, vt_x_ref, gt_x_ref, so_x_ref, mlg_ref,
                  hn_c_ref, hn_x_ref, hf_c, hb_c, hf_x, hb_x, c_scr, m_scr, *, nc_c, nc_x):
    L = ML_CHUNK
    dh = ML_HEAD_DIM
    row = lax.broadcasted_iota(jnp.int32, (L, L), 0)
    col = lax.broadcasted_iota(jnp.int32, (L, L), 1)
    upper = row <= col
    lower = row >= col
    tri2 = jnp.concatenate([upper.astype(F32), lower.astype(F32)], axis=1)
    is_fwd = lax.broadcasted_iota(jnp.int32, (N_CHAIN, L), 0) < ML_HEADS

    c_scr[...] = jnp.zeros(c_scr.shape, F32)
    m_scr[...] = jnp.zeros(m_scr.shape, F32)

    def iteration(qk_ref, vt_ref, gt_ref, hf, hb, cf, cb):
        gf = gt_ref[0, cf]
        gb = gt_ref[0, cb]
        li = jnp.where(is_fwd, gf[:N_CHAIN], gb[:N_CHAIN])
        lf = _log_sigmoid(jnp.where(is_fwd, gf[N_CHAIN:], gb[N_CHAIN:]))
        cum2 = jnp.dot(lf, tri2, preferred_element_type=F32, precision=lax.Precision.HIGHEST)
        bc = jnp.where(is_fwd, cum2[:, :L], cum2[:, L:])
        r = li - bc
        b_last = jnp.sum(lf, axis=1, keepdims=True)
        m = m_scr[:, 0:1]
        w_src = b_last + r
        m_new = jnp.maximum(b_last + m, jnp.max(w_src, axis=1, keepdims=True))
        a_state = jnp.exp(b_last + m - m_new)
        w = jnp.exp(w_src - m_new)
        inter = bc + m
        r_cols = jnp.concatenate([r, jnp.zeros((L - N_CHAIN, L), F32)], axis=0).T
        m_scr[...] = jnp.broadcast_to(m_new, m_scr.shape)
        early = []
        for j in range(N_CHAIN):
            d, h = divmod(j, ML_HEADS)
            cc = cf if d == 0 else cb
            r0 = pl.multiple_of(cc * L, L)
            q = qk_ref[0, pl.ds(r0, L), h * dh:(h + 1) * dh]
            k = qk_ref[0, pl.ds(r0, L), ML_DIM + h * dh:ML_DIM + (h + 1) * dh]
            vt = vt_ref[0, cc, h * ML_AUG:(h + 1) * ML_AUG, :]
            ct = c_scr[j]
            both = lax.dot_general(jnp.concatenate([k, ct.astype(BF16)], axis=0), q, (((1,), (1,)), ((), ())),
                                   preferred_element_type=F32)
            wv = (vt.astype(F32) * w[j:j + 1, :]).astype(BF16)
            c_scr[j] = a_state[j:j + 1, :] * ct + jnp.dot(wv, k, preferred_element_type=F32)
            early.append((both, vt))
        for j in range(N_CHAIN):
            d, h = divmod(j, ML_HEADS)
            cc = cf if d == 0 else cb
            mask = upper if d == 0 else lower
            both, vt = early[j]
            log_d = jnp.where(mask, bc[j:j + 1, :] + r_cols[:, j:j + 1], -jnp.inf)
            m_row = jnp.maximum(inter[j:j + 1, :], jnp.max(log_d, axis=0, keepdims=True))
            a_inter = jnp.exp(inter[j:j + 1, :] - m_row)
            dmat = jnp.exp(log_d - m_row)
            pt = (both[:L] * dmat).astype(BF16)
            num_aug = a_inter * both[L:] + jnp.dot(vt, pt, preferred_element_type=F32)
            den = num_aug[dh:dh + 1, :]
            hs = hf if d == 0 else hb
            hs[cc, h * dh:(h + 1) * dh, :] = num_aug[:dh] / jnp.maximum(jnp.abs(den), jnp.exp(-m_row))

    def scan(qk_ref, vt_ref, gt_ref, hf, hb, nc):
        def body(i, carry):
            iteration(qk_ref, vt_ref, gt_ref, hf, hb, i, nc - 1 - i)
            return carry

        lax.fori_loop(0, nc, body, 0, unroll=min(2, nc))

    scan(qk_c_ref, vt_c_ref, gt_c_ref, hf_c, hb_c, nc_c)
    scan(qk_x_ref, vt_x_ref, gt_x_ref, hf_x, hb_x, nc_x)

    def finish(hf, hb, so_ref, out_ref, nc):
        def body(c, carry):
            r0 = pl.multiple_of(c * L, L)
            hsum = hf[c] + hb[c]
            for h in range(ML_HEADS):
                sl = slice(h * dh, (h + 1) * dh)
                z = hsum[sl, :]
                mu = jnp.mean(z, axis=0, keepdims=True)
                zc = z - mu
                var = jnp.mean(zc * zc, axis=0, keepdims=True)
                hn = (zc * lax.rsqrt(var + LN_EPS)).T * mlg_ref[:, sl]
                out_ref[0, pl.ds(r0, L), sl] = (so_ref[0, pl.ds(r0, L), sl].astype(F32) * hn).astype(BF16)
            return carry

        lax.fori_loop(0, nc, body, 0)

    finish(hf_c, hb_c, so_c_ref, hn_c_ref, nc_c)
    finish(hf_x, hb_x, so_x_ref, hn_x_ref, nc_x)


def _mlstm_branch(pc, px, l, wts):
    bsz, t_c, _ = pc[0].shape
    t_x = px[0].shape[1]
    nc_c, nc_x = t_c // ML_CHUNK, t_x // ML_CHUNK
    seq = lambda t, n: pl.BlockSpec((1, t, n), lambda b: (b, 0, 0))
    chunked = lambda nc, n: pl.BlockSpec((1, nc, n, ML_CHUNK), lambda b: (b, 0, 0, 0))
    ins = lambda t, nc: [seq(t, 2 * ML_DIM), chunked(nc, ML_HEADS * ML_AUG), chunked(nc, 4 * ML_HEADS),
                         seq(t, ML_DIM)]
    hbuf = lambda nc: pltpu.VMEM((nc, ML_DIM, ML_CHUNK), F32)
    return pl.pallas_call(
        functools.partial(_mlstm_kernel, nc_c=nc_c, nc_x=nc_x),
        name="mlstm",
        grid=(bsz,),
        in_specs=ins(t_c, nc_c) + ins(t_x, nc_x) + [_layer_spec(l, (1, ML_DIM))],
        out_specs=[seq(t_c, ML_DIM), seq(t_x, ML_DIM)],
        out_shape=[jax.ShapeDtypeStruct((bsz, t_c, ML_DIM), BF16),
                   jax.ShapeDtypeStruct((bsz, t_x, ML_DIM), BF16)],
        scratch_shapes=[hbuf(nc_c), hbuf(nc_c), hbuf(nc_x), hbuf(nc_x),
                        pltpu.VMEM((N_CHAIN, ML_AUG, ML_HEAD_DIM), F32),
                        pltpu.VMEM((N_CHAIN, LANES), F32)],
        compiler_params=pltpu.CompilerParams(
            dimension_semantics=("parallel",), vmem_limit_bytes=VMEM_LIMIT),
    )(*pc, *px, wts["ml_g"])


ATT_KEYS = 256
ATT_QG = 256
ATT_LOOKAHEAD = 4


def _attn_kernel(*refs, n_seg):
    q_ref = refs[0]
    k_refs = refs[1:1 + n_seg]
    vt_refs = refs[1 + n_seg:1 + 2 * n_seg]
    o_ref = refs[1 + 2 * n_seg]
    n_qg = q_ref.shape[1] // ATT_QG
    chunks = [(k_ref, vt_ref, j) for k_ref, vt_ref in zip(k_refs, vt_refs)
              for j in range(k_ref.shape[1] // ATT_KEYS)]
    steps = [(ci, h, g) for ci in range(len(chunks)) for h in range(MLA_HEADS) for g in range(n_qg)]

    def scores(ci, h, g):
        k_ref, _, j = chunks[ci]
        hs = slice(h * HEAD_PAD, (h + 1) * HEAD_PAD)
        return lax.dot_general(k_ref[0, j * ATT_KEYS:(j + 1) * ATT_KEYS, hs], q_ref[0, g * ATT_QG:(g + 1) * ATT_QG, hs],
                               (((1,), (1,)), ((), ())), preferred_element_type=F32)

    state = {}
    pending = {}
    for idx in range(len(steps) + ATT_LOOKAHEAD):
        if idx < len(steps):
            pending[idx] = scores(*steps[idx])
        if idx < ATT_LOOKAHEAD:
            continue
        ci, h, g = steps[idx - ATT_LOOKAHEAD]
        s = pending.pop(idx - ATT_LOOKAHEAD)
        _, vt_ref, j = chunks[ci]
        vtj = vt_ref[0, h * ATT_AUG:(h + 1) * ATT_AUG, j * ATT_KEYS:(j + 1) * ATT_KEYS]
        cm = jnp.max(s, axis=0, keepdims=True)
        if ci == 0:
            m_new = cm
            acc = jnp.dot(vtj, jnp.exp2(s - m_new).astype(BF16), preferred_element_type=F32)
        else:
            m_old, acc_old = state[h, g]
            m_new = jnp.maximum(m_old, cm)
            acc = acc_old * jnp.exp2(m_old - m_new) + jnp.dot(
                vtj, jnp.exp2(s - m_new).astype(BF16), preferred_element_type=F32)
        state[h, g] = (m_new, acc)
    outs = []
    for h in range(MLA_HEADS):
        accs = [state[h, g][1] for g in range(n_qg)]
        outs.append(jnp.concatenate([a[:V_HEAD] * (1.0 / a[V_HEAD:V_HEAD + 1]) for a in accs], axis=1))
    o_ref[0] = jnp.concatenate(outs, axis=0).T.astype(BF16)


def _attention(q, ks, vts, tq):
    bsz, t, hq = q.shape
    n_seg = len(ks)
    hv = MLA_HEADS * V_HEAD
    kspec = [pl.BlockSpec((1, k.shape[1], hq), lambda b, i: (b, 0, 0)) for k in ks]
    vspec = [pl.BlockSpec((1, MLA_HEADS * ATT_AUG, vt.shape[2]), lambda b, i: (b, 0, 0)) for vt in vts]
    return pl.pallas_call(
        functools.partial(_attn_kernel, n_seg=n_seg),
        name="attention",
        grid=(bsz, t // tq),
        in_specs=[pl.BlockSpec((1, tq, hq), lambda b, i: (b, i, 0))] + kspec + vspec,
        out_specs=pl.BlockSpec((1, tq, hv), lambda b, i: (b, i, 0)),
        out_shape=jax.ShapeDtypeStruct((bsz, t, hv), BF16),
        compiler_params=pltpu.CompilerParams(
            dimension_semantics=("parallel", "arbitrary"), vmem_limit_bytes=VMEM_LIMIT),
    )(q, *ks, *vts)


FF_CHUNK = 1024
MERGE_GROUPS = 2


def _merge_mlp_kernel(x_ref, hc_ref, hm_ref, oa_ref, gates_ref, g1_ref, sc2_ref, sh2_ref, g2_ref,
                      wc_ref, wm_ref, wa_ref, wo_ref, bo_ref, ln1g_ref, ln1b_ref,
                      w1_ref, b1_ref, w2_ref, b2_ref, ln2g_ref, ln2b_ref, o_ref):
    d = D_MODEL
    tm = x_ref.shape[1]
    rows = [slice(g * tm // MERGE_GROUPS, (g + 1) * tm // MERGE_GROUPS) for g in range(MERGE_GROUPS)]
    dot = functools.partial(jnp.dot, preferred_element_type=F32)
    yc = [dot(hc_ref[0, r, :], wc_ref[...]) for r in rows]
    ym = [dot(hm_ref[0, r, :], wm_ref[...]) for r in rows]
    ya = [dot(oa_ref[0, r, :], wa_ref[...]) for r in rows]
    mix = [(gates_ref[0, r, 0:d].astype(F32) * c + gates_ref[0, r, d:2 * d].astype(F32) * m
            + gates_ref[0, r, 2 * d:3 * d].astype(F32) * a).astype(BF16) for r, c, m, a in zip(rows, yc, ym, ya)]
    y = [dot(mx, wo_ref[...]) + bo_ref[...] for mx in mix]
    x1 = [_layer_norm(ALPHA * x_ref[0, r, :] + g1_ref[...] * yy, ln1g_ref[...], ln1b_ref[...])
          for r, yy in zip(rows, y)]
    u2 = [(xx * (1.0 + sc2_ref[...]) + sh2_ref[...]).astype(BF16) for xx in x1]
    acc = [jnp.zeros(xx.shape, F32) for xx in x1]
    for j in range(D_FF // FF_CHUNK):
        sl = slice(j * FF_CHUNK, (j + 1) * FF_CHUNK)
        hdn = [jnp.maximum(dot(u, w1_ref[:, sl]) + b1_ref[:, sl], 0.0) for u in u2]
        acc = [ac + dot((hd * hd).astype(BF16), w2_ref[sl, :]) for ac, hd in zip(acc, hdn)]
    for r, xx, ac in zip(rows, x1, acc):
        o_ref[0, r, :] = _layer_norm(ALPHA * xx + g2_ref[...] * (ac + b2_ref[...]), ln2g_ref[...], ln2b_ref[...])


def _merge_mlp(x, hc, hm, oa, gates, mod, l, mod_row, wts, tm):
    bsz, t, d = x.shape
    tok = lambda n: pl.BlockSpec((1, tm, n), lambda b, i: (b, i, 0))
    mods = [_mod_spec(l, k, mod_row) for k in (MOD_G1, MOD_SC2, MOD_SH2, MOD_G2)]
    return pl.pallas_call(
        _merge_mlp_kernel,
        name="merge_mlp",
        grid=(bsz, t // tm),
        in_specs=[tok(d), tok(CONV_DIM), tok(ML_DIM), tok(MLA_HEADS * V_HEAD), tok(3 * d)] + mods + [
            _layer_spec(l, (CONV_DIM, d)), _layer_spec(l, (ML_DIM, d)), _layer_spec(l, (MLA_HEADS * V_HEAD, d)),
            _layer_spec(l, (d, d)), _layer_spec(l, (1, d)), _layer_spec(l, (1, d)), _layer_spec(l, (1, d)),
            _layer_spec(l, (d, D_FF)), _layer_spec(l, (1, D_FF)), _layer_spec(l, (D_FF, d)), _layer_spec(l, (1, d)),
            _layer_spec(l, (1, d)), _layer_spec(l, (1, d))],
        out_specs=tok(d),
        out_shape=jax.ShapeDtypeStruct((bsz, t, d), F32),
        compiler_params=pltpu.CompilerParams(
            dimension_semantics=("parallel", "arbitrary"), vmem_limit_bytes=VMEM_LIMIT),
    )(x, hc, hm, oa, gates, mod, mod, mod, mod,
      wts["w_conv_out"], wts["w_ml_out"], wts["w_mla_out"], wts["w_out"], wts["b_out"], wts["ln1_g"], wts["ln1_b"],
      wts["w1"], wts["b1"], wts["w2"], wts["b2"], wts["ln2_g"], wts["ln2_b"])


def _rope_runs():
    runs = []
    for half in range(2):
        for a in range(2):
            start = a * 2 * ROPE_FREQ + half * ROPE_FREQ
            runs.append(slice(start, start + ROPE_FREQ))
    return runs


def _pack_weights(w_in, b_in, w_dw, b_dw, cn_g, cn_b, w_conv_out, ml_g, w_ml_out, qn_g, w_uq, kvn_g, w_ukv,
                  w_mla_out, w_out, b_out, ln1_g, ln1_b, w1, b1, w2, b2, ln2_g, ln2_b):
    nl = w_in.shape[0]

    def packed(z):
        a, qkv, o, g, cq, ckv, kr, gates = (z[..., _SRC[i]:_SRC[i + 1]] for i in range(8))
        zeros = lambda n: jnp.zeros(z.shape[:-1] + (n,), z.dtype)
        kr_block = [zeros(QK_NOPE)] + [kr[..., s] for s in _rope_runs()] + [zeros(HEAD_PAD - QK_NOPE - QK_ROPE)]
        g_block = [g[..., i * ML_HEADS:(i + 1) * ML_HEADS] for i in (0, 2, 1, 3)] + [zeros(LANES - 4 * ML_HEADS)]
        return jnp.concatenate([a, qkv, o, cq, ckv] + kr_block + g_block + [gates], axis=-1)

    wq = w_uq.reshape(nl, Q_LORA, MLA_HEADS, QK_NOPE + QK_ROPE)
    wq_p = jnp.concatenate([wq[..., :QK_NOPE]] + [wq[..., QK_NOPE:][..., s] for s in _rope_runs()]
                           + [jnp.zeros(wq.shape[:-1] + (HEAD_PAD - QK_NOPE - QK_ROPE,), F32)], axis=-1)
    wkv = w_ukv.reshape(nl, KV_LORA, MLA_HEADS, QK_NOPE + V_HEAD)
    wkn_p = jnp.concatenate([wkv[..., :QK_NOPE], jnp.zeros(wkv.shape[:-1] + (HEAD_PAD - QK_NOPE,), F32)], axis=-1)
    wv_t = wkv[..., QK_NOPE:].reshape(nl, KV_LORA, MLA_HEADS * V_HEAD).transpose(0, 2, 1)
    ncg = CONV_DIM // LANES
    row = lambda z: z[:, None, :]
    return {
        "w_pack": packed(w_in).astype(BF16), "b_pack": row(packed(b_in)),
        "qn_g": row(qn_g), "w_uq": wq_p.reshape(nl, Q_LORA, MLA_HEADS * HEAD_PAD).astype(BF16),
        "kvn_g": row(kvn_g), "w_kn": wkn_p.reshape(nl, KV_LORA, MLA_HEADS * HEAD_PAD).astype(BF16),
        "w_v": wv_t.astype(BF16),
        "w_dw": jnp.pad(w_dw, ((0, 0), (0, 1), (0, 0))).reshape(nl, CONV_K + 1, ncg, LANES).transpose(0, 2, 1, 3),
        "b_dw": b_dw.reshape(nl, ncg, 1, LANES),
        "cn_g": row(cn_g), "cn_b": row(cn_b), "ml_g": row(ml_g),
        "w_conv_out": w_conv_out.astype(BF16), "w_ml_out": w_ml_out.astype(BF16),
        "w_mla_out": w_mla_out.astype(BF16), "w_out": w_out.astype(BF16), "b_out": row(b_out),
        "ln1_g": row(ln1_g), "ln1_b": row(ln1_b), "w1": w1.astype(BF16), "b1": row(b1),
        "w2": w2.astype(BF16), "b2": row(b2), "ln2_g": row(ln2_g), "ln2_b": row(ln2_b),
    }


def _rope_tables(n_tokens):
    rows = n_tokens // GRID_W
    rr, cc = np.meshgrid(np.arange(rows, dtype=np.float32), np.arange(GRID_W, dtype=np.float32), indexing="ij")
    inv = (np.float32(ROPE_THETA) ** (-np.arange(ROPE_FREQ, dtype=np.float32) / np.float32(ROPE_FREQ))).astype(np.float32)
    ang = np.stack([rr.reshape(-1), cc.reshape(-1)], -1)[..., None] * inv
    cos = np.cos(ang).astype(np.float32).reshape(n_tokens, ROPE_HALF)
    sin = np.sin(ang).astype(np.float32).reshape(n_tokens, ROPE_HALF)
    one = np.ones((n_tokens, QK_NOPE), np.float32)
    z_nope = np.zeros((n_tokens, QK_NOPE), np.float32)
    z_half = np.zeros((n_tokens, ROPE_HALF), np.float32)
    z_pad = np.zeros((n_tokens, HEAD_PAD - QK_NOPE - QK_ROPE), np.float32)
    tc = np.concatenate([one, cos, cos, z_pad], axis=1)
    tsm = np.concatenate([z_nope, -sin, z_half, z_pad], axis=1)
    tsp = np.concatenate([z_nope, z_half, sin, z_pad], axis=1)
    return tc, tsm, tsp


def _identity_tables(n_tokens):
    return (np.ones((n_tokens, HEAD_PAD), np.float32), np.zeros((n_tokens, HEAD_PAD), np.float32),
            np.zeros((n_tokens, HEAD_PAD), np.float32))


def kernel(x, c, ctx, c_ctx, w_mod, b_mod, w_in, b_in, w_dw, b_dw, conv_norm_g, conv_norm_b, w_conv_out,
           mlstm_norm_g, w_mlstm_out, q_norm_g, w_uq, kv_norm_g, w_ukv, w_mla_out, w_out, b_out,
           ln1_g, ln1_b, w1, b1, w2, b2, ln2_g, ln2_b):
    bsz, t_x, d = x.shape
    t_c = ctx.shape[1]
    n_rows = 24
    cc = jnp.concatenate([c, c_ctx[None, :], jnp.zeros((n_rows - bsz - 1, d), F32)], axis=0)
    mod = _modulation(cc, w_mod, b_mod).reshape(DEPTH, n_rows, N_MOD, 1, d)
    wts = _pack_weights(w_in, b_in, w_dw, b_dw, conv_norm_g, conv_norm_b, w_conv_out, mlstm_norm_g,
                        w_mlstm_out, q_norm_g, w_uq, kv_norm_g, w_ukv, w_mla_out, w_out, b_out,
                        ln1_g, ln1_b, w1, b1, w2, b2, ln2_g, ln2_b)
    rope_x = _rope_tables(t_x)
    rope_c = _identity_tables(t_c)
    ctx_row = bsz

    for l in range(DEPTH):
        with_ctx = l < DEPTH - 1
        glu_x, qk_x, vtm_x, so_x, gt_x, gates_x, q_x, k_x, vta_x = _in_projection(x, mod, l, None, wts, rope_x, 512)
        glu_c, qk_c, vtm_c, so_c, gt_c, gates_c, q_c, k_c, vta_c = _in_projection(
            ctx, mod, l, ctx_row, wts, rope_c, 256)

        hm_c, hm_x = _mlstm_branch((qk_c, vtm_c, gt_c, so_c), (qk_x, vtm_x, gt_x, so_x), l, wts)
        oa_x = _attention(q_x, [k_c, k_x], [vta_c, vta_x], 512)
        hc_x = _conv_branch(glu_x, l, wts)
        x_new = _merge_mlp(x, hc_x, hm_x, oa_x, gates_x, mod, l, None, wts, 512)
        if with_ctx:
            hc_c = _conv_branch(glu_c, l, wts)
            oa_c = _attention(q_c, [k_c], [vta_c], 256)
            ctx = _merge_mlp(ctx, hc_c, hm_c, oa_c, gates_c, mod, l, ctx_row, wts, 256)
        x = x_new
    return x
```

```python
import functools

import numpy as np
import jax
import jax.numpy as jnp
from jax import lax
from jax.experimental import pallas as pl
from jax.experimental.pallas import tpu as pltpu

F32 = jnp.float32
BF16 = jnp.bfloat16

D_MODEL = 1024
DEPTH = 2
GRID_W = 64
CONV_DIM = 512
CONV_K = 31
ML_HEADS = 4
ML_HEAD_DIM = 128
ML_DIM = ML_HEADS * ML_HEAD_DIM
ML_CHUNK = 256
MLA_HEADS = 8
QK_NOPE = 64
QK_ROPE = 32
V_HEAD = 64
Q_LORA = 768
KV_LORA = 256
ROPE_THETA = 10000.0
ROPE_FREQ = QK_ROPE // 4
MLA_SCALE = (QK_NOPE + QK_ROPE) ** -0.5
ML_SCALE = ML_HEAD_DIM ** -0.5
D_FF = 4 * D_MODEL
LN_EPS = 1e-5
ALPHA = (2 * DEPTH) ** 0.25
N_MOD = 6
MOD_SH1, MOD_SC1, MOD_G1, MOD_SH2, MOD_SC2, MOD_G2 = range(N_MOD)

LANES = 128
SUBLANES = 8
HEAD_PAD = LANES
ROPE_HALF = QK_ROPE // 2
ONES_ROWS = 16
ML_AUG = ML_HEAD_DIM + ONES_ROWS
ATT_AUG = V_HEAD + ONES_ROWS
Q_SCALE = MLA_SCALE * float(np.log2(np.e))
VMEM_LIMIT = 56 * 1024 * 1024

OFF_A = 0
OFF_QKV = OFF_A + 2 * CONV_DIM
OFF_O = OFF_QKV + 3 * ML_DIM
OFF_CQ = OFF_O + ML_DIM
OFF_CKV = OFF_CQ + Q_LORA
OFF_KR = OFF_CKV + KV_LORA
OFF_GIF = OFF_KR + HEAD_PAD
OFF_GATES = OFF_GIF + LANES
N_PACK = OFF_GATES + 3 * D_MODEL

_COLS = (2 * CONV_DIM, 3 * ML_DIM, ML_DIM, 4 * ML_HEADS, Q_LORA, KV_LORA, QK_ROPE, 3 * D_MODEL)
_SRC = tuple(int(s) for s in np.cumsum((0,) + _COLS))


def _layer_spec(l, shape):
    nd = len(shape)
    return pl.BlockSpec((None,) + tuple(shape), lambda *_: (l,) + (0,) * nd, pipeline_mode=pl.Buffered(1))


def _mod_spec(l, k, row=None):
    block = (None, None, None, 1, D_MODEL)
    if row is None:
        return pl.BlockSpec(block, lambda b, i: (l, b, k, 0, 0))
    return pl.BlockSpec(block, lambda b, i: (l, row, k, 0, 0))


def _sigmoid(z):
    return 1.0 / (1.0 + jnp.exp(-z))


def _log_sigmoid(z):
    return jnp.minimum(z, 0.0) - jnp.log1p(jnp.exp(-jnp.abs(z)))


def _layer_norm(z, g, b):
    mu = jnp.mean(z, axis=-1, keepdims=True)
    zc = z - mu
    var = jnp.mean(zc * zc, axis=-1, keepdims=True)
    return zc * lax.rsqrt(var + LN_EPS) * g + b


def _mod_kernel(c_ref, w_ref, b_ref, o_ref):
    c = c_ref[...]
    s = (c * _sigmoid(c)).astype(BF16)
    o_ref[0] = jnp.dot(s, w_ref[0].astype(BF16), preferred_element_type=F32) + b_ref[0]


def _modulation(cc, w_mod, b_mod):
    nl, d, n = w_mod.shape
    r = cc.shape[0]
    tn = 1024
    return pl.pallas_call(
        _mod_kernel,
        name="modulation",
        grid=(nl, n // tn),
        in_specs=[
            pl.BlockSpec((r, d), lambda l, j: (0, 0)),
            pl.BlockSpec((1, d, tn), lambda l, j: (l, 0, j)),
            pl.BlockSpec((1, 1, tn), lambda l, j: (l, 0, j)),
        ],
        out_specs=pl.BlockSpec((1, r, tn), lambda l, j: (l, 0, j)),
        out_shape=jax.ShapeDtypeStruct((nl, r, n), F32),
        compiler_params=pltpu.CompilerParams(
            dimension_semantics=("arbitrary", "arbitrary"), vmem_limit_bytes=VMEM_LIMIT),
    )(cc, w_mod, b_mod.reshape(nl, 1, n))


def _rope(z, tc, tsm, tsp):
    return z * tc + pltpu.roll(z, HEAD_PAD - ROPE_HALF, 1) * tsm + pltpu.roll(z, ROPE_HALF, 1) * tsp


def _inproj_kernel(x_ref, sc_ref, sh_ref, w_ref, b_ref, qng_ref, wuq_ref, kvg_ref, wkn_ref, wv_ref,
                   tc_ref, tsm_ref, tsp_ref, *out_refs, keys_only):
    if keys_only:
        qk_ref, vtm_ref, gt_ref, k_ref, vta_ref = out_refs
    else:
        glu_ref, qk_ref, vtm_ref, so_ref, gt_ref, gates_ref, q_ref, k_ref, vta_ref = out_refs
    u = (x_ref[0] * (1.0 + sc_ref[...]) + sh_ref[...]).astype(BF16)
    tm = u.shape[0]
    ones_rows = jnp.ones((ONES_ROWS, tm), BF16)

    def proj(off, n):
        return jnp.dot(u, w_ref[:, off:off + n], preferred_element_type=F32) + b_ref[:, off:off + n]

    if not keys_only:
        cq = proj(OFF_CQ, Q_LORA)
        nq = (cq * lax.rsqrt(jnp.mean(cq * cq, axis=-1, keepdims=True) + LN_EPS) * qng_ref[...]).astype(BF16)
    ckv = proj(OFF_CKV, KV_LORA)
    nkv = (ckv * lax.rsqrt(jnp.mean(ckv * ckv, axis=-1, keepdims=True) + LN_EPS) * kvg_ref[...]).astype(BF16)

    if not keys_only:
        a = proj(OFF_A, 2 * CONV_DIM)
        glu_ref[0] = (a[:, :CONV_DIM] * _sigmoid(a[:, CONV_DIM:])).astype(BF16)

    qk_ref[0, :, :ML_DIM] = proj(OFF_QKV, ML_DIM).astype(BF16)
    qk_ref[0, :, ML_DIM:] = (proj(OFF_QKV + ML_DIM, ML_DIM) * ML_SCALE).astype(BF16)
    vt = proj(OFF_QKV + 2 * ML_DIM, ML_DIM).T.astype(BF16)
    gt = proj(OFF_GIF, LANES).T
    for c in range(tm // ML_CHUNK):
        cs = slice(c * ML_CHUNK, (c + 1) * ML_CHUNK)
        gt_ref[0, c] = gt[:4 * ML_HEADS, cs]
        for h in range(ML_HEADS):
            vtm_ref[0, c, h * ML_AUG:h * ML_AUG + ML_HEAD_DIM, :] = vt[h * ML_HEAD_DIM:(h + 1) * ML_HEAD_DIM, cs]
            vtm_ref[0, c, h * ML_AUG + ML_HEAD_DIM:(h + 1) * ML_AUG, :] = ones_rows[:, cs]
    tc, tsm, tsp = tc_ref[...], tsm_ref[...], tsp_ref[...]
    if not keys_only:
        so_ref[0] = _sigmoid(proj(OFF_O, ML_DIM)).astype(BF16)
        for j in range(3):
            gates_ref[0, :, j * D_MODEL:(j + 1) * D_MODEL] = _sigmoid(
                proj(OFF_GATES + j * D_MODEL, D_MODEL)).astype(BF16)
        q = jnp.dot(nq, wuq_ref[...], preferred_element_type=F32)
        for h in range(MLA_HEADS):
            sl = slice(h * HEAD_PAD, (h + 1) * HEAD_PAD)
            q_ref[0, :, sl] = (_rope(q[:, sl], tc, tsm, tsp) * Q_SCALE).astype(BF16)

    kn = jnp.dot(nkv, wkn_ref[...], preferred_element_type=F32)
    kr = _rope(proj(OFF_KR, HEAD_PAD), tc, tsm, tsp)
    for h in range(MLA_HEADS):
        sl = slice(h * HEAD_PAD, (h + 1) * HEAD_PAD)
        k_ref[0, :, sl] = (kn[:, sl] + kr).astype(BF16)
    vta = lax.dot_general(wv_ref[...], nkv, (((1,), (1,)), ((), ())),
                          preferred_element_type=F32).astype(BF16)
    for h in range(MLA_HEADS):
        vta_ref[0, h * ATT_AUG:h * ATT_AUG + V_HEAD, :] = vta[h * V_HEAD:(h + 1) * V_HEAD, :]
        vta_ref[0, h * ATT_AUG + V_HEAD:(h + 1) * ATT_AUG, :] = ones_rows


def _in_projection(x, mod, l, mod_row, wts, tables, tm, keys_only=False):
    bsz, t, d = x.shape
    tok = lambda n: pl.BlockSpec((1, tm, n), lambda b, i: (b, i, 0))
    tab = pl.BlockSpec((tm, HEAD_PAD), lambda b, i: (i, 0))
    hq = MLA_HEADS * HEAD_PAD
    nc, cpt = t // ML_CHUNK, tm // ML_CHUNK
    seq = lambda n: (jax.ShapeDtypeStruct((bsz, t, n), BF16), tok(n))
    outputs = {
        "glu": seq(CONV_DIM),
        "qk": seq(2 * ML_DIM),
        "vtm": (jax.ShapeDtypeStruct((bsz, nc, ML_HEADS * ML_AUG, ML_CHUNK), BF16),
                pl.BlockSpec((1, cpt, ML_HEADS * ML_AUG, ML_CHUNK), lambda b, i: (b, i, 0, 0))),
        "so": seq(ML_DIM),
        "gt": (jax.ShapeDtypeStruct((bsz, nc, 4 * ML_HEADS, ML_CHUNK), F32),
               pl.BlockSpec((1, cpt, 4 * ML_HEADS, ML_CHUNK), lambda b, i: (b, i, 0, 0))),
        "gates": seq(3 * D_MODEL),
        "q": seq(hq),
        "k": seq(hq),
        "vta": (jax.ShapeDtypeStruct((bsz, MLA_HEADS * ATT_AUG, t), BF16),
                pl.BlockSpec((1, MLA_HEADS * ATT_AUG, tm), lambda b, i: (b, 0, i))),
    }
    if keys_only:
        outputs = {n: outputs[n] for n in ("qk", "vtm", "gt", "k", "vta")}
    out_shape = [v[0] for v in outputs.values()]
    out_specs = [v[1] for v in outputs.values()]
    res = pl.pallas_call(
        functools.partial(_inproj_kernel, keys_only=keys_only),
        name="in_projection",
        grid=(bsz, t // tm),
        in_specs=[tok(d), _mod_spec(l, MOD_SC1, mod_row), _mod_spec(l, MOD_SH1, mod_row),
                  _layer_spec(l, (d, N_PACK)), _layer_spec(l, (1, N_PACK)),
                  _layer_spec(l, (1, Q_LORA)), _layer_spec(l, (Q_LORA, hq)),
                  _layer_spec(l, (1, KV_LORA)), _layer_spec(l, (KV_LORA, hq)),
                  _layer_spec(l, (MLA_HEADS * V_HEAD, KV_LORA)),
                  tab, tab, tab],
        out_specs=out_specs,
        out_shape=out_shape,
        compiler_params=pltpu.CompilerParams(
            dimension_semantics=("parallel", "arbitrary"), vmem_limit_bytes=VMEM_LIMIT),
    )(x, mod, mod, wts["w_pack"], wts["b_pack"], wts["qn_g"], wts["w_uq"], wts["kvn_g"], wts["w_kn"], wts["w_v"],
      *tables)
    return dict(zip(outputs, res))


CONV_HALO = 16
CONV_TILE = 128


def _conv_kernel(h_ref, wdw_ref, bdw_ref, g_ref, b_ref, o_ref, hp, stage, cv, shifted, *, t):
    nt = t // CONV_TILE
    ncg = CONV_DIM // LANES
    first = CONV_HALO - CONV_K // 2
    span = CONV_TILE + (CONV_K // SUBLANES) * SUBLANES
    for cg in range(ncg):
        hp[cg, 0:CONV_HALO, :] = jnp.zeros((CONV_HALO, LANES), F32)
        hp[cg, t + CONV_HALO:t + 2 * CONV_HALO, :] = jnp.zeros((CONV_HALO, LANES), F32)

    def fill(i, carry):
        r0 = pl.multiple_of(i * CONV_TILE, CONV_TILE)
        hx = h_ref[0, pl.ds(r0, CONV_TILE), :].astype(F32)
        for cg in range(ncg):
            hp[cg, pl.ds(r0 + CONV_HALO, CONV_TILE), :] = hx[:, cg * LANES:(cg + 1) * LANES]
        return carry

    lax.fori_loop(0, nt, fill, 0)

    def tile(i, carry):
        r0 = pl.multiple_of(i * CONV_TILE, CONV_TILE)

        def group(cg, inner):
            stage[...] = hp[cg, pl.ds(r0, CONV_TILE + 2 * CONV_HALO), :]
            acc = jnp.zeros((CONV_TILE, LANES), F32)
            for r in range(SUBLANES):
                shifted[r] = stage[r:r + span, :]
                for k in range(CONV_K):
                    if (first + k) % SUBLANES == r:
                        a = first + k - r
                        acc = acc + shifted[r, a:a + CONV_TILE, :] * wdw_ref[cg, k:k + 1, :]
            cv[cg] = acc + bdw_ref[cg]
            return inner

        lax.fori_loop(0, ncg, group, 0)
        z = jnp.concatenate([cv[cg] for cg in range(ncg)], axis=1)
        z = _layer_norm(z, g_ref[...], b_ref[...])
        o_ref[0, pl.ds(r0, CONV_TILE), :] = (z * _sigmoid(z)).astype(BF16)
        return carry

    lax.fori_loop(0, nt, tile, 0)


def _conv_branch(glu, l, wts):
    bsz, t, _ = glu.shape
    ncg = CONV_DIM // LANES
    return pl.pallas_call(
        functools.partial(_conv_kernel, t=t),
        name="conv",
        grid=(bsz,),
        in_specs=[pl.BlockSpec((1, t, CONV_DIM), lambda b: (b, 0, 0)),
                  _layer_spec(l, (ncg, CONV_K + 1, LANES)), _layer_spec(l, (ncg, 1, LANES)),
                  _layer_spec(l, (1, CONV_DIM)), _layer_spec(l, (1, CONV_DIM))],
        out_specs=pl.BlockSpec((1, t, CONV_DIM), lambda b: (b, 0, 0)),
        out_shape=jax.ShapeDtypeStruct((bsz, t, CONV_DIM), BF16),
        scratch_shapes=[pltpu.VMEM((ncg, t + 2 * CONV_HALO, LANES), F32),
                        pltpu.VMEM((CONV_TILE + 2 * CONV_HALO, LANES), F32),
                        pltpu.VMEM((ncg, CONV_TILE, LANES), F32),
                        pltpu.VMEM((SUBLANES, CONV_TILE + (CONV_K // SUBLANES) * SUBLANES, LANES), F32)],
        compiler_params=pltpu.CompilerParams(
            dimension_semantics=("parallel",), vmem_limit_bytes=VMEM_LIMIT),
    )(glu, wts["w_dw"], wts["b_dw"], wts["cn_g"], wts["cn_b"])


N_CHAIN = 2 * ML_HEADS


def _mlstm_kernel(*refs, nc_c, nc_x, ctx_out):
    refs = list(refs)
    qk_c_ref, vt_c_ref, gt_c_ref = refs[:3]
    so_c_ref = refs[3] if ctx_out else None
    qk_x_ref, vt_x_ref, gt_x_ref, so_x_ref, mlg_ref = refs[3 + ctx_out:8 + ctx_out]
    outs = refs[8 + ctx_out:9 + 2 * ctx_out]
    hn_c_ref, hn_x_ref = (outs[0] if ctx_out else None), outs[-1]
    hf_c, hb_c, hf_x, hb_x, c_scr, m_scr = refs[9 + 2 * ctx_out:]
    L = ML_CHUNK
    dh = ML_HEAD_DIM
    row = lax.broadcasted_iota(jnp.int32, (L, L), 0)
    col = lax.broadcasted_iota(jnp.int32, (L, L), 1)
    upper = row <= col
    lower = row >= col
    tri2 = jnp.concatenate([upper.astype(F32), lower.astype(F32)], axis=1)
    is_fwd = lax.broadcasted_iota(jnp.int32, (N_CHAIN, L), 0) < ML_HEADS

    c_scr[...] = jnp.zeros(c_scr.shape, F32)
    m_scr[...] = jnp.zeros(m_scr.shape, F32)

    def iteration(qk_ref, vt_ref, gt_ref, hf, hb, cf, cb):
        gf = gt_ref[0, cf]
        gb = gt_ref[0, cb]
        li = jnp.where(is_fwd, gf[:N_CHAIN], gb[:N_CHAIN])
        lf = _log_sigmoid(jnp.where(is_fwd, gf[N_CHAIN:], gb[N_CHAIN:]))
        cum2 = jnp.dot(lf, tri2, preferred_element_type=F32, precision=lax.Precision.HIGHEST)
        bc = jnp.where(is_fwd, cum2[:, :L], cum2[:, L:])
        r = li - bc
        b_last = jnp.sum(lf, axis=1, keepdims=True)
        m = m_scr[:, 0:1]
        w_src = b_last + r
        m_new = jnp.maximum(b_last + m, jnp.max(w_src, axis=1, keepdims=True))
        a_state = jnp.exp(b_last + m - m_new)
        w = jnp.exp(w_src - m_new)
        inter = bc + m
        r_cols = jnp.concatenate([r, jnp.zeros((L - N_CHAIN, L), F32)], axis=0).T
        m_scr[...] = jnp.broadcast_to(m_new, m_scr.shape)
        early = []
        for j in range(N_CHAIN):
            d, h = divmod(j, ML_HEADS)
            cc = cf if d == 0 else cb
            r0 = pl.multiple_of(cc * L, L)
            q = qk_ref[0, pl.ds(r0, L), h * dh:(h + 1) * dh]
            k = qk_ref[0, pl.ds(r0, L), ML_DIM + h * dh:ML_DIM + (h + 1) * dh]
            vt = vt_ref[0, cc, h * ML_AUG:(h + 1) * ML_AUG, :]
            ct = c_scr[j]
            both = lax.dot_general(jnp.concatenate([k, ct.astype(BF16)], axis=0), q, (((1,), (1,)), ((), ())),
                                   preferred_element_type=F32)
            wv = (vt.astype(F32) * w[j:j + 1, :]).astype(BF16)
            c_scr[j] = a_state[j:j + 1, :] * ct + jnp.dot(wv, k, preferred_element_type=F32)
            early.append((both, vt))
        for j in range(N_CHAIN):
            d, h = divmod(j, ML_HEADS)
            cc = cf if d == 0 else cb
            mask = upper if d == 0 else lower
            both, vt = early[j]
            log_d = jnp.where(mask, bc[j:j + 1, :] + r_cols[:, j:j + 1], -jnp.inf)
            m_row = jnp.maximum(inter[j:j + 1, :], jnp.max(log_d, axis=0, keepdims=True))
            a_inter = jnp.exp(inter[j:j + 1, :] - m_row)
            dmat = jnp.exp(log_d - m_row)
            pt = (both[:L] * dmat).astype(BF16)
            num_aug = a_inter * both[L:] + jnp.dot(vt, pt, preferred_element_type=F32)
            den = num_aug[dh:dh + 1, :]
            hs = hf if d == 0 else hb
            hs[cc, h * dh:(h + 1) * dh, :] = num_aug[:dh] / jnp.maximum(jnp.abs(den), jnp.exp(-m_row))

    def scan(qk_ref, vt_ref, gt_ref, hf, hb, nc):
        def body(i, carry):
            iteration(qk_ref, vt_ref, gt_ref, hf, hb, i, nc - 1 - i)
            return carry

        lax.fori_loop(0, nc, body, 0, unroll=min(2, nc))

    scan(qk_c_ref, vt_c_ref, gt_c_ref, hf_c, hb_c, nc_c)
    scan(qk_x_ref, vt_x_ref, gt_x_ref, hf_x, hb_x, nc_x)

    def finish(hf, hb, so_ref, out_ref, nc):
        def body(c, carry):
            r0 = pl.multiple_of(c * L, L)
            hsum = hf[c] + hb[c]
            for h in range(ML_HEADS):
                sl = slice(h * dh, (h + 1) * dh)
                z = hsum[sl, :]
                mu = jnp.mean(z, axis=0, keepdims=True)
                zc = z - mu
                var = jnp.mean(zc * zc, axis=0, keepdims=True)
                hn = (zc * lax.rsqrt(var + LN_EPS)).T * mlg_ref[:, sl]
                out_ref[0, pl.ds(r0, L), sl] = (so_ref[0, pl.ds(r0, L), sl].astype(F32) * hn).astype(BF16)
            return carry

        lax.fori_loop(0, nc, body, 0)

    if ctx_out:
        finish(hf_c, hb_c, so_c_ref, hn_c_ref, nc_c)
    finish(hf_x, hb_x, so_x_ref, hn_x_ref, nc_x)


def _mlstm_branch(pc, px, l, wts, ctx_out):
    bsz, t_c, _ = pc["qk"].shape
    t_x = px["qk"].shape[1]
    nc_c, nc_x = t_c // ML_CHUNK, t_x // ML_CHUNK
    seq = lambda t, n: pl.BlockSpec((1, t, n), lambda b: (b, 0, 0))
    chunked = lambda nc, n: pl.BlockSpec((1, nc, n, ML_CHUNK), lambda b: (b, 0, 0, 0))
    ins = lambda t, nc, so: [seq(t, 2 * ML_DIM), chunked(nc, ML_HEADS * ML_AUG), chunked(nc, 4 * ML_HEADS)] + (
        [seq(t, ML_DIM)] if so else [])
    args = lambda p, so: [p["qk"], p["vtm"], p["gt"]] + ([p["so"]] if so else [])
    hbuf = lambda nc: pltpu.VMEM((nc, ML_DIM, ML_CHUNK), F32)
    out_t = ([t_c] if ctx_out else []) + [t_x]
    res = pl.pallas_call(
        functools.partial(_mlstm_kernel, nc_c=nc_c, nc_x=nc_x, ctx_out=ctx_out),
        name="mlstm",
        grid=(bsz,),
        in_specs=ins(t_c, nc_c, ctx_out) + ins(t_x, nc_x, True) + [_layer_spec(l, (1, ML_DIM))],
        out_specs=[seq(t, ML_DIM) for t in out_t],
        out_shape=[jax.ShapeDtypeStruct((bsz, t, ML_DIM), BF16) for t in out_t],
        scratch_shapes=[hbuf(nc_c), hbuf(nc_c), hbuf(nc_x), hbuf(nc_x),
                        pltpu.VMEM((N_CHAIN, ML_AUG, ML_HEAD_DIM), F32),
                        pltpu.VMEM((N_CHAIN, LANES), F32)],
        compiler_params=pltpu.CompilerParams(
            dimension_semantics=("parallel",), vmem_limit_bytes=VMEM_LIMIT),
    )(*args(pc, ctx_out), *args(px, True), wts["ml_g"])
    return (res[0], res[1]) if ctx_out else (None, res[0])


ATT_KEYS = 256
ATT_QG = 256
ATT_LOOKAHEAD = 4


def _attn_kernel(*refs, n_seg):
    q_ref = refs[0]
    k_refs = refs[1:1 + n_seg]
    vt_refs = refs[1 + n_seg:1 + 2 * n_seg]
    o_ref = refs[1 + 2 * n_seg]
    qg = min(ATT_QG, q_ref.shape[1])
    n_qg = q_ref.shape[1] // qg
    chunks = [(k_ref, vt_ref, j) for k_ref, vt_ref in zip(k_refs, vt_refs)
              for j in range(k_ref.shape[1] // ATT_KEYS)]
    steps = [(ci, h, g) for ci in range(len(chunks)) for h in range(MLA_HEADS) for g in range(n_qg)]

    def scores(ci, h, g):
        k_ref, _, j = chunks[ci]
        hs = slice(h * HEAD_PAD, (h + 1) * HEAD_PAD)
        return lax.dot_general(k_ref[0, j * ATT_KEYS:(j + 1) * ATT_KEYS, hs], q_ref[0, g * qg:(g + 1) * qg, hs],
                               (((1,), (1,)), ((), ())), preferred_element_type=F32)

    state = {}
    pending = {}
    for idx in range(len(steps) + ATT_LOOKAHEAD):
        if idx < len(steps):
            pending[idx] = scores(*steps[idx])
        if idx < ATT_LOOKAHEAD:
            continue
        ci, h, g = steps[idx - ATT_LOOKAHEAD]
        s = pending.pop(idx - ATT_LOOKAHEAD)
        _, vt_ref, j = chunks[ci]
        vtj = vt_ref[0, h * ATT_AUG:(h + 1) * ATT_AUG, j * ATT_KEYS:(j + 1) * ATT_KEYS]
        cm = jnp.max(s, axis=0, keepdims=True)
        if ci == 0:
            m_new = cm
            acc = jnp.dot(vtj, jnp.exp2(s - m_new).astype(BF16), preferred_element_type=F32)
        else:
            m_old, acc_old = state[h, g]
            m_new = jnp.maximum(m_old, cm)
            acc = acc_old * jnp.exp2(m_old - m_new) + jnp.dot(
                vtj, jnp.exp2(s - m_new).astype(BF16), preferred_element_type=F32)
        state[h, g] = (m_new, acc)
    outs = []
    for h in range(MLA_HEADS):
        accs = [state[h, g][1] for g in range(n_qg)]
        outs.append(jnp.concatenate([a[:V_HEAD] * (1.0 / a[V_HEAD:V_HEAD + 1]) for a in accs], axis=1))
    o_ref[0] = jnp.concatenate(outs, axis=0).T.astype(BF16)


def _attention(q, ks, vts, tq):
    bsz, t, hq = q.shape
    n_seg = len(ks)
    hv = MLA_HEADS * V_HEAD
    kspec = [pl.BlockSpec((1, k.shape[1], hq), lambda b, i: (b, 0, 0)) for k in ks]
    vspec = [pl.BlockSpec((1, MLA_HEADS * ATT_AUG, vt.shape[2]), lambda b, i: (b, 0, 0)) for vt in vts]
    return pl.pallas_call(
        functools.partial(_attn_kernel, n_seg=n_seg),
        name="attention",
        grid=(bsz, t // tq),
        in_specs=[pl.BlockSpec((1, tq, hq), lambda b, i: (b, i, 0))] + kspec + vspec,
        out_specs=pl.BlockSpec((1, tq, hv), lambda b, i: (b, i, 0)),
        out_shape=jax.ShapeDtypeStruct((bsz, t, hv), BF16),
        compiler_params=pltpu.CompilerParams(
            dimension_semantics=("parallel", "arbitrary"), vmem_limit_bytes=VMEM_LIMIT),
    )(q, *ks, *vts)


FF_CHUNK = 1024
MERGE_GROUPS = 2


def _merge_mlp_kernel(x_ref, hc_ref, hm_ref, oa_ref, gates_ref, g1_ref, sc2_ref, sh2_ref, g2_ref,
                      wc_ref, wm_ref, wa_ref, wo_ref, bo_ref, ln1g_ref, ln1b_ref,
                      w1_ref, b1_ref, w2_ref, b2_ref, ln2g_ref, ln2b_ref, o_ref):
    d = D_MODEL
    tm = x_ref.shape[1]
    rows = [slice(g * tm // MERGE_GROUPS, (g + 1) * tm // MERGE_GROUPS) for g in range(MERGE_GROUPS)]
    dot = functools.partial(jnp.dot, preferred_element_type=F32)
    yc = [dot(hc_ref[0, r, :], wc_ref[...]) for r in rows]
    ym = [dot(hm_ref[0, r, :], wm_ref[...]) for r in rows]
    ya = [dot(oa_ref[0, r, :], wa_ref[...]) for r in rows]
    mix = [(gates_ref[0, r, 0:d].astype(F32) * c + gates_ref[0, r, d:2 * d].astype(F32) * m
            + gates_ref[0, r, 2 * d:3 * d].astype(F32) * a).astype(BF16) for r, c, m, a in zip(rows, yc, ym, ya)]
    y = [dot(mx, wo_ref[...]) + bo_ref[...] for mx in mix]
    x1 = [_layer_norm(ALPHA * x_ref[0, r, :] + g1_ref[...] * yy, ln1g_ref[...], ln1b_ref[...])
          for r, yy in zip(rows, y)]
    u2 = [(xx * (1.0 + sc2_ref[...]) + sh2_ref[...]).astype(BF16) for xx in x1]
    acc = [jnp.zeros(xx.shape, F32) for xx in x1]
    for j in range(D_FF // FF_CHUNK):
        sl = slice(j * FF_CHUNK, (j + 1) * FF_CHUNK)
        hdn = [jnp.maximum(dot(u, w1_ref[:, sl]) + b1_ref[:, sl], 0.0) for u in u2]
        acc = [ac + dot((hd * hd).astype(BF16), w2_ref[sl, :]) for ac, hd in zip(acc, hdn)]
    for r, xx, ac in zip(rows, x1, acc):
        o_ref[0, r, :] = _layer_norm(ALPHA * xx + g2_ref[...] * (ac + b2_ref[...]), ln2g_ref[...], ln2b_ref[...])


def _merge_mlp(x, hc, hm, oa, gates, mod, l, mod_row, wts, tm):
    bsz, t, d = x.shape
    tok = lambda n: pl.BlockSpec((1, tm, n), lambda b, i: (b, i, 0))
    mods = [_mod_spec(l, k, mod_row) for k in (MOD_G1, MOD_SC2, MOD_SH2, MOD_G2)]
    return pl.pallas_call(
        _merge_mlp_kernel,
        name="merge_mlp",
        grid=(bsz, t // tm),
        in_specs=[tok(d), tok(CONV_DIM), tok(ML_DIM), tok(MLA_HEADS * V_HEAD), tok(3 * d)] + mods + [
            _layer_spec(l, (CONV_DIM, d)), _layer_spec(l, (ML_DIM, d)), _layer_spec(l, (MLA_HEADS * V_HEAD, d)),
            _layer_spec(l, (d, d)), _layer_spec(l, (1, d)), _layer_spec(l, (1, d)), _layer_spec(l, (1, d)),
            _layer_spec(l, (d, D_FF)), _layer_spec(l, (1, D_FF)), _layer_spec(l, (D_FF, d)), _layer_spec(l, (1, d)),
            _layer_spec(l, (1, d)), _layer_spec(l, (1, d))],
        out_specs=tok(d),
        out_shape=jax.ShapeDtypeStruct((bsz, t, d), F32),
        compiler_params=pltpu.CompilerParams(
            dimension_semantics=("parallel", "arbitrary"), vmem_limit_bytes=VMEM_LIMIT),
    )(x, hc, hm, oa, gates, mod, mod, mod, mod,
      wts["w_conv_out"], wts["w_ml_out"], wts["w_mla_out"], wts["w_out"], wts["b_out"], wts["ln1_g"], wts["ln1_b"],
      wts["w1"], wts["b1"], wts["w2"], wts["b2"], wts["ln2_g"], wts["ln2_b"])


def _rope_runs():
    runs = []
    for half in range(2):
        for a in range(2):
            start = a * 2 * ROPE_FREQ + half * ROPE_FREQ
            runs.append(slice(start, start + ROPE_FREQ))
    return runs


def _pack_weights(w_in, b_in, w_dw, b_dw, cn_g, cn_b, w_conv_out, ml_g, w_ml_out, qn_g, w_uq, kvn_g, w_ukv,
                  w_mla_out, w_out, b_out, ln1_g, ln1_b, w1, b1, w2, b2, ln2_g, ln2_b):
    nl = w_in.shape[0]

    def packed(z, dtype):
        a, qkv, o, g, cq, ckv, kr, gates = (z[..., _SRC[i]:_SRC[i + 1]].astype(dtype) for i in range(8))
        zeros = lambda n: jnp.zeros(z.shape[:-1] + (n,), dtype)
        kr_block = [zeros(QK_NOPE)] + [kr[..., s] for s in _rope_runs()] + [zeros(HEAD_PAD - QK_NOPE - QK_ROPE)]
        g_block = [g[..., i * ML_HEADS:(i + 1) * ML_HEADS] for i in (0, 2, 1, 3)] + [zeros(LANES - 4 * ML_HEADS)]
        return jnp.concatenate([a, qkv, o, cq, ckv] + kr_block + g_block + [gates], axis=-1)

    wq = w_uq.reshape(nl, Q_LORA, MLA_HEADS, QK_NOPE + QK_ROPE)
    wq_p = jnp.concatenate([wq[..., :QK_NOPE]] + [wq[..., QK_NOPE:][..., s] for s in _rope_runs()]
                           + [jnp.zeros(wq.shape[:-1] + (HEAD_PAD - QK_NOPE - QK_ROPE,), F32)], axis=-1)
    wkv = w_ukv.reshape(nl, KV_LORA, MLA_HEADS, QK_NOPE + V_HEAD)
    wkn_p = jnp.concatenate([wkv[..., :QK_NOPE], jnp.zeros(wkv.shape[:-1] + (HEAD_PAD - QK_NOPE,), F32)], axis=-1)
    wv_t = wkv[..., QK_NOPE:].reshape(nl, KV_LORA, MLA_HEADS * V_HEAD).transpose(0, 2, 1)
    ncg = CONV_DIM // LANES
    row = lambda z: z[:, None, :]
    return {
        "w_pack": packed(w_in, BF16), "b_pack": row(packed(b_in, F32)),
        "qn_g": row(qn_g), "w_uq": wq_p.reshape(nl, Q_LORA, MLA_HEADS * HEAD_PAD).astype(BF16),
        "kvn_g": row(kvn_g), "w_kn": wkn_p.reshape(nl, KV_LORA, MLA_HEADS * HEAD_PAD).astype(BF16),
        "w_v": wv_t.astype(BF16),
        "w_dw": jnp.pad(w_dw, ((0, 0), (0, 1), (0, 0))).reshape(nl, CONV_K + 1, ncg, LANES).transpose(0, 2, 1, 3),
        "b_dw": b_dw.reshape(nl, ncg, 1, LANES),
        "cn_g": row(cn_g), "cn_b": row(cn_b), "ml_g": row(ml_g),
        "w_conv_out": w_conv_out.astype(BF16), "w_ml_out": w_ml_out.astype(BF16),
        "w_mla_out": w_mla_out.astype(BF16), "w_out": w_out.astype(BF16), "b_out": row(b_out),
        "ln1_g": row(ln1_g), "ln1_b": row(ln1_b), "w1": w1.astype(BF16), "b1": row(b1),
        "w2": w2.astype(BF16), "b2": row(b2), "ln2_g": row(ln2_g), "ln2_b": row(ln2_b),
    }


def _rope_tables(n_tokens):
    rows = n_tokens // GRID_W
    rr, cc = np.meshgrid(np.arange(rows, dtype=np.float32), np.arange(GRID_W, dtype=np.float32), indexing="ij")
    inv = (np.float32(ROPE_THETA) ** (-np.arange(ROPE_FREQ, dtype=np.float32) / np.float32(ROPE_FREQ))).astype(np.float32)
    ang = np.stack([rr.reshape(-1), cc.reshape(-1)], -1)[..., None] * inv
    cos = np.cos(ang).astype(np.float32).reshape(n_tokens, ROPE_HALF)
    sin = np.sin(ang).astype(np.float32).reshape(n_tokens, ROPE_HALF)
    one = np.ones((n_tokens, QK_NOPE), np.float32)
    z_nope = np.zeros((n_tokens, QK_NOPE), np.float32)
    z_half = np.zeros((n_tokens, ROPE_HALF), np.float32)
    z_pad = np.zeros((n_tokens, HEAD_PAD - QK_NOPE - QK_ROPE), np.float32)
    tc = np.concatenate([one, cos, cos, z_pad], axis=1)
    tsm = np.concatenate([z_nope, -sin, z_half, z_pad], axis=1)
    tsp = np.concatenate([z_nope, z_half, sin, z_pad], axis=1)
    return tc, tsm, tsp


def _identity_tables(n_tokens):
    return (np.ones((n_tokens, HEAD_PAD), np.float32), np.zeros((n_tokens, HEAD_PAD), np.float32),
            np.zeros((n_tokens, HEAD_PAD), np.float32))


def kernel(x, c, ctx, c_ctx, w_mod, b_mod, w_in, b_in, w_dw, b_dw, conv_norm_g, conv_norm_b, w_conv_out,
           mlstm_norm_g, w_mlstm_out, q_norm_g, w_uq, kv_norm_g, w_ukv, w_mla_out, w_out, b_out,
           ln1_g, ln1_b, w1, b1, w2, b2, ln2_g, ln2_b):
    bsz, t_x, d = x.shape
    t_c = ctx.shape[1]
    n_rows = 24
    cc = jnp.concatenate([c, c_ctx[None, :], jnp.zeros((n_rows - bsz - 1, d), F32)], axis=0)
    mod = _modulation(cc, w_mod, b_mod).reshape(DEPTH, n_rows, N_MOD, 1, d)
    wts = _pack_weights(w_in, b_in, w_dw, b_dw, conv_norm_g, conv_norm_b, w_conv_out, mlstm_norm_g,
                        w_mlstm_out, q_norm_g, w_uq, kv_norm_g, w_ukv, w_mla_out, w_out, b_out,
                        ln1_g, ln1_b, w1, b1, w2, b2, ln2_g, ln2_b)
    rope_x = _rope_tables(t_x)
    rope_c = _identity_tables(t_c)
    ctx_row = bsz

    for l in range(DEPTH):
        with_ctx = l < DEPTH - 1
        px = _in_projection(x, mod, l, None, wts, rope_x, 512)
        pc = _in_projection(ctx, mod, l, ctx_row, wts, rope_c, 256, keys_only=not with_ctx)

        hm_c, hm_x = _mlstm_branch(pc, px, l, wts, with_ctx)
        oa_x = _attention(px["q"], [pc["k"], px["k"]], [pc["vta"], px["vta"]], 512)
        hc_x = _conv_branch(px["glu"], l, wts)
        x_new = _merge_mlp(x, hc_x, hm_x, oa_x, px["gates"], mod, l, None, wts, 512)
        if with_ctx:
            hc_c = _conv_branch(pc["glu"], l, wts)
            oa_c = _attention(pc["q"], [pc["k"]], [pc["vta"]], 256)
            ctx = _merge_mlp(ctx, hc_c, hm_c, oa_c, pc["gates"], mod, l, ctx_row, wts, 256)
        x = x_new
    return x
```

```python
import functools

import numpy as np
import jax
import jax.numpy as jnp
from jax import lax
from jax.experimental import pallas as pl
from jax.experimental.pallas import tpu as pltpu

F32 = jnp.float32
BF16 = jnp.bfloat16

D_MODEL = 1024
DEPTH = 2
GRID_W = 64
CONV_DIM = 512
CONV_K = 31
ML_HEADS = 4
ML_HEAD_DIM = 128
ML_DIM = ML_HEADS * ML_HEAD_DIM
ML_CHUNK = 256
MLA_HEADS = 8
QK_NOPE = 64
QK_ROPE = 32
V_HEAD = 64
Q_LORA = 768
KV_LORA = 256
ROPE_THETA = 10000.0
ROPE_FREQ = QK_ROPE // 4
MLA_SCALE = (QK_NOPE + QK_ROPE) ** -0.5
ML_SCALE = ML_HEAD_DIM ** -0.5
D_FF = 4 * D_MODEL
LN_EPS = 1e-5
ALPHA = (2 * DEPTH) ** 0.25
N_MOD = 6
MOD_SH1, MOD_SC1, MOD_G1, MOD_SH2, MOD_SC2, MOD_G2 = range(N_MOD)

LANES = 128
SUBLANES = 8
HEAD_PAD = LANES
ROPE_HALF = QK_ROPE // 2
ONES_ROWS = 16
ML_AUG = ML_HEAD_DIM + ONES_ROWS
ATT_AUG = V_HEAD + ONES_ROWS
Q_SCALE = MLA_SCALE * float(np.log2(np.e))
VMEM_LIMIT = 56 * 1024 * 1024

OFF_A = 0
OFF_QKV = OFF_A + 2 * CONV_DIM
OFF_O = OFF_QKV + 3 * ML_DIM
OFF_CQ = OFF_O + ML_DIM
OFF_CKV = OFF_CQ + Q_LORA
OFF_KR = OFF_CKV + KV_LORA
OFF_GIF = OFF_KR + HEAD_PAD
OFF_GATES = OFF_GIF + LANES
N_PACK = OFF_GATES + 3 * D_MODEL

_COLS = (2 * CONV_DIM, 3 * ML_DIM, ML_DIM, 4 * ML_HEADS, Q_LORA, KV_LORA, QK_ROPE, 3 * D_MODEL)
_SRC = tuple(int(s) for s in np.cumsum((0,) + _COLS))


def _layer_spec(l, shape):
    nd = len(shape)
    return pl.BlockSpec((None,) + tuple(shape), lambda *_: (l,) + (0,) * nd, pipeline_mode=pl.Buffered(1))


def _mod_spec(l, k, row=None):
    block = (None, None, None, 1, D_MODEL)
    if row is None:
        return pl.BlockSpec(block, lambda b, i: (l, b, k, 0, 0))
    return pl.BlockSpec(block, lambda b, i: (l, row, k, 0, 0))


def _sigmoid(z):
    return 1.0 / (1.0 + jnp.exp(-z))


def _log_sigmoid(z):
    return jnp.minimum(z, 0.0) - jnp.log1p(jnp.exp(-jnp.abs(z)))


def _layer_norm(z, g, b):
    mu = jnp.mean(z, axis=-1, keepdims=True)
    zc = z - mu
    var = jnp.mean(zc * zc, axis=-1, keepdims=True)
    return zc * lax.rsqrt(var + LN_EPS) * g + b


def _mod_kernel(c_ref, w_ref, b_ref, o_ref):
    c = c_ref[...]
    s = (c * _sigmoid(c)).astype(BF16)
    o_ref[0] = jnp.dot(s, w_ref[0].astype(BF16), preferred_element_type=F32) + b_ref[0]


def _modulation(cc, w_mod, b_mod):
    nl, d, n = w_mod.shape
    r = cc.shape[0]
    tn = 1024
    return pl.pallas_call(
        _mod_kernel,
        name="modulation",
        grid=(nl, n // tn),
        in_specs=[
            pl.BlockSpec((r, d), lambda l, j: (0, 0)),
            pl.BlockSpec((1, d, tn), lambda l, j: (l, 0, j)),
            pl.BlockSpec((1, 1, tn), lambda l, j: (l, 0, j)),
        ],
        out_specs=pl.BlockSpec((1, r, tn), lambda l, j: (l, 0, j)),
        out_shape=jax.ShapeDtypeStruct((nl, r, n), F32),
        compiler_params=pltpu.CompilerParams(
            dimension_semantics=("arbitrary", "arbitrary"), vmem_limit_bytes=VMEM_LIMIT),
    )(cc, w_mod, b_mod.reshape(nl, 1, n))


def _rope(z, tc, tsm, tsp):
    return z * tc + pltpu.roll(z, HEAD_PAD - ROPE_HALF, 1) * tsm + pltpu.roll(z, ROPE_HALF, 1) * tsp


def _inproj_kernel(x_ref, sc_ref, sh_ref, w_ref, b_ref, qng_ref, wuq_ref, kvg_ref, wkn_ref, wv_ref,
                   tc_ref, tsm_ref, tsp_ref, *out_refs, keys_only):
    if keys_only:
        qk_ref, vtm_ref, gt_ref, k_ref, vta_ref = out_refs
    else:
        glu_ref, qk_ref, vtm_ref, so_ref, gt_ref, gates_ref, q_ref, k_ref, vta_ref = out_refs
    u = (x_ref[0] * (1.0 + sc_ref[...]) + sh_ref[...]).astype(BF16)
    tm = u.shape[0]
    ones_rows = jnp.ones((ONES_ROWS, tm), BF16)

    def proj(off, n):
        return jnp.dot(u, w_ref[:, off:off + n], preferred_element_type=F32) + b_ref[:, off:off + n]

    if not keys_only:
        cq = proj(OFF_CQ, Q_LORA)
        nq = (cq * lax.rsqrt(jnp.mean(cq * cq, axis=-1, keepdims=True) + LN_EPS) * qng_ref[...]).astype(BF16)
    ckv = proj(OFF_CKV, KV_LORA)
    nkv = (ckv * lax.rsqrt(jnp.mean(ckv * ckv, axis=-1, keepdims=True) + LN_EPS) * kvg_ref[...]).astype(BF16)

    if not keys_only:
        a = proj(OFF_A, 2 * CONV_DIM)
        glu_ref[0] = (a[:, :CONV_DIM] * _sigmoid(a[:, CONV_DIM:])).astype(BF16)

    qk_ref[0, :, :ML_DIM] = proj(OFF_QKV, ML_DIM).astype(BF16)
    qk_ref[0, :, ML_DIM:] = (proj(OFF_QKV + ML_DIM, ML_DIM) * ML_SCALE).astype(BF16)
    vt = proj(OFF_QKV + 2 * ML_DIM, ML_DIM).T.astype(BF16)
    gt = proj(OFF_GIF, LANES).T
    for c in range(tm // ML_CHUNK):
        cs = slice(c * ML_CHUNK, (c + 1) * ML_CHUNK)
        gt_ref[0, c] = gt[:4 * ML_HEADS, cs]
        for h in range(ML_HEADS):
            vtm_ref[0, c, h * ML_AUG:h * ML_AUG + ML_HEAD_DIM, :] = vt[h * ML_HEAD_DIM:(h + 1) * ML_HEAD_DIM, cs]
            vtm_ref[0, c, h * ML_AUG + ML_HEAD_DIM:(h + 1) * ML_AUG, :] = ones_rows[:, cs]
    tc, tsm, tsp = tc_ref[...], tsm_ref[...], tsp_ref[...]
    if not keys_only:
        so_ref[0] = _sigmoid(proj(OFF_O, ML_DIM)).astype(BF16)
        for j in range(3):
            gates_ref[0, :, j * D_MODEL:(j + 1) * D_MODEL] = _sigmoid(
                proj(OFF_GATES + j * D_MODEL, D_MODEL)).astype(BF16)
        q = jnp.dot(nq, wuq_ref[...], preferred_element_type=F32)
        for h in range(MLA_HEADS):
            sl = slice(h * HEAD_PAD, (h + 1) * HEAD_PAD)
            q_ref[0, :, sl] = (_rope(q[:, sl], tc, tsm, tsp) * Q_SCALE).astype(BF16)

    kn = jnp.dot(nkv, wkn_ref[...], preferred_element_type=F32)
    kr = _rope(proj(OFF_KR, HEAD_PAD), tc, tsm, tsp)
    for h in range(MLA_HEADS):
        sl = slice(h * HEAD_PAD, (h + 1) * HEAD_PAD)
        k_ref[0, :, sl] = (kn[:, sl] + kr).astype(BF16)
    vta = lax.dot_general(wv_ref[...], nkv, (((1,), (1,)), ((), ())),
                          preferred_element_type=F32).astype(BF16)
    for h in range(MLA_HEADS):
        vta_ref[0, h * ATT_AUG:h * ATT_AUG + V_HEAD, :] = vta[h * V_HEAD:(h + 1) * V_HEAD, :]
        vta_ref[0, h * ATT_AUG + V_HEAD:(h + 1) * ATT_AUG, :] = ones_rows


def _in_projection(x, mod, l, mod_row, wts, tables, tm, keys_only=False):
    bsz, t, d = x.shape
    tok = lambda n: pl.BlockSpec((1, tm, n), lambda b, i: (b, i, 0))
    tab = pl.BlockSpec((tm, HEAD_PAD), lambda b, i: (i, 0))
    hq = MLA_HEADS * HEAD_PAD
    nc, cpt = t // ML_CHUNK, tm // ML_CHUNK
    seq = lambda n: (jax.ShapeDtypeStruct((bsz, t, n), BF16), tok(n))
    outputs = {
        "glu": seq(CONV_DIM),
        "qk": seq(2 * ML_DIM),
        "vtm": (jax.ShapeDtypeStruct((bsz, nc, ML_HEADS * ML_AUG, ML_CHUNK), BF16),
                pl.BlockSpec((1, cpt, ML_HEADS * ML_AUG, ML_CHUNK), lambda b, i: (b, i, 0, 0))),
        "so": seq(ML_DIM),
        "gt": (jax.ShapeDtypeStruct((bsz, nc, 4 * ML_HEADS, ML_CHUNK), F32),
               pl.BlockSpec((1, cpt, 4 * ML_HEADS, ML_CHUNK), lambda b, i: (b, i, 0, 0))),
        "gates": seq(3 * D_MODEL),
        "q": seq(hq),
        "k": seq(hq),
        "vta": (jax.ShapeDtypeStruct((bsz, MLA_HEADS * ATT_AUG, t), BF16),
                pl.BlockSpec((1, MLA_HEADS * ATT_AUG, tm), lambda b, i: (b, 0, i))),
    }
    if keys_only:
        outputs = {n: outputs[n] for n in ("qk", "vtm", "gt", "k", "vta")}
    out_shape = [v[0] for v in outputs.values()]
    out_specs = [v[1] for v in outputs.values()]
    res = pl.pallas_call(
        functools.partial(_inproj_kernel, keys_only=keys_only),
        name="in_projection",
        grid=(bsz, t // tm),
        in_specs=[tok(d), _mod_spec(l, MOD_SC1, mod_row), _mod_spec(l, MOD_SH1, mod_row),
                  _layer_spec(l, (d, N_PACK)), _layer_spec(l, (1, N_PACK)),
                  _layer_spec(l, (1, Q_LORA)), _layer_spec(l, (Q_LORA, hq)),
                  _layer_spec(l, (1, KV_LORA)), _layer_spec(l, (KV_LORA, hq)),
                  _layer_spec(l, (MLA_HEADS * V_HEAD, KV_LORA)),
                  tab, tab, tab],
        out_specs=out_specs,
        out_shape=out_shape,
        compiler_params=pltpu.CompilerParams(
            dimension_semantics=("parallel", "arbitrary"), vmem_limit_bytes=VMEM_LIMIT),
    )(x, mod, mod, wts["w_pack"], wts["b_pack"], wts["qn_g"], wts["w_uq"], wts["kvn_g"], wts["w_kn"], wts["w_v"],
      *tables)
    return dict(zip(outputs, res))


CONV_HALO = 16
CONV_TILE = 128


def _conv_kernel(h_ref, wdw_ref, bdw_ref, g_ref, b_ref, o_ref, hp, stage, cv, shifted, *, t):
    nt = t // CONV_TILE
    ncg = CONV_DIM // LANES
    first = CONV_HALO - CONV_K // 2
    span = CONV_TILE + (CONV_K // SUBLANES) * SUBLANES
    for cg in range(ncg):
        hp[cg, 0:CONV_HALO, :] = jnp.zeros((CONV_HALO, LANES), F32)
        hp[cg, t + CONV_HALO:t + 2 * CONV_HALO, :] = jnp.zeros((CONV_HALO, LANES), F32)

    def fill(i, carry):
        r0 = pl.multiple_of(i * CONV_TILE, CONV_TILE)
        hx = h_ref[0, pl.ds(r0, CONV_TILE), :].astype(F32)
        for cg in range(ncg):
            hp[cg, pl.ds(r0 + CONV_HALO, CONV_TILE), :] = hx[:, cg * LANES:(cg + 1) * LANES]
        return carry

    lax.fori_loop(0, nt, fill, 0)

    def tile(i, carry):
        r0 = pl.multiple_of(i * CONV_TILE, CONV_TILE)

        def group(cg, inner):
            stage[...] = hp[cg, pl.ds(r0, CONV_TILE + 2 * CONV_HALO), :]
            acc = jnp.zeros((CONV_TILE, LANES), F32)
            for r in range(SUBLANES):
                shifted[r] = stage[r:r + span, :]
                for k in range(CONV_K):
                    if (first + k) % SUBLANES == r:
                        a = first + k - r
                        acc = acc + shifted[r, a:a + CONV_TILE, :] * wdw_ref[cg, k:k + 1, :]
            cv[cg] = acc + bdw_ref[cg]
            return inner

        lax.fori_loop(0, ncg, group, 0)
        z = jnp.concatenate([cv[cg] for cg in range(ncg)], axis=1)
        z = _layer_norm(z, g_ref[...], b_ref[...])
        o_ref[0, pl.ds(r0, CONV_TILE), :] = (z * _sigmoid(z)).astype(BF16)
        return carry

    lax.fori_loop(0, nt, tile, 0)


def _conv_branch(glu, l, wts):
    bsz, t, _ = glu.shape
    ncg = CONV_DIM // LANES
    return pl.pallas_call(
        functools.partial(_conv_kernel, t=t),
        name="conv",
        grid=(bsz,),
        in_specs=[pl.BlockSpec((1, t, CONV_DIM), lambda b: (b, 0, 0)),
                  _layer_spec(l, (ncg, CONV_K + 1, LANES)), _layer_spec(l, (ncg, 1, LANES)),
                  _layer_spec(l, (1, CONV_DIM)), _layer_spec(l, (1, CONV_DIM))],
        out_specs=pl.BlockSpec((1, t, CONV_DIM), lambda b: (b, 0, 0)),
        out_shape=jax.ShapeDtypeStruct((bsz, t, CONV_DIM), BF16),
        scratch_shapes=[pltpu.VMEM((ncg, t + 2 * CONV_HALO, LANES), F32),
                        pltpu.VMEM((CONV_TILE + 2 * CONV_HALO, LANES), F32),
                        pltpu.VMEM((ncg, CONV_TILE, LANES), F32),
                        pltpu.VMEM((SUBLANES, CONV_TILE + (CONV_K // SUBLANES) * SUBLANES, LANES), F32)],
        compiler_params=pltpu.CompilerParams(
            dimension_semantics=("parallel",), vmem_limit_bytes=VMEM_LIMIT),
    )(glu, wts["w_dw"], wts["b_dw"], wts["cn_g"], wts["cn_b"])


N_CHAIN = 2 * ML_HEADS


def _mlstm_kernel(*refs, nc_c, nc_x, ctx_out):
    refs = list(refs)
    qk_c_ref, vt_c_ref, gt_c_ref = refs[:3]
    so_c_ref = refs[3] if ctx_out else None
    qk_x_ref, vt_x_ref, gt_x_ref, so_x_ref, mlg_ref = refs[3 + ctx_out:8 + ctx_out]
    outs = refs[8 + ctx_out:9 + 2 * ctx_out]
    hn_c_ref, hn_x_ref = (outs[0] if ctx_out else None), outs[-1]
    hf_c, hb_c, hf_x, hb_x, c_scr, m_scr = refs[9 + 2 * ctx_out:]
    L = ML_CHUNK
    dh = ML_HEAD_DIM
    row = lax.broadcasted_iota(jnp.int32, (L, L), 0)
    col = lax.broadcasted_iota(jnp.int32, (L, L), 1)
    upper = row <= col
    lower = row >= col
    tri2 = jnp.concatenate([upper.astype(F32), lower.astype(F32)], axis=1)
    is_fwd = lax.broadcasted_iota(jnp.int32, (N_CHAIN, L), 0) < ML_HEADS

    c_scr[...] = jnp.zeros(c_scr.shape, F32)
    m_scr[...] = jnp.zeros(m_scr.shape, F32)

    def iteration(qk_ref, vt_ref, gt_ref, hf, hb, cf, cb):
        gf = gt_ref[0, cf]
        gb = gt_ref[0, cb]
        li = jnp.where(is_fwd, gf[:N_CHAIN], gb[:N_CHAIN])
        lf = _log_sigmoid(jnp.where(is_fwd, gf[N_CHAIN:], gb[N_CHAIN:]))
        cum2 = jnp.dot(lf, tri2, preferred_element_type=F32, precision=lax.Precision.HIGHEST)
        bc = jnp.where(is_fwd, cum2[:, :L], cum2[:, L:])
        r = li - bc
        b_last = jnp.sum(lf, axis=1, keepdims=True)
        m = m_scr[:, 0:1]
        w_src = b_last + r
        m_new = jnp.maximum(b_last + m, jnp.max(w_src, axis=1, keepdims=True))
        a_state = jnp.exp(b_last + m - m_new)
        w = jnp.exp(w_src - m_new)
        inter = bc + m
        r_cols = jnp.concatenate([r, jnp.zeros((L - N_CHAIN, L), F32)], axis=0).T
        m_scr[...] = jnp.broadcast_to(m_new, m_scr.shape)
        early = []
        for j in range(N_CHAIN):
            d, h = divmod(j, ML_HEADS)
            cc = cf if d == 0 else cb
            r0 = pl.multiple_of(cc * L, L)
            q = qk_ref[0, pl.ds(r0, L), h * dh:(h + 1) * dh]
            k = qk_ref[0, pl.ds(r0, L), ML_DIM + h * dh:ML_DIM + (h + 1) * dh]
            vt = vt_ref[0, cc, h * ML_AUG:(h + 1) * ML_AUG, :]
            ct = c_scr[j]
            both = lax.dot_general(jnp.concatenate([k, ct.astype(BF16)], axis=0), q, (((1,), (1,)), ((), ())),
                                   preferred_element_type=F32)
            wv = (vt.astype(F32) * w[j:j + 1, :]).astype(BF16)
            c_scr[j] = a_state[j:j + 1, :] * ct + jnp.dot(wv, k, preferred_element_type=F32)
            early.append((both, vt))
        for j in range(N_CHAIN):
            d, h = divmod(j, ML_HEADS)
            cc = cf if d == 0 else cb
            mask = upper if d == 0 else lower
            both, vt = early[j]
            log_d = jnp.where(mask, bc[j:j + 1, :] + r_cols[:, j:j + 1], -jnp.inf)
            m_row = jnp.maximum(inter[j:j + 1, :], jnp.max(log_d, axis=0, keepdims=True))
            a_inter = jnp.exp(inter[j:j + 1, :] - m_row)
            dmat = jnp.exp(log_d - m_row)
            pt = (both[:L] * dmat).astype(BF16)
            num_aug = a_inter * both[L:] + jnp.dot(vt, pt, preferred_element_type=F32)
            den = num_aug[dh:dh + 1, :]
            hs = hf if d == 0 else hb
            hs[cc, h * dh:(h + 1) * dh, :] = num_aug[:dh] / jnp.maximum(jnp.abs(den), jnp.exp(-m_row))

    def scan(qk_ref, vt_ref, gt_ref, hf, hb, nc):
        def body(i, carry):
            iteration(qk_ref, vt_ref, gt_ref, hf, hb, i, nc - 1 - i)
            return carry

        lax.fori_loop(0, nc, body, 0, unroll=min(2, nc))

    scan(qk_c_ref, vt_c_ref, gt_c_ref, hf_c, hb_c, nc_c)
    scan(qk_x_ref, vt_x_ref, gt_x_ref, hf_x, hb_x, nc_x)

    def finish(hf, hb, so_ref, out_ref, nc):
        def body(c, carry):
            r0 = pl.multiple_of(c * L, L)
            hsum = hf[c] + hb[c]
            for h in range(ML_HEADS):
                sl = slice(h * dh, (h + 1) * dh)
                z = hsum[sl, :]
                mu = jnp.mean(z, axis=0, keepdims=True)
                zc = z - mu
                var = jnp.mean(zc * zc, axis=0, keepdims=True)
                hn = (zc * lax.rsqrt(var + LN_EPS)).T * mlg_ref[:, sl]
                out_ref[0, pl.ds(r0, L), sl] = (so_ref[0, pl.ds(r0, L), sl].astype(F32) * hn).astype(BF16)
            return carry

        lax.fori_loop(0, nc, body, 0)

    if ctx_out:
        finish(hf_c, hb_c, so_c_ref, hn_c_ref, nc_c)
    finish(hf_x, hb_x, so_x_ref, hn_x_ref, nc_x)


def _mlstm_branch(pc, px, l, wts, ctx_out):
    bsz, t_c, _ = pc["qk"].shape
    t_x = px["qk"].shape[1]
    nc_c, nc_x = t_c // ML_CHUNK, t_x // ML_CHUNK
    seq = lambda t, n: pl.BlockSpec((1, t, n), lambda b: (b, 0, 0))
    chunked = lambda nc, n: pl.BlockSpec((1, nc, n, ML_CHUNK), lambda b: (b, 0, 0, 0))
    ins = lambda t, nc, so: [seq(t, 2 * ML_DIM), chunked(nc, ML_HEADS * ML_AUG), chunked(nc, 4 * ML_HEADS)] + (
        [seq(t, ML_DIM)] if so else [])
    args = lambda p, so: [p["qk"], p["vtm"], p["gt"]] + ([p["so"]] if so else [])
    hbuf = lambda nc: pltpu.VMEM((nc, ML_DIM, ML_CHUNK), F32)
    out_t = ([t_c] if ctx_out else []) + [t_x]
    res = pl.pallas_call(
        functools.partial(_mlstm_kernel, nc_c=nc_c, nc_x=nc_x, ctx_out=ctx_out),
        name="mlstm",
        grid=(bsz,),
        in_specs=ins(t_c, nc_c, ctx_out) + ins(t_x, nc_x, True) + [_layer_spec(l, (1, ML_DIM))],
        out_specs=[seq(t, ML_DIM) for t in out_t],
        out_shape=[jax.ShapeDtypeStruct((bsz, t, ML_DIM), BF16) for t in out_t],
        scratch_shapes=[hbuf(nc_c), hbuf(nc_c), hbuf(nc_x), hbuf(nc_x),
                        pltpu.VMEM((N_CHAIN, ML_AUG, ML_HEAD_DIM), F32),
                        pltpu.VMEM((N_CHAIN, LANES), F32)],
        compiler_params=pltpu.CompilerParams(
            dimension_semantics=("parallel",), vmem_limit_bytes=VMEM_LIMIT),
    )(*args(pc, ctx_out), *args(px, True), wts["ml_g"])
    return (res[0], res[1]) if ctx_out else (None, res[0])


ATT_KEYS = 256
ATT_QG = 256
ATT_LOOKAHEAD = 8


def _attn_kernel(*refs, n_seg):
    q_ref = refs[0]
    k_refs = refs[1:1 + n_seg]
    vt_refs = refs[1 + n_seg:1 + 2 * n_seg]
    o_ref = refs[1 + 2 * n_seg]
    qg = min(ATT_QG, q_ref.shape[1])
    n_qg = q_ref.shape[1] // qg
    chunks = [(k_ref, vt_ref, j) for k_ref, vt_ref in zip(k_refs, vt_refs)
              for j in range(k_ref.shape[1] // ATT_KEYS)]
    steps = [(ci, h, g) for ci in range(len(chunks)) for h in range(MLA_HEADS) for g in range(n_qg)]

    def scores(ci, h, g):
        k_ref, _, j = chunks[ci]
        hs = slice(h * HEAD_PAD, (h + 1) * HEAD_PAD)
        return lax.dot_general(k_ref[0, j * ATT_KEYS:(j + 1) * ATT_KEYS, hs], q_ref[0, g * qg:(g + 1) * qg, hs],
                               (((1,), (1,)), ((), ())), preferred_element_type=F32)

    state = {}
    pending = {}
    for idx in range(len(steps) + ATT_LOOKAHEAD):
        if idx < len(steps):
            pending[idx] = scores(*steps[idx])
        if idx < ATT_LOOKAHEAD:
            continue
        ci, h, g = steps[idx - ATT_LOOKAHEAD]
        s = pending.pop(idx - ATT_LOOKAHEAD)
        _, vt_ref, j = chunks[ci]
        vtj = vt_ref[0, h * ATT_AUG:(h + 1) * ATT_AUG, j * ATT_KEYS:(j + 1) * ATT_KEYS]
        cm = jnp.max(s, axis=0, keepdims=True)
        if ci == 0:
            m_new = cm
            acc = jnp.dot(vtj, jnp.exp2(s - m_new).astype(BF16), preferred_element_type=F32)
        else:
            m_old, acc_old = state[h, g]
            m_new = jnp.maximum(m_old, cm)
            acc = acc_old * jnp.exp2(m_old - m_new) + jnp.dot(
                vtj, jnp.exp2(s - m_new).astype(BF16), preferred_element_type=F32)
        state[h, g] = (m_new, acc)
    outs = []
    for h in range(MLA_HEADS):
        accs = [state[h, g][1] for g in range(n_qg)]
        outs.append(jnp.concatenate([a[:V_HEAD] * (1.0 / a[V_HEAD:V_HEAD + 1]) for a in accs], axis=1))
    o_ref[0] = jnp.concatenate(outs, axis=0).T.astype(BF16)


def _attention(q, ks, vts, tq):
    bsz, t, hq = q.shape
    n_seg = len(ks)
    hv = MLA_HEADS * V_HEAD
    kspec = [pl.BlockSpec((1, k.shape[1], hq), lambda b, i: (b, 0, 0)) for k in ks]
    vspec = [pl.BlockSpec((1, MLA_HEADS * ATT_AUG, vt.shape[2]), lambda b, i: (b, 0, 0)) for vt in vts]
    return pl.pallas_call(
        functools.partial(_attn_kernel, n_seg=n_seg),
        name="attention",
        grid=(bsz, t // tq),
        in_specs=[pl.BlockSpec((1, tq, hq), lambda b, i: (b, i, 0))] + kspec + vspec,
        out_specs=pl.BlockSpec((1, tq, hv), lambda b, i: (b, i, 0)),
        out_shape=jax.ShapeDtypeStruct((bsz, t, hv), BF16),
        compiler_params=pltpu.CompilerParams(
            dimension_semantics=("parallel", "arbitrary"), vmem_limit_bytes=VMEM_LIMIT),
    )(q, *ks, *vts)


FF_CHUNK = 1024
MERGE_GROUPS = 2


def _merge_mlp_kernel(x_ref, hc_ref, hm_ref, oa_ref, gates_ref, g1_ref, sc2_ref, sh2_ref, g2_ref,
                      wc_ref, wm_ref, wa_ref, wo_ref, bo_ref, ln1g_ref, ln1b_ref,
                      w1_ref, b1_ref, w2_ref, b2_ref, ln2g_ref, ln2b_ref, o_ref):
    d = D_MODEL
    tm = x_ref.shape[1]
    rows = [slice(g * tm // MERGE_GROUPS, (g + 1) * tm // MERGE_GROUPS) for g in range(MERGE_GROUPS)]
    dot = functools.partial(jnp.dot, preferred_element_type=F32)
    yc = [dot(hc_ref[0, r, :], wc_ref[...]) for r in rows]
    ym = [dot(hm_ref[0, r, :], wm_ref[...]) for r in rows]
    ya = [dot(oa_ref[0, r, :], wa_ref[...]) for r in rows]
    mix = [(gates_ref[0, r, 0:d].astype(F32) * c + gates_ref[0, r, d:2 * d].astype(F32) * m
            + gates_ref[0, r, 2 * d:3 * d].astype(F32) * a).astype(BF16) for r, c, m, a in zip(rows, yc, ym, ya)]
    y = [dot(mx, wo_ref[...]) + bo_ref[...] for mx in mix]
    x1 = [_layer_norm(ALPHA * x_ref[0, r, :] + g1_ref[...] * yy, ln1g_ref[...], ln1b_ref[...])
          for r, yy in zip(rows, y)]
    u2 = [(xx * (1.0 + sc2_ref[...]) + sh2_ref[...]).astype(BF16) for xx in x1]
    acc = [jnp.zeros(xx.shape, F32) for xx in x1]
    for j in range(D_FF // FF_CHUNK):
        sl = slice(j * FF_CHUNK, (j + 1) * FF_CHUNK)
        hdn = [jnp.maximum(dot(u, w1_ref[:, sl]) + b1_ref[:, sl], 0.0) for u in u2]
        acc = [ac + dot((hd * hd).astype(BF16), w2_ref[sl, :]) for ac, hd in zip(acc, hdn)]
    for r, xx, ac in zip(rows, x1, acc):
        o_ref[0, r, :] = _layer_norm(ALPHA * xx + g2_ref[...] * (ac + b2_ref[...]), ln2g_ref[...], ln2b_ref[...])


def _merge_mlp(x, hc, hm, oa, gates, mod, l, mod_row, wts, tm):
    bsz, t, d = x.shape
    tok = lambda n: pl.BlockSpec((1, tm, n), lambda b, i: (b, i, 0))
    mods = [_mod_spec(l, k, mod_row) for k in (MOD_G1, MOD_SC2, MOD_SH2, MOD_G2)]
    return pl.pallas_call(
        _merge_mlp_kernel,
        name="merge_mlp",
        grid=(bsz, t // tm),
        in_specs=[tok(d), tok(CONV_DIM), tok(ML_DIM), tok(MLA_HEADS * V_HEAD), tok(3 * d)] + mods + [
            _layer_spec(l, (CONV_DIM, d)), _layer_spec(l, (ML_DIM, d)), _layer_spec(l, (MLA_HEADS * V_HEAD, d)),
            _layer_spec(l, (d, d)), _layer_spec(l, (1, d)), _layer_spec(l, (1, d)), _layer_spec(l, (1, d)),
            _layer_spec(l, (d, D_FF)), _layer_spec(l, (1, D_FF)), _layer_spec(l, (D_FF, d)), _layer_spec(l, (1, d)),
            _layer_spec(l, (1, d)), _layer_spec(l, (1, d))],
        out_specs=tok(d),
        out_shape=jax.ShapeDtypeStruct((bsz, t, d), F32),
        compiler_params=pltpu.CompilerParams(
            dimension_semantics=("parallel", "arbitrary"), vmem_limit_bytes=VMEM_LIMIT),
    )(x, hc, hm, oa, gates, mod, mod, mod, mod,
      wts["w_conv_out"], wts["w_ml_out"], wts["w_mla_out"], wts["w_out"], wts["b_out"], wts["ln1_g"], wts["ln1_b"],
      wts["w1"], wts["b1"], wts["w2"], wts["b2"], wts["ln2_g"], wts["ln2_b"])


def _rope_runs():
    runs = []
    for half in range(2):
        for a in range(2):
            start = a * 2 * ROPE_FREQ + half * ROPE_FREQ
            runs.append(slice(start, start + ROPE_FREQ))
    return runs


def _pack_weights(w_in, b_in, w_dw, b_dw, cn_g, cn_b, w_conv_out, ml_g, w_ml_out, qn_g, w_uq, kvn_g, w_ukv,
                  w_mla_out, w_out, b_out, ln1_g, ln1_b, w1, b1, w2, b2, ln2_g, ln2_b):
    nl = w_in.shape[0]

    def packed(z, dtype):
        a, qkv, o, g, cq, ckv, kr, gates = (z[..., _SRC[i]:_SRC[i + 1]].astype(dtype) for i in range(8))
        zeros = lambda n: jnp.zeros(z.shape[:-1] + (n,), dtype)
        kr_block = [zeros(QK_NOPE)] + [kr[..., s] for s in _rope_runs()] + [zeros(HEAD_PAD - QK_NOPE - QK_ROPE)]
        g_block = [g[..., i * ML_HEADS:(i + 1) * ML_HEADS] for i in (0, 2, 1, 3)] + [zeros(LANES - 4 * ML_HEADS)]
        return jnp.concatenate([a, qkv, o, cq, ckv] + kr_block + g_block + [gates], axis=-1)

    wq = w_uq.reshape(nl, Q_LORA, MLA_HEADS, QK_NOPE + QK_ROPE)
    wq_p = jnp.concatenate([wq[..., :QK_NOPE]] + [wq[..., QK_NOPE:][..., s] for s in _rope_runs()]
                           + [jnp.zeros(wq.shape[:-1] + (HEAD_PAD - QK_NOPE - QK_ROPE,), F32)], axis=-1)
    wkv = w_ukv.reshape(nl, KV_LORA, MLA_HEADS, QK_NOPE + V_HEAD)
    wkn_p = jnp.concatenate([wkv[..., :QK_NOPE], jnp.zeros(wkv.shape[:-1] + (HEAD_PAD - QK_NOPE,), F32)], axis=-1)
    wv_t = wkv[..., QK_NOPE:].reshape(nl, KV_LORA, MLA_HEADS * V_HEAD).transpose(0, 2, 1)
    ncg = CONV_DIM // LANES
    row = lambda z: z[:, None, :]
    return {
        "w_pack": packed(w_in, BF16), "b_pack": row(packed(b_in, F32)),
        "qn_g": row(qn_g), "w_uq": wq_p.reshape(nl, Q_LORA, MLA_HEADS * HEAD_PAD).astype(BF16),
        "kvn_g": row(kvn_g), "w_kn": wkn_p.reshape(nl, KV_LORA, MLA_HEADS * HEAD_PAD).astype(BF16),
        "w_v": wv_t.astype(BF16),
        "w_dw": jnp.pad(w_dw, ((0, 0), (0, 1), (0, 0))).reshape(nl, CONV_K + 1, ncg, LANES).transpose(0, 2, 1, 3),
        "b_dw": b_dw.reshape(nl, ncg, 1, LANES),
        "cn_g": row(cn_g), "cn_b": row(cn_b), "ml_g": row(ml_g),
        "w_conv_out": w_conv_out.astype(BF16), "w_ml_out": w_ml_out.astype(BF16),
        "w_mla_out": w_mla_out.astype(BF16), "w_out": w_out.astype(BF16), "b_out": row(b_out),
        "ln1_g": row(ln1_g), "ln1_b": row(ln1_b), "w1": w1.astype(BF16), "b1": row(b1),
        "w2": w2.astype(BF16), "b2": row(b2), "ln2_g": row(ln2_g), "ln2_b": row(ln2_b),
    }


def _rope_tables(n_tokens):
    rows = n_tokens // GRID_W
    rr, cc = np.meshgrid(np.arange(rows, dtype=np.float32), np.arange(GRID_W, dtype=np.float32), indexing="ij")
    inv = (np.float32(ROPE_THETA) ** (-np.arange(ROPE_FREQ, dtype=np.float32) / np.float32(ROPE_FREQ))).astype(np.float32)
    ang = np.stack([rr.reshape(-1), cc.reshape(-1)], -1)[..., None] * inv
    cos = np.cos(ang).astype(np.float32).reshape(n_tokens, ROPE_HALF)
    sin = np.sin(ang).astype(np.float32).reshape(n_tokens, ROPE_HALF)
    one = np.ones((n_tokens, QK_NOPE), np.float32)
    z_nope = np.zeros((n_tokens, QK_NOPE), np.float32)
    z_half = np.zeros((n_tokens, ROPE_HALF), np.float32)
    z_pad = np.zeros((n_tokens, HEAD_PAD - QK_NOPE - QK_ROPE), np.float32)
    tc = np.concatenate([one, cos, cos, z_pad], axis=1)
    tsm = np.concatenate([z_nope, -sin, z_half, z_pad], axis=1)
    tsp = np.concatenate([z_nope, z_half, sin, z_pad], axis=1)
    return tc, tsm, tsp


def _identity_tables(n_tokens):
    return (np.ones((n_tokens, HEAD_PAD), np.float32), np.zeros((n_tokens, HEAD_PAD), np.float32),
            np.zeros((n_tokens, HEAD_PAD), np.float32))


def kernel(x, c, ctx, c_ctx, w_mod, b_mod, w_in, b_in, w_dw, b_dw, conv_norm_g, conv_norm_b, w_conv_out,
           mlstm_norm_g, w_mlstm_out, q_norm_g, w_uq, kv_norm_g, w_ukv, w_mla_out, w_out, b_out,
           ln1_g, ln1_b, w1, b1, w2, b2, ln2_g, ln2_b):
    bsz, t_x, d = x.shape
    t_c = ctx.shape[1]
    n_rows = 24
    cc = jnp.concatenate([c, c_ctx[None, :], jnp.zeros((n_rows - bsz - 1, d), F32)], axis=0)
    mod = _modulation(cc, w_mod, b_mod).reshape(DEPTH, n_rows, N_MOD, 1, d)
    wts = _pack_weights(w_in, b_in, w_dw, b_dw, conv_norm_g, conv_norm_b, w_conv_out, mlstm_norm_g,
                        w_mlstm_out, q_norm_g, w_uq, kv_norm_g, w_ukv, w_mla_out, w_out, b_out,
                        ln1_g, ln1_b, w1, b1, w2, b2, ln2_g, ln2_b)
    rope_x = _rope_tables(t_x)
    rope_c = _identity_tables(t_c)
    ctx_row = bsz

    for l in range(DEPTH):
        with_ctx = l < DEPTH - 1
        px = _in_projection(x, mod, l, None, wts, rope_x, 512)
        pc = _in_projection(ctx, mod, l, ctx_row, wts, rope_c, 256, keys_only=not with_ctx)

        hm_c, hm_x = _mlstm_branch(pc, px, l, wts, with_ctx)
        oa_x = _attention(px["q"], [pc["k"], px["k"]], [pc["vta"], px["vta"]], 512)
        hc_x = _conv_branch(px["glu"], l, wts)
        x_new = _merge_mlp(x, hc_x, hm_x, oa_x, px["gates"], mod, l, None, wts, 512)
        if with_ctx:
            hc_c = _conv_branch(pc["glu"], l, wts)
            oa_c = _attention(pc["q"], [pc["k"]], [pc["vta"]], 256)
            ctx = _merge_mlp(ctx, hc_c, hm_c, oa_c, pc["gates"], mod, l, ctx_row, wts, 256)
        x = x_new
    return x
```

```python
import functools

import numpy as np
import jax
import jax.numpy as jnp
from jax import lax
from jax.experimental import pallas as pl
from jax.experimental.pallas import tpu as pltpu

F32 = jnp.float32
BF16 = jnp.bfloat16

D_MODEL = 1024
DEPTH = 2
GRID_W = 64
CONV_DIM = 512
CONV_K = 31
ML_HEADS = 4
ML_HEAD_DIM = 128
ML_DIM = ML_HEADS * ML_HEAD_DIM
ML_CHUNK = 256
MLA_HEADS = 8
QK_NOPE = 64
QK_ROPE = 32
V_HEAD = 64
Q_LORA = 768
KV_LORA = 256
ROPE_THETA = 10000.0
ROPE_FREQ = QK_ROPE // 4
MLA_SCALE = (QK_NOPE + QK_ROPE) ** -0.5
ML_SCALE = ML_HEAD_DIM ** -0.5
D_FF = 4 * D_MODEL
LN_EPS = 1e-5
ALPHA = (2 * DEPTH) ** 0.25
N_MOD = 6
MOD_SH1, MOD_SC1, MOD_G1, MOD_SH2, MOD_SC2, MOD_G2 = range(N_MOD)

LANES = 128
SUBLANES = 8
HEAD_PAD = LANES
ROPE_HALF = QK_ROPE // 2
ONES_ROWS = 16
ML_AUG = ML_HEAD_DIM + ONES_ROWS
ATT_AUG = V_HEAD + ONES_ROWS
Q_SCALE = MLA_SCALE * float(np.log2(np.e))
VMEM_LIMIT = 56 * 1024 * 1024

OFF_A = 0
OFF_QKV = OFF_A + 2 * CONV_DIM
OFF_O = OFF_QKV + 3 * ML_DIM
OFF_CQ = OFF_O + ML_DIM
OFF_CKV = OFF_CQ + Q_LORA
OFF_KR = OFF_CKV + KV_LORA
OFF_GIF = OFF_KR + HEAD_PAD
OFF_GATES = OFF_GIF + LANES
N_PACK = OFF_GATES + 3 * D_MODEL

_COLS = (2 * CONV_DIM, 3 * ML_DIM, ML_DIM, 4 * ML_HEADS, Q_LORA, KV_LORA, QK_ROPE, 3 * D_MODEL)
_SRC = tuple(int(s) for s in np.cumsum((0,) + _COLS))


def _layer_spec(l, shape):
    nd = len(shape)
    return pl.BlockSpec((None,) + tuple(shape), lambda *_: (l,) + (0,) * nd, pipeline_mode=pl.Buffered(1))


def _mod_spec(l, k, row=None):
    block = (None, None, None, 1, D_MODEL)
    if row is None:
        return pl.BlockSpec(block, lambda b, i: (l, b, k, 0, 0))
    return pl.BlockSpec(block, lambda b, i: (l, row, k, 0, 0))


def _sigmoid(z):
    return 1.0 / (1.0 + jnp.exp(-z))


def _log_sigmoid(z):
    return jnp.minimum(z, 0.0) - jnp.log1p(jnp.exp(-jnp.abs(z)))


def _layer_norm(z, g, b):
    mu = jnp.mean(z, axis=-1, keepdims=True)
    zc = z - mu
    var = jnp.mean(zc * zc, axis=-1, keepdims=True)
    return zc * lax.rsqrt(var + LN_EPS) * g + b


def _mod_kernel(c_ref, w_ref, b_ref, o_ref):
    c = c_ref[...]
    s = (c * _sigmoid(c)).astype(BF16)
    o_ref[0] = jnp.dot(s, w_ref[0].astype(BF16), preferred_element_type=F32) + b_ref[0]


def _modulation(cc, w_mod, b_mod):
    nl, d, n = w_mod.shape
    r = cc.shape[0]
    tn = 1024
    return pl.pallas_call(
        _mod_kernel,
        name="modulation",
        grid=(nl, n // tn),
        in_specs=[
            pl.BlockSpec((r, d), lambda l, j: (0, 0)),
            pl.BlockSpec((1, d, tn), lambda l, j: (l, 0, j)),
            pl.BlockSpec((1, 1, tn), lambda l, j: (l, 0, j)),
        ],
        out_specs=pl.BlockSpec((1, r, tn), lambda l, j: (l, 0, j)),
        out_shape=jax.ShapeDtypeStruct((nl, r, n), F32),
        compiler_params=pltpu.CompilerParams(
            dimension_semantics=("arbitrary", "arbitrary"), vmem_limit_bytes=VMEM_LIMIT),
    )(cc, w_mod, b_mod.reshape(nl, 1, n))


def _rope(z, tc, tsm, tsp):
    return z * tc + pltpu.roll(z, HEAD_PAD - ROPE_HALF, 1) * tsm + pltpu.roll(z, ROPE_HALF, 1) * tsp


def _inproj_kernel(x_ref, sc_ref, sh_ref, w_ref, b_ref, qng_ref, wuq_ref, kvg_ref, wkn_ref, wv_ref,
                   tc_ref, tsm_ref, tsp_ref, *out_refs, keys_only):
    if keys_only:
        qk_ref, vtm_ref, gt_ref, k_ref, vta_ref = out_refs
    else:
        glu_ref, qk_ref, vtm_ref, so_ref, gt_ref, gates_ref, q_ref, k_ref, vta_ref = out_refs
    u = (x_ref[0] * (1.0 + sc_ref[...]) + sh_ref[...]).astype(BF16)
    tm = u.shape[0]
    ones_rows = jnp.ones((ONES_ROWS, tm), BF16)

    def proj(off, n):
        return jnp.dot(u, w_ref[:, off:off + n], preferred_element_type=F32) + b_ref[:, off:off + n]

    if not keys_only:
        cq = proj(OFF_CQ, Q_LORA)
        nq = (cq * lax.rsqrt(jnp.mean(cq * cq, axis=-1, keepdims=True) + LN_EPS) * qng_ref[...]).astype(BF16)
    ckv = proj(OFF_CKV, KV_LORA)
    nkv = (ckv * lax.rsqrt(jnp.mean(ckv * ckv, axis=-1, keepdims=True) + LN_EPS) * kvg_ref[...]).astype(BF16)

    if not keys_only:
        a = proj(OFF_A, 2 * CONV_DIM)
        glu_ref[0] = (a[:, :CONV_DIM] * _sigmoid(a[:, CONV_DIM:])).astype(BF16)

    qk_ref[0, :, :ML_DIM] = proj(OFF_QKV, ML_DIM).astype(BF16)
    qk_ref[0, :, ML_DIM:] = (proj(OFF_QKV + ML_DIM, ML_DIM) * ML_SCALE).astype(BF16)
    vt = proj(OFF_QKV + 2 * ML_DIM, ML_DIM).T.astype(BF16)
    gt = proj(OFF_GIF, LANES).T
    for c in range(tm // ML_CHUNK):
        cs = slice(c * ML_CHUNK, (c + 1) * ML_CHUNK)
        gt_ref[0, c] = gt[:4 * ML_HEADS, cs]
        for h in range(ML_HEADS):
            vtm_ref[0, c, h * ML_AUG:h * ML_AUG + ML_HEAD_DIM, :] = vt[h * ML_HEAD_DIM:(h + 1) * ML_HEAD_DIM, cs]
            vtm_ref[0, c, h * ML_AUG + ML_HEAD_DIM:(h + 1) * ML_AUG, :] = ones_rows[:, cs]
    tc, tsm, tsp = tc_ref[...], tsm_ref[...], tsp_ref[...]
    if not keys_only:
        so_ref[0] = _sigmoid(proj(OFF_O, ML_DIM)).astype(BF16)
        for j in range(3):
            gates_ref[0, :, j * D_MODEL:(j + 1) * D_MODEL] = _sigmoid(
                proj(OFF_GATES + j * D_MODEL, D_MODEL)).astype(BF16)
        q = jnp.dot(nq, wuq_ref[...], preferred_element_type=F32)
        for h in range(MLA_HEADS):
            sl = slice(h * HEAD_PAD, (h + 1) * HEAD_PAD)
            q_ref[0, :, sl] = (_rope(q[:, sl], tc, tsm, tsp) * Q_SCALE).astype(BF16)

    kn = jnp.dot(nkv, wkn_ref[...], preferred_element_type=F32)
    kr = _rope(proj(OFF_KR, HEAD_PAD), tc, tsm, tsp)
    for h in range(MLA_HEADS):
        sl = slice(h * HEAD_PAD, (h + 1) * HEAD_PAD)
        k_ref[0, :, sl] = (kn[:, sl] + kr).astype(BF16)
    vta = lax.dot_general(wv_ref[...], nkv, (((1,), (1,)), ((), ())),
                          preferred_element_type=F32).astype(BF16)
    for h in range(MLA_HEADS):
        vta_ref[0, h * ATT_AUG:h * ATT_AUG + V_HEAD, :] = vta[h * V_HEAD:(h + 1) * V_HEAD, :]
        vta_ref[0, h * ATT_AUG + V_HEAD:(h + 1) * ATT_AUG, :] = ones_rows


def _in_projection(x, mod, l, mod_row, wts, tables, tm, keys_only=False):
    bsz, t, d = x.shape
    tok = lambda n: pl.BlockSpec((1, tm, n), lambda b, i: (b, i, 0))
    tab = pl.BlockSpec((tm, HEAD_PAD), lambda b, i: (i, 0))
    hq = MLA_HEADS * HEAD_PAD
    nc, cpt = t // ML_CHUNK, tm // ML_CHUNK
    seq = lambda n: (jax.ShapeDtypeStruct((bsz, t, n), BF16), tok(n))
    outputs = {
        "glu": seq(CONV_DIM),
        "qk": seq(2 * ML_DIM),
        "vtm": (jax.ShapeDtypeStruct((bsz, nc, ML_HEADS * ML_AUG, ML_CHUNK), BF16),
                pl.BlockSpec((1, cpt, ML_HEADS * ML_AUG, ML_CHUNK), lambda b, i: (b, i, 0, 0))),
        "so": seq(ML_DIM),
        "gt": (jax.ShapeDtypeStruct((bsz, nc, 4 * ML_HEADS, ML_CHUNK), F32),
               pl.BlockSpec((1, cpt, 4 * ML_HEADS, ML_CHUNK), lambda b, i: (b, i, 0, 0))),
        "gates": seq(3 * D_MODEL),
        "q": seq(hq),
        "k": seq(hq),
        "vta": (jax.ShapeDtypeStruct((bsz, MLA_HEADS * ATT_AUG, t), BF16),
                pl.BlockSpec((1, MLA_HEADS * ATT_AUG, tm), lambda b, i: (b, 0, i))),
    }
    if keys_only:
        outputs = {n: outputs[n] for n in ("qk", "vtm", "gt", "k", "vta")}
    out_shape = [v[0] for v in outputs.values()]
    out_specs = [v[1] for v in outputs.values()]
    res = pl.pallas_call(
        functools.partial(_inproj_kernel, keys_only=keys_only),
        name="in_projection",
        grid=(bsz, t // tm),
        in_specs=[tok(d), _mod_spec(l, MOD_SC1, mod_row), _mod_spec(l, MOD_SH1, mod_row),
                  _layer_spec(l, (d, N_PACK)), _layer_spec(l, (1, N_PACK)),
                  _layer_spec(l, (1, Q_LORA)), _layer_spec(l, (Q_LORA, hq)),
                  _layer_spec(l, (1, KV_LORA)), _layer_spec(l, (KV_LORA, hq)),
                  _layer_spec(l, (MLA_HEADS * V_HEAD, KV_LORA)),
                  tab, tab, tab],
        out_specs=out_specs,
        out_shape=out_shape,
        compiler_params=pltpu.CompilerParams(
            dimension_semantics=("parallel", "arbitrary"), vmem_limit_bytes=VMEM_LIMIT),
    )(x, mod, mod, wts["w_pack"], wts["b_pack"], wts["qn_g"], wts["w_uq"], wts["kvn_g"], wts["w_kn"], wts["w_v"],
      *tables)
    return dict(zip(outputs, res))


CONV_HALO = 16
CONV_TILE = 128


def _conv_kernel(h_ref, wdw_ref, bdw_ref, g_ref, b_ref, o_ref, hp, stage, cv, shifted, *, t):
    nt = t // CONV_TILE
    ncg = CONV_DIM // LANES
    first = CONV_HALO - CONV_K // 2
    span = CONV_TILE + (CONV_K // SUBLANES) * SUBLANES
    for cg in range(ncg):
        hp[cg, 0:CONV_HALO, :] = jnp.zeros((CONV_HALO, LANES), F32)
        hp[cg, t + CONV_HALO:t + 2 * CONV_HALO, :] = jnp.zeros((CONV_HALO, LANES), F32)

    def fill(i, carry):
        r0 = pl.multiple_of(i * CONV_TILE, CONV_TILE)
        hx = h_ref[0, pl.ds(r0, CONV_TILE), :].astype(F32)
        for cg in range(ncg):
            hp[cg, pl.ds(r0 + CONV_HALO, CONV_TILE), :] = hx[:, cg * LANES:(cg + 1) * LANES]
        return carry

    lax.fori_loop(0, nt, fill, 0)

    def tile(i, carry):
        r0 = pl.multiple_of(i * CONV_TILE, CONV_TILE)

        def group(cg, inner):
            stage[...] = hp[cg, pl.ds(r0, CONV_TILE + 2 * CONV_HALO), :]
            acc = jnp.zeros((CONV_TILE, LANES), F32)
            for r in range(SUBLANES):
                shifted[r] = stage[r:r + span, :]
                for k in range(CONV_K):
                    if (first + k) % SUBLANES == r:
                        a = first + k - r
                        acc = acc + shifted[r, a:a + CONV_TILE, :] * wdw_ref[cg, k:k + 1, :]
            cv[cg] = acc + bdw_ref[cg]
            return inner

        lax.fori_loop(0, ncg, group, 0)
        z = jnp.concatenate([cv[cg] for cg in range(ncg)], axis=1)
        z = _layer_norm(z, g_ref[...], b_ref[...])
        o_ref[0, pl.ds(r0, CONV_TILE), :] = (z * _sigmoid(z)).astype(BF16)
        return carry

    lax.fori_loop(0, nt, tile, 0)


def _conv_branch(glu, l, wts):
    bsz, t, _ = glu.shape
    ncg = CONV_DIM // LANES
    return pl.pallas_call(
        functools.partial(_conv_kernel, t=t),
        name="conv",
        grid=(bsz,),
        in_specs=[pl.BlockSpec((1, t, CONV_DIM), lambda b: (b, 0, 0)),
                  _layer_spec(l, (ncg, CONV_K + 1, LANES)), _layer_spec(l, (ncg, 1, LANES)),
                  _layer_spec(l, (1, CONV_DIM)), _layer_spec(l, (1, CONV_DIM))],
        out_specs=pl.BlockSpec((1, t, CONV_DIM), lambda b: (b, 0, 0)),
        out_shape=jax.ShapeDtypeStruct((bsz, t, CONV_DIM), BF16),
        scratch_shapes=[pltpu.VMEM((ncg, t + 2 * CONV_HALO, LANES), F32),
                        pltpu.VMEM((CONV_TILE + 2 * CONV_HALO, LANES), F32),
                        pltpu.VMEM((ncg, CONV_TILE, LANES), F32),
                        pltpu.VMEM((SUBLANES, CONV_TILE + (CONV_K // SUBLANES) * SUBLANES, LANES), F32)],
        compiler_params=pltpu.CompilerParams(
            dimension_semantics=("parallel",), vmem_limit_bytes=VMEM_LIMIT),
    )(glu, wts["w_dw"], wts["b_dw"], wts["cn_g"], wts["cn_b"])


N_CHAIN = 2 * ML_HEADS


def _mlstm_kernel(*refs, nc_c, nc_x, ctx_out):
    refs = list(refs)
    qk_c_ref, vt_c_ref, gt_c_ref = refs[:3]
    so_c_ref = refs[3] if ctx_out else None
    qk_x_ref, vt_x_ref, gt_x_ref, so_x_ref, mlg_ref = refs[3 + ctx_out:8 + ctx_out]
    outs = refs[8 + ctx_out:9 + 2 * ctx_out]
    hn_c_ref, hn_x_ref = (outs[0] if ctx_out else None), outs[-1]
    hf_c, hb_c, hf_x, hb_x, c_scr, m_scr = refs[9 + 2 * ctx_out:]
    L = ML_CHUNK
    dh = ML_HEAD_DIM
    row = lax.broadcasted_iota(jnp.int32, (L, L), 0)
    col = lax.broadcasted_iota(jnp.int32, (L, L), 1)
    upper = row <= col
    lower = row >= col
    tri2 = jnp.concatenate([upper.astype(F32), lower.astype(F32)], axis=1)
    is_fwd = lax.broadcasted_iota(jnp.int32, (N_CHAIN, L), 0) < ML_HEADS

    c_scr[...] = jnp.zeros(c_scr.shape, F32)
    m_scr[...] = jnp.zeros(m_scr.shape, F32)

    def iteration(qk_ref, vt_ref, gt_ref, hf, hb, cf, cb):
        gf = gt_ref[0, cf]
        gb = gt_ref[0, cb]
        li = jnp.where(is_fwd, gf[:N_CHAIN], gb[:N_CHAIN])
        lf = _log_sigmoid(jnp.where(is_fwd, gf[N_CHAIN:], gb[N_CHAIN:]))
        cum2 = jnp.dot(lf, tri2, preferred_element_type=F32, precision=lax.Precision.HIGHEST)
        bc = jnp.where(is_fwd, cum2[:, :L], cum2[:, L:])
        r = li - bc
        b_last = jnp.sum(lf, axis=1, keepdims=True)
        m = m_scr[:, 0:1]
        w_src = b_last + r
        m_new = jnp.maximum(b_last + m, jnp.max(w_src, axis=1, keepdims=True))
        a_state = jnp.exp(b_last + m - m_new)
        w = jnp.exp(w_src - m_new)
        inter = bc + m
        r_cols = jnp.concatenate([r, jnp.zeros((L - N_CHAIN, L), F32)], axis=0).T
        m_scr[...] = jnp.broadcast_to(m_new, m_scr.shape)
        early = []
        for j in range(N_CHAIN):
            d, h = divmod(j, ML_HEADS)
            cc = cf if d == 0 else cb
            r0 = pl.multiple_of(cc * L, L)
            q = qk_ref[0, pl.ds(r0, L), h * dh:(h + 1) * dh]
            k = qk_ref[0, pl.ds(r0, L), ML_DIM + h * dh:ML_DIM + (h + 1) * dh]
            vt = vt_ref[0, cc, h * ML_AUG:(h + 1) * ML_AUG, :]
            ct = c_scr[j]
            both = lax.dot_general(jnp.concatenate([k, ct.astype(BF16)], axis=0), q, (((1,), (1,)), ((), ())),
                                   preferred_element_type=F32)
            early.append((both, vt, k, ct))
        for j in range(N_CHAIN):
            _, vt, k, ct = early[j]
            wv = (vt.astype(F32) * w[j:j + 1, :]).astype(BF16)
            c_scr[j] = a_state[j:j + 1, :] * ct + jnp.dot(wv, k, preferred_element_type=F32)
        for j in range(N_CHAIN):
            d, h = divmod(j, ML_HEADS)
            cc = cf if d == 0 else cb
            mask = upper if d == 0 else lower
            both, vt, _, _ = early[j]
            log_d = jnp.where(mask, bc[j:j + 1, :] + r_cols[:, j:j + 1], -jnp.inf)
            m_row = jnp.maximum(inter[j:j + 1, :], jnp.max(log_d, axis=0, keepdims=True))
            a_inter = jnp.exp(inter[j:j + 1, :] - m_row)
            dmat = jnp.exp(log_d - m_row)
            pt = (both[:L] * dmat).astype(BF16)
            num_aug = a_inter * both[L:] + jnp.dot(vt, pt, preferred_element_type=F32)
            den = num_aug[dh:dh + 1, :]
            hs = hf if d == 0 else hb
            hs[cc, h * dh:(h + 1) * dh, :] = num_aug[:dh] / jnp.maximum(jnp.abs(den), jnp.exp(-m_row))

    def scan(qk_ref, vt_ref, gt_ref, hf, hb, nc):
        def body(i, carry):
            iteration(qk_ref, vt_ref, gt_ref, hf, hb, i, nc - 1 - i)
            return carry

        lax.fori_loop(0, nc, body, 0, unroll=min(2, nc))

    scan(qk_c_ref, vt_c_ref, gt_c_ref, hf_c, hb_c, nc_c)
    scan(qk_x_ref, vt_x_ref, gt_x_ref, hf_x, hb_x, nc_x)

    def finish(hf, hb, so_ref, out_ref, nc):
        def body(c, carry):
            r0 = pl.multiple_of(c * L, L)
            hsum = hf[c] + hb[c]
            for h in range(ML_HEADS):
                sl = slice(h * dh, (h + 1) * dh)
                z = hsum[sl, :]
                mu = jnp.mean(z, axis=0, keepdims=True)
                zc = z - mu
                var = jnp.mean(zc * zc, axis=0, keepdims=True)
                hn = (zc * lax.rsqrt(var + LN_EPS)).T * mlg_ref[:, sl]
                out_ref[0, pl.ds(r0, L), sl] = (so_ref[0, pl.ds(r0, L), sl].astype(F32) * hn).astype(BF16)
            return carry

        lax.fori_loop(0, nc, body, 0)

    if ctx_out:
        finish(hf_c, hb_c, so_c_ref, hn_c_ref, nc_c)
    finish(hf_x, hb_x, so_x_ref, hn_x_ref, nc_x)


def _mlstm_branch(pc, px, l, wts, ctx_out):
    bsz, t_c, _ = pc["qk"].shape
    t_x = px["qk"].shape[1]
    nc_c, nc_x = t_c // ML_CHUNK, t_x // ML_CHUNK
    seq = lambda t, n: pl.BlockSpec((1, t, n), lambda b: (b, 0, 0))
    chunked = lambda nc, n: pl.BlockSpec((1, nc, n, ML_CHUNK), lambda b: (b, 0, 0, 0))
    ins = lambda t, nc, so: [seq(t, 2 * ML_DIM), chunked(nc, ML_HEADS * ML_AUG), chunked(nc, 4 * ML_HEADS)] + (
        [seq(t, ML_DIM)] if so else [])
    args = lambda p, so: [p["qk"], p["vtm"], p["gt"]] + ([p["so"]] if so else [])
    hbuf = lambda nc: pltpu.VMEM((nc, ML_DIM, ML_CHUNK), F32)
    out_t = ([t_c] if ctx_out else []) + [t_x]
    res = pl.pallas_call(
        functools.partial(_mlstm_kernel, nc_c=nc_c, nc_x=nc_x, ctx_out=ctx_out),
        name="mlstm",
        grid=(bsz,),
        in_specs=ins(t_c, nc_c, ctx_out) + ins(t_x, nc_x, True) + [_layer_spec(l, (1, ML_DIM))],
        out_specs=[seq(t, ML_DIM) for t in out_t],
        out_shape=[jax.ShapeDtypeStruct((bsz, t, ML_DIM), BF16) for t in out_t],
        scratch_shapes=[hbuf(nc_c), hbuf(nc_c), hbuf(nc_x), hbuf(nc_x),
                        pltpu.VMEM((N_CHAIN, ML_AUG, ML_HEAD_DIM), F32),
                        pltpu.VMEM((N_CHAIN, LANES), F32)],
        compiler_params=pltpu.CompilerParams(
            dimension_semantics=("parallel",), vmem_limit_bytes=VMEM_LIMIT),
    )(*args(pc, ctx_out), *args(px, True), wts["ml_g"])
    return (res[0], res[1]) if ctx_out else (None, res[0])


ATT_KEYS = 256
ATT_QG = 256
ATT_LOOKAHEAD = 8


def _attn_kernel(*refs, n_seg):
    q_ref = refs[0]
    k_refs = refs[1:1 + n_seg]
    vt_refs = refs[1 + n_seg:1 + 2 * n_seg]
    o_ref = refs[1 + 2 * n_seg]
    qg = min(ATT_QG, q_ref.shape[1])
    n_qg = q_ref.shape[1] // qg
    chunks = [(k_ref, vt_ref, slice(j, j + min(ATT_KEYS, k_ref.shape[1])))
              for k_ref, vt_ref in zip(k_refs, vt_refs)
              for j in range(0, k_ref.shape[1], min(ATT_KEYS, k_ref.shape[1]))]
    steps = [(ci, h, g) for ci in range(len(chunks)) for h in range(MLA_HEADS) for g in range(n_qg)]

    def scores(ci, h, g):
        k_ref, _, ks = chunks[ci]
        hs = slice(h * HEAD_PAD, (h + 1) * HEAD_PAD)
        return lax.dot_general(k_ref[0, ks, hs], q_ref[0, g * qg:(g + 1) * qg, hs],
                               (((1,), (1,)), ((), ())), preferred_element_type=F32)

    state = {}
    pending = {}
    for idx in range(len(steps) + ATT_LOOKAHEAD):
        if idx < len(steps):
            pending[idx] = scores(*steps[idx])
        if idx < ATT_LOOKAHEAD:
            continue
        ci, h, g = steps[idx - ATT_LOOKAHEAD]
        s = pending.pop(idx - ATT_LOOKAHEAD)
        _, vt_ref, ks = chunks[ci]
        vtj = vt_ref[0, h * ATT_AUG:(h + 1) * ATT_AUG, ks]
        cm = jnp.max(s, axis=0, keepdims=True)
        if ci == 0:
            m_new = cm
            acc = jnp.dot(vtj, jnp.exp2(s - m_new).astype(BF16), preferred_element_type=F32)
        else:
            m_old, acc_old = state[h, g]
            m_new = jnp.maximum(m_old, cm)
            acc = acc_old * jnp.exp2(m_old - m_new) + jnp.dot(
                vtj, jnp.exp2(s - m_new).astype(BF16), preferred_element_type=F32)
        state[h, g] = (m_new, acc)
    outs = []
    for h in range(MLA_HEADS):
        accs = [state[h, g][1] for g in range(n_qg)]
        outs.append(jnp.concatenate([a[:V_HEAD] * (1.0 / a[V_HEAD:V_HEAD + 1]) for a in accs], axis=1))
    o_ref[0] = jnp.concatenate(outs, axis=0).T.astype(BF16)


def _attention(q, ks, vts, tq):
    bsz, t, hq = q.shape
    n_seg = len(ks)
    hv = MLA_HEADS * V_HEAD
    kspec = [pl.BlockSpec((1, k.shape[1], hq), lambda b, i: (b, 0, 0)) for k in ks]
    vspec = [pl.BlockSpec((1, MLA_HEADS * ATT_AUG, vt.shape[2]), lambda b, i: (b, 0, 0)) for vt in vts]
    return pl.pallas_call(
        functools.partial(_attn_kernel, n_seg=n_seg),
        name="attention",
        grid=(bsz, t // tq),
        in_specs=[pl.BlockSpec((1, tq, hq), lambda b, i: (b, i, 0))] + kspec + vspec,
        out_specs=pl.BlockSpec((1, tq, hv), lambda b, i: (b, i, 0)),
        out_shape=jax.ShapeDtypeStruct((bsz, t, hv), BF16),
        compiler_params=pltpu.CompilerParams(
            dimension_semantics=("parallel", "arbitrary"), vmem_limit_bytes=VMEM_LIMIT),
    )(q, *ks, *vts)


FF_CHUNK = 1024
MERGE_GROUPS = 2


def _merge_mlp_kernel(x_ref, hc_ref, hm_ref, oa_ref, gates_ref, g1_ref, sc2_ref, sh2_ref, g2_ref,
                      wc_ref, wm_ref, wa_ref, wo_ref, bo_ref, ln1g_ref, ln1b_ref,
                      w1_ref, b1_ref, w2_ref, b2_ref, ln2g_ref, ln2b_ref, o_ref):
    d = D_MODEL
    tm = x_ref.shape[1]
    rows = [slice(g * tm // MERGE_GROUPS, (g + 1) * tm // MERGE_GROUPS) for g in range(MERGE_GROUPS)]
    dot = functools.partial(jnp.dot, preferred_element_type=F32)
    yc = [dot(hc_ref[0, r, :], wc_ref[...]) for r in rows]
    ym = [dot(hm_ref[0, r, :], wm_ref[...]) for r in rows]
    ya = [dot(oa_ref[0, r, :], wa_ref[...]) for r in rows]
    mix = [(gates_ref[0, r, 0:d].astype(F32) * c + gates_ref[0, r, d:2 * d].astype(F32) * m
            + gates_ref[0, r, 2 * d:3 * d].astype(F32) * a).astype(BF16) for r, c, m, a in zip(rows, yc, ym, ya)]
    y = [dot(mx, wo_ref[...]) + bo_ref[...] for mx in mix]
    x1 = [_layer_norm(ALPHA * x_ref[0, r, :] + g1_ref[...] * yy, ln1g_ref[...], ln1b_ref[...])
          for r, yy in zip(rows, y)]
    u2 = [(xx * (1.0 + sc2_ref[...]) + sh2_ref[...]).astype(BF16) for xx in x1]
    acc = [jnp.zeros(xx.shape, F32) for xx in x1]
    for j in range(D_FF // FF_CHUNK):
        sl = slice(j * FF_CHUNK, (j + 1) * FF_CHUNK)
        hdn = [jnp.maximum(dot(u, w1_ref[:, sl]) + b1_ref[:, sl], 0.0) for u in u2]
        acc = [ac + dot((hd * hd).astype(BF16), w2_ref[sl, :]) for ac, hd in zip(acc, hdn)]
    for r, xx, ac in zip(rows, x1, acc):
        o_ref[0, r, :] = _layer_norm(ALPHA * xx + g2_ref[...] * (ac + b2_ref[...]), ln2g_ref[...], ln2b_ref[...])


def _merge_mlp(x, hc, hm, oa, gates, mod, l, mod_row, wts, tm):
    bsz, t, d = x.shape
    tok = lambda n: pl.BlockSpec((1, tm, n), lambda b, i: (b, i, 0))
    mods = [_mod_spec(l, k, mod_row) for k in (MOD_G1, MOD_SC2, MOD_SH2, MOD_G2)]
    return pl.pallas_call(
        _merge_mlp_kernel,
        name="merge_mlp",
        grid=(bsz, t // tm),
        in_specs=[tok(d), tok(CONV_DIM), tok(ML_DIM), tok(MLA_HEADS * V_HEAD), tok(3 * d)] + mods + [
            _layer_spec(l, (CONV_DIM, d)), _layer_spec(l, (ML_DIM, d)), _layer_spec(l, (MLA_HEADS * V_HEAD, d)),
            _layer_spec(l, (d, d)), _layer_spec(l, (1, d)), _layer_spec(l, (1, d)), _layer_spec(l, (1, d)),
            _layer_spec(l, (d, D_FF)), _layer_spec(l, (1, D_FF)), _layer_spec(l, (D_FF, d)), _layer_spec(l, (1, d)),
            _layer_spec(l, (1, d)), _layer_spec(l, (1, d))],
        out_specs=tok(d),
        out_shape=jax.ShapeDtypeStruct((bsz, t, d), F32),
        compiler_params=pltpu.CompilerParams(
            dimension_semantics=("parallel", "arbitrary"), vmem_limit_bytes=VMEM_LIMIT),
    )(x, hc, hm, oa, gates, mod, mod, mod, mod,
      wts["w_conv_out"], wts["w_ml_out"], wts["w_mla_out"], wts["w_out"], wts["b_out"], wts["ln1_g"], wts["ln1_b"],
      wts["w1"], wts["b1"], wts["w2"], wts["b2"], wts["ln2_g"], wts["ln2_b"])


def _rope_runs():
    runs = []
    for half in range(2):
        for a in range(2):
            start = a * 2 * ROPE_FREQ + half * ROPE_FREQ
            runs.append(slice(start, start + ROPE_FREQ))
    return runs


def _pack_weights(w_in, b_in, w_dw, b_dw, cn_g, cn_b, w_conv_out, ml_g, w_ml_out, qn_g, w_uq, kvn_g, w_ukv,
                  w_mla_out, w_out, b_out, ln1_g, ln1_b, w1, b1, w2, b2, ln2_g, ln2_b):
    nl = w_in.shape[0]

    def packed(z, dtype):
        a, qkv, o, g, cq, ckv, kr, gates = (z[..., _SRC[i]:_SRC[i + 1]].astype(dtype) for i in range(8))
        zeros = lambda n: jnp.zeros(z.shape[:-1] + (n,), dtype)
        kr_block = [zeros(QK_NOPE)] + [kr[..., s] for s in _rope_runs()] + [zeros(HEAD_PAD - QK_NOPE - QK_ROPE)]
        g_block = [g[..., i * ML_HEADS:(i + 1) * ML_HEADS] for i in (0, 2, 1, 3)] + [zeros(LANES - 4 * ML_HEADS)]
        return jnp.concatenate([a, qkv, o, cq, ckv] + kr_block + g_block + [gates], axis=-1)

    wq = w_uq.reshape(nl, Q_LORA, MLA_HEADS, QK_NOPE + QK_ROPE)
    wq_p = jnp.concatenate([wq[..., :QK_NOPE]] + [wq[..., QK_NOPE:][..., s] for s in _rope_runs()]
                           + [jnp.zeros(wq.shape[:-1] + (HEAD_PAD - QK_NOPE - QK_ROPE,), F32)], axis=-1)
    wkv = w_ukv.reshape(nl, KV_LORA, MLA_HEADS, QK_NOPE + V_HEAD)
    wkn_p = jnp.concatenate([wkv[..., :QK_NOPE], jnp.zeros(wkv.shape[:-1] + (HEAD_PAD - QK_NOPE,), F32)], axis=-1)
    wv_t = wkv[..., QK_NOPE:].reshape(nl, KV_LORA, MLA_HEADS * V_HEAD).transpose(0, 2, 1)
    ncg = CONV_DIM // LANES
    row = lambda z: z[:, None, :]
    return {
        "w_pack": packed(w_in, BF16), "b_pack": row(packed(b_in, F32)),
        "qn_g": row(qn_g), "w_uq": wq_p.reshape(nl, Q_LORA, MLA_HEADS * HEAD_PAD).astype(BF16),
        "kvn_g": row(kvn_g), "w_kn": wkn_p.reshape(nl, KV_LORA, MLA_HEADS * HEAD_PAD).astype(BF16),
        "w_v": wv_t.astype(BF16),
        "w_dw": jnp.pad(w_dw, ((0, 0), (0, 1), (0, 0))).reshape(nl, CONV_K + 1, ncg, LANES).transpose(0, 2, 1, 3),
        "b_dw": b_dw.reshape(nl, ncg, 1, LANES),
        "cn_g": row(cn_g), "cn_b": row(cn_b), "ml_g": row(ml_g),
        "w_conv_out": w_conv_out.astype(BF16), "w_ml_out": w_ml_out.astype(BF16),
        "w_mla_out": w_mla_out.astype(BF16), "w_out": w_out.astype(BF16), "b_out": row(b_out),
        "ln1_g": row(ln1_g), "ln1_b": row(ln1_b), "w1": w1.astype(BF16), "b1": row(b1),
        "w2": w2.astype(BF16), "b2": row(b2), "ln2_g": row(ln2_g), "ln2_b": row(ln2_b),
    }


def _rope_tables(n_tokens):
    rows = n_tokens // GRID_W
    rr, cc = np.meshgrid(np.arange(rows, dtype=np.float32), np.arange(GRID_W, dtype=np.float32), indexing="ij")
    inv = (np.float32(ROPE_THETA) ** (-np.arange(ROPE_FREQ, dtype=np.float32) / np.float32(ROPE_FREQ))).astype(np.float32)
    ang = np.stack([rr.reshape(-1), cc.reshape(-1)], -1)[..., None] * inv
    cos = np.cos(ang).astype(np.float32).reshape(n_tokens, ROPE_HALF)
    sin = np.sin(ang).astype(np.float32).reshape(n_tokens, ROPE_HALF)
    one = np.ones((n_tokens, QK_NOPE), np.float32)
    z_nope = np.zeros((n_tokens, QK_NOPE), np.float32)
    z_half = np.zeros((n_tokens, ROPE_HALF), np.float32)
    z_pad = np.zeros((n_tokens, HEAD_PAD - QK_NOPE - QK_ROPE), np.float32)
    tc = np.concatenate([one, cos, cos, z_pad], axis=1)
    tsm = np.concatenate([z_nope, -sin, z_half, z_pad], axis=1)
    tsp = np.concatenate([z_nope, z_half, sin, z_pad], axis=1)
    return tc, tsm, tsp


def _identity_tables(n_tokens):
    return (np.ones((n_tokens, HEAD_PAD), np.float32), np.zeros((n_tokens, HEAD_PAD), np.float32),
            np.zeros((n_tokens, HEAD_PAD), np.float32))


def kernel(x, c, ctx, c_ctx, w_mod, b_mod, w_in, b_in, w_dw, b_dw, conv_norm_g, conv_norm_b, w_conv_out,
           mlstm_norm_g, w_mlstm_out, q_norm_g, w_uq, kv_norm_g, w_ukv, w_mla_out, w_out, b_out,
           ln1_g, ln1_b, w1, b1, w2, b2, ln2_g, ln2_b):
    bsz, t_x, d = x.shape
    t_c = ctx.shape[1]
    n_rows = 24
    cc = jnp.concatenate([c, c_ctx[None, :], jnp.zeros((n_rows - bsz - 1, d), F32)], axis=0)
    mod = _modulation(cc, w_mod, b_mod).reshape(DEPTH, n_rows, N_MOD, 1, d)
    wts = _pack_weights(w_in, b_in, w_dw, b_dw, conv_norm_g, conv_norm_b, w_conv_out, mlstm_norm_g,
                        w_mlstm_out, q_norm_g, w_uq, kv_norm_g, w_ukv, w_mla_out, w_out, b_out,
                        ln1_g, ln1_b, w1, b1, w2, b2, ln2_g, ln2_b)
    rope_x = _rope_tables(t_x)
    rope_c = _identity_tables(t_c)
    ctx_row = bsz

    for l in range(DEPTH):
        with_ctx = l < DEPTH - 1
        px = _in_projection(x, mod, l, None, wts, rope_x, 512)
        pc = _in_projection(ctx, mod, l, ctx_row, wts, rope_c, 256, keys_only=not with_ctx)

        hm_c, hm_x = _mlstm_branch(pc, px, l, wts, with_ctx)
        oa_x = _attention(px["q"], [pc["k"], px["k"]], [pc["vta"], px["vta"]], 512)
        hc_x = _conv_branch(px["glu"], l, wts)
        x_new = _merge_mlp(x, hc_x, hm_x, oa_x, px["gates"], mod, l, None, wts, 512)
        if with_ctx:
            hc_c = _conv_branch(pc["glu"], l, wts)
            oa_c = _attention(pc["q"], [pc["k"]], [pc["vta"]], 256)
            ctx = _merge_mlp(ctx, hc_c, hm_c, oa_c, pc["gates"], mod, l, ctx_row, wts, 256)
        x = x_new
    return x
```

```python
import functools

import numpy as np
import jax
import jax.numpy as jnp
from jax import lax
from jax.experimental import pallas as pl
from jax.experimental.pallas import tpu as pltpu

F32 = jnp.float32
BF16 = jnp.bfloat16

D_MODEL = 1024
DEPTH = 2
GRID_W = 64
CONV_DIM = 512
CONV_K = 31
ML_HEADS = 4
ML_HEAD_DIM = 128
ML_DIM = ML_HEADS * ML_HEAD_DIM
ML_CHUNK = 256
MLA_HEADS = 8
QK_NOPE = 64
QK_ROPE = 32
V_HEAD = 64
Q_LORA = 768
KV_LORA = 256
ROPE_THETA = 10000.0
ROPE_FREQ = QK_ROPE // 4
MLA_SCALE = (QK_NOPE + QK_ROPE) ** -0.5
ML_SCALE = ML_HEAD_DIM ** -0.5
D_FF = 4 * D_MODEL
LN_EPS = 1e-5
ALPHA = (2 * DEPTH) ** 0.25
N_MOD = 6
MOD_SH1, MOD_SC1, MOD_G1, MOD_SH2, MOD_SC2, MOD_G2 = range(N_MOD)

LANES = 128
SUBLANES = 8
HEAD_PAD = LANES
ROPE_HALF = QK_ROPE // 2
ONES_ROWS = 16
ML_AUG = ML_HEAD_DIM + ONES_ROWS
ATT_AUG = V_HEAD + ONES_ROWS
Q_SCALE = MLA_SCALE * float(np.log2(np.e))
VMEM_LIMIT = 56 * 1024 * 1024

OFF_A = 0
OFF_QKV = OFF_A + 2 * CONV_DIM
OFF_O = OFF_QKV + 3 * ML_DIM
OFF_CQ = OFF_O + ML_DIM
OFF_CKV = OFF_CQ + Q_LORA
OFF_KR = OFF_CKV + KV_LORA
OFF_GIF = OFF_KR + HEAD_PAD
OFF_GATES = OFF_GIF + LANES
N_PACK = OFF_GATES + 3 * D_MODEL

_COLS = (2 * CONV_DIM, 3 * ML_DIM, ML_DIM, 4 * ML_HEADS, Q_LORA, KV_LORA, QK_ROPE, 3 * D_MODEL)
_SRC = tuple(int(s) for s in np.cumsum((0,) + _COLS))


def _layer_spec(l, shape):
    nd = len(shape)
    return pl.BlockSpec((None,) + tuple(shape), lambda *_: (l,) + (0,) * nd, pipeline_mode=pl.Buffered(1))


def _mod_spec(l, k, row=None):
    block = (None, None, None, 1, D_MODEL)
    if row is None:
        return pl.BlockSpec(block, lambda b, i: (l, b, k, 0, 0))
    return pl.BlockSpec(block, lambda b, i: (l, row, k, 0, 0))


def _sigmoid(z):
    return 1.0 / (1.0 + jnp.exp(-z))


def _log_sigmoid(z):
    return jnp.minimum(z, 0.0) - jnp.log1p(jnp.exp(-jnp.abs(z)))


def _layer_norm(z, g, b):
    mu = jnp.mean(z, axis=-1, keepdims=True)
    zc = z - mu
    var = jnp.mean(zc * zc, axis=-1, keepdims=True)
    return zc * lax.rsqrt(var + LN_EPS) * g + b


def _mod_kernel(c_ref, w_ref, b_ref, o_ref):
    c = c_ref[...]
    s = (c * _sigmoid(c)).astype(BF16)
    o_ref[0] = jnp.dot(s, w_ref[0].astype(BF16), preferred_element_type=F32) + b_ref[0]


def _modulation(cc, w_mod, b_mod):
    nl, d, n = w_mod.shape
    r = cc.shape[0]
    tn = 1024
    return pl.pallas_call(
        _mod_kernel,
        name="modulation",
        grid=(nl, n // tn),
        in_specs=[
            pl.BlockSpec((r, d), lambda l, j: (0, 0)),
            pl.BlockSpec((1, d, tn), lambda l, j: (l, 0, j)),
            pl.BlockSpec((1, 1, tn), lambda l, j: (l, 0, j)),
        ],
        out_specs=pl.BlockSpec((1, r, tn), lambda l, j: (l, 0, j)),
        out_shape=jax.ShapeDtypeStruct((nl, r, n), F32),
        compiler_params=pltpu.CompilerParams(
            dimension_semantics=("arbitrary", "arbitrary"), vmem_limit_bytes=VMEM_LIMIT),
    )(cc, w_mod, b_mod.reshape(nl, 1, n))


def _rope(z, tc, tsm, tsp):
    return z * tc + pltpu.roll(z, HEAD_PAD - ROPE_HALF, 1) * tsm + pltpu.roll(z, ROPE_HALF, 1) * tsp


def _inproj_kernel(x_ref, sc_ref, sh_ref, w_ref, b_ref, qng_ref, wuq_ref, kvg_ref, wkn_ref, wv_ref,
                   tc_ref, tsm_ref, tsp_ref, *out_refs, keys_only):
    if keys_only:
        qk_ref, vtm_ref, gt_ref, k_ref, vta_ref = out_refs
    else:
        glu_ref, qk_ref, vtm_ref, so_ref, gt_ref, gates_ref, q_ref, k_ref, vta_ref = out_refs
    u = (x_ref[0] * (1.0 + sc_ref[...]) + sh_ref[...]).astype(BF16)
    tm = u.shape[0]
    ones_rows = jnp.ones((ONES_ROWS, tm), BF16)

    def proj(off, n):
        return jnp.dot(u, w_ref[:, off:off + n], preferred_element_type=F32) + b_ref[:, off:off + n]

    if not keys_only:
        cq = proj(OFF_CQ, Q_LORA)
        nq = (cq * lax.rsqrt(jnp.mean(cq * cq, axis=-1, keepdims=True) + LN_EPS) * qng_ref[...]).astype(BF16)
    ckv = proj(OFF_CKV, KV_LORA)
    nkv = (ckv * lax.rsqrt(jnp.mean(ckv * ckv, axis=-1, keepdims=True) + LN_EPS) * kvg_ref[...]).astype(BF16)

    if not keys_only:
        a = proj(OFF_A, 2 * CONV_DIM)
        glu_ref[0] = (a[:, :CONV_DIM] * _sigmoid(a[:, CONV_DIM:])).astype(BF16)

    qk_ref[0, :, :ML_DIM] = proj(OFF_QKV, ML_DIM).astype(BF16)
    qk_ref[0, :, ML_DIM:] = (proj(OFF_QKV + ML_DIM, ML_DIM) * ML_SCALE).astype(BF16)
    vt = proj(OFF_QKV + 2 * ML_DIM, ML_DIM).T.astype(BF16)
    gt = proj(OFF_GIF, LANES).T
    for c in range(tm // ML_CHUNK):
        cs = slice(c * ML_CHUNK, (c + 1) * ML_CHUNK)
        gt_ref[0, c] = gt[:4 * ML_HEADS, cs]
        for h in range(ML_HEADS):
            vtm_ref[0, c, h * ML_AUG:h * ML_AUG + ML_HEAD_DIM, :] = vt[h * ML_HEAD_DIM:(h + 1) * ML_HEAD_DIM, cs]
            vtm_ref[0, c, h * ML_AUG + ML_HEAD_DIM:(h + 1) * ML_AUG, :] = ones_rows[:, cs]
    tc, tsm, tsp = tc_ref[...], tsm_ref[...], tsp_ref[...]
    if not keys_only:
        so_ref[0] = _sigmoid(proj(OFF_O, ML_DIM)).astype(BF16)
        for j in range(3):
            gates_ref[0, :, j * D_MODEL:(j + 1) * D_MODEL] = _sigmoid(
                proj(OFF_GATES + j * D_MODEL, D_MODEL)).astype(BF16)
        q = jnp.dot(nq, wuq_ref[...], preferred_element_type=F32)
        for h in range(MLA_HEADS):
            sl = slice(h * HEAD_PAD, (h + 1) * HEAD_PAD)
            q_ref[0, :, sl] = (_rope(q[:, sl], tc, tsm, tsp) * Q_SCALE).astype(BF16)

    kn = jnp.dot(nkv, wkn_ref[...], preferred_element_type=F32)
    kr = _rope(proj(OFF_KR, HEAD_PAD), tc, tsm, tsp)
    for h in range(MLA_HEADS):
        sl = slice(h * HEAD_PAD, (h + 1) * HEAD_PAD)
        k_ref[0, :, sl] = (kn[:, sl] + kr).astype(BF16)
    vta = lax.dot_general(wv_ref[...], nkv, (((1,), (1,)), ((), ())),
                          preferred_element_type=F32).astype(BF16)
    for h in range(MLA_HEADS):
        vta_ref[0, h * ATT_AUG:h * ATT_AUG + V_HEAD, :] = vta[h * V_HEAD:(h + 1) * V_HEAD, :]
        vta_ref[0, h * ATT_AUG + V_HEAD:(h + 1) * ATT_AUG, :] = ones_rows


def _in_projection(x, mod, l, mod_row, wts, tables, tm, keys_only=False):
    bsz, t, d = x.shape
    tok = lambda n: pl.BlockSpec((1, tm, n), lambda b, i: (b, i, 0))
    tab = pl.BlockSpec((tm, HEAD_PAD), lambda b, i: (i, 0))
    hq = MLA_HEADS * HEAD_PAD
    nc, cpt = t // ML_CHUNK, tm // ML_CHUNK
    seq = lambda n: (jax.ShapeDtypeStruct((bsz, t, n), BF16), tok(n))
    outputs = {
        "glu": seq(CONV_DIM),
        "qk": seq(2 * ML_DIM),
        "vtm": (jax.ShapeDtypeStruct((bsz, nc, ML_HEADS * ML_AUG, ML_CHUNK), BF16),
                pl.BlockSpec((1, cpt, ML_HEADS * ML_AUG, ML_CHUNK), lambda b, i: (b, i, 0, 0))),
        "so": seq(ML_DIM),
        "gt": (jax.ShapeDtypeStruct((bsz, nc, 4 * ML_HEADS, ML_CHUNK), F32),
               pl.BlockSpec((1, cpt, 4 * ML_HEADS, ML_CHUNK), lambda b, i: (b, i, 0, 0))),
        "gates": seq(3 * D_MODEL),
        "q": seq(hq),
        "k": seq(hq),
        "vta": (jax.ShapeDtypeStruct((bsz, MLA_HEADS * ATT_AUG, t), BF16),
                pl.BlockSpec((1, MLA_HEADS * ATT_AUG, tm), lambda b, i: (b, 0, i))),
    }
    if keys_only:
        outputs = {n: outputs[n] for n in ("qk", "vtm", "gt", "k", "vta")}
    out_shape = [v[0] for v in outputs.values()]
    out_specs = [v[1] for v in outputs.values()]
    res = pl.pallas_call(
        functools.partial(_inproj_kernel, keys_only=keys_only),
        name="in_projection",
        grid=(bsz, t // tm),
        in_specs=[tok(d), _mod_spec(l, MOD_SC1, mod_row), _mod_spec(l, MOD_SH1, mod_row),
                  _layer_spec(l, (d, N_PACK)), _layer_spec(l, (1, N_PACK)),
                  _layer_spec(l, (1, Q_LORA)), _layer_spec(l, (Q_LORA, hq)),
                  _layer_spec(l, (1, KV_LORA)), _layer_spec(l, (KV_LORA, hq)),
                  _layer_spec(l, (MLA_HEADS * V_HEAD, KV_LORA)),
                  tab, tab, tab],
        out_specs=out_specs,
        out_shape=out_shape,
        compiler_params=pltpu.CompilerParams(
            dimension_semantics=("parallel", "arbitrary"), vmem_limit_bytes=VMEM_LIMIT),
    )(x, mod, mod, wts["w_pack"], wts["b_pack"], wts["qn_g"], wts["w_uq"], wts["kvn_g"], wts["w_kn"], wts["w_v"],
      *tables)
    return dict(zip(outputs, res))


CONV_HALO = 16
CONV_TILE = 128


def _conv_kernel(h_ref, wdw_ref, bdw_ref, g_ref, b_ref, o_ref, hp, stage, cv, shifted, *, t):
    nt = t // CONV_TILE
    ncg = CONV_DIM // LANES
    first = CONV_HALO - CONV_K // 2
    span = CONV_TILE + (CONV_K // SUBLANES) * SUBLANES
    for cg in range(ncg):
        hp[cg, 0:CONV_HALO, :] = jnp.zeros((CONV_HALO, LANES), F32)
        hp[cg, t + CONV_HALO:t + 2 * CONV_HALO, :] = jnp.zeros((CONV_HALO, LANES), F32)

    def fill(i, carry):
        r0 = pl.multiple_of(i * CONV_TILE, CONV_TILE)
        hx = h_ref[0, pl.ds(r0, CONV_TILE), :].astype(F32)
        for cg in range(ncg):
            hp[cg, pl.ds(r0 + CONV_HALO, CONV_TILE), :] = hx[:, cg * LANES:(cg + 1) * LANES]
        return carry

    lax.fori_loop(0, nt, fill, 0)

    def tile(i, carry):
        r0 = pl.multiple_of(i * CONV_TILE, CONV_TILE)

        def group(cg, inner):
            stage[...] = hp[cg, pl.ds(r0, CONV_TILE + 2 * CONV_HALO), :]
            acc = jnp.zeros((CONV_TILE, LANES), F32)
            for r in range(SUBLANES):
                shifted[r] = stage[r:r + span, :]
                for k in range(CONV_K):
                    if (first + k) % SUBLANES == r:
                        a = first + k - r
                        acc = acc + shifted[r, a:a + CONV_TILE, :] * wdw_ref[cg, k:k + 1, :]
            cv[cg] = acc + bdw_ref[cg]
            return inner

        lax.fori_loop(0, ncg, group, 0)
        z = jnp.concatenate([cv[cg] for cg in range(ncg)], axis=1)
        z = _layer_norm(z, g_ref[...], b_ref[...])
        o_ref[0, pl.ds(r0, CONV_TILE), :] = (z * _sigmoid(z)).astype(BF16)
        return carry

    lax.fori_loop(0, nt, tile, 0)


def _conv_branch(glu, l, wts):
    bsz, t, _ = glu.shape
    ncg = CONV_DIM // LANES
    return pl.pallas_call(
        functools.partial(_conv_kernel, t=t),
        name="conv",
        grid=(bsz,),
        in_specs=[pl.BlockSpec((1, t, CONV_DIM), lambda b: (b, 0, 0)),
                  _layer_spec(l, (ncg, CONV_K + 1, LANES)), _layer_spec(l, (ncg, 1, LANES)),
                  _layer_spec(l, (1, CONV_DIM)), _layer_spec(l, (1, CONV_DIM))],
        out_specs=pl.BlockSpec((1, t, CONV_DIM), lambda b: (b, 0, 0)),
        out_shape=jax.ShapeDtypeStruct((bsz, t, CONV_DIM), BF16),
        scratch_shapes=[pltpu.VMEM((ncg, t + 2 * CONV_HALO, LANES), F32),
                        pltpu.VMEM((CONV_TILE + 2 * CONV_HALO, LANES), F32),
                        pltpu.VMEM((ncg, CONV_TILE, LANES), F32),
                        pltpu.VMEM((SUBLANES, CONV_TILE + (CONV_K // SUBLANES) * SUBLANES, LANES), F32)],
        compiler_params=pltpu.CompilerParams(
            dimension_semantics=("parallel",), vmem_limit_bytes=VMEM_LIMIT),
    )(glu, wts["w_dw"], wts["b_dw"], wts["cn_g"], wts["cn_b"])


N_CHAIN = 2 * ML_HEADS


def _mlstm_kernel(*refs, nc_c, nc_x, ctx_out):
    refs = list(refs)
    qk_c_ref, vt_c_ref, gt_c_ref = refs[:3]
    so_c_ref = refs[3] if ctx_out else None
    qk_x_ref, vt_x_ref, gt_x_ref, so_x_ref, mlg_ref = refs[3 + ctx_out:8 + ctx_out]
    outs = refs[8 + ctx_out:9 + 2 * ctx_out]
    hn_c_ref, hn_x_ref = (outs[0] if ctx_out else None), outs[-1]
    hf_c, hb_c, hf_x, hb_x, c_scr, m_scr = refs[9 + 2 * ctx_out:]
    L = ML_CHUNK
    dh = ML_HEAD_DIM
    row = lax.broadcasted_iota(jnp.int32, (L, L), 0)
    col = lax.broadcasted_iota(jnp.int32, (L, L), 1)
    upper = row <= col
    lower = row >= col
    tri2 = jnp.concatenate([upper.astype(F32), lower.astype(F32)], axis=1).astype(BF16)
    is_fwd = lax.broadcasted_iota(jnp.int32, (N_CHAIN, L), 0) < ML_HEADS

    c_scr[...] = jnp.zeros(c_scr.shape, F32)
    m_scr[...] = jnp.zeros(m_scr.shape, F32)

    def iteration(qk_ref, vt_ref, gt_ref, hf, hb, cf, cb):
        gf = gt_ref[0, cf]
        gb = gt_ref[0, cb]
        li = jnp.where(is_fwd, gf[:N_CHAIN], gb[:N_CHAIN])
        lf = _log_sigmoid(jnp.where(is_fwd, gf[N_CHAIN:], gb[N_CHAIN:]))
        hi = lf.astype(BF16).astype(F32)
        mid = (lf - hi).astype(BF16).astype(F32)
        lo = lf - hi - mid
        parts = jnp.dot(jnp.concatenate([hi, mid, lo], axis=0).astype(BF16), tri2, preferred_element_type=F32)
        cum2 = parts[:N_CHAIN] + parts[N_CHAIN:2 * N_CHAIN] + parts[2 * N_CHAIN:]
        bc = jnp.where(is_fwd, cum2[:, :L], cum2[:, L:])
        r = li - bc
        b_last = jnp.sum(lf, axis=1, keepdims=True)
        m = m_scr[:, 0:1]
        w_src = b_last + r
        m_new = jnp.maximum(b_last + m, jnp.max(w_src, axis=1, keepdims=True))
        a_state = jnp.exp(b_last + m - m_new)
        w = jnp.exp(w_src - m_new)
        inter = bc + m
        r_cols = jnp.concatenate([r, jnp.zeros((L - N_CHAIN, L), F32)], axis=0).T
        m_scr[...] = jnp.broadcast_to(m_new, m_scr.shape)
        early = []
        for j in range(N_CHAIN):
            d, h = divmod(j, ML_HEADS)
            cc = cf if d == 0 else cb
            r0 = pl.multiple_of(cc * L, L)
            q = qk_ref[0, pl.ds(r0, L), h * dh:(h + 1) * dh]
            k = qk_ref[0, pl.ds(r0, L), ML_DIM + h * dh:ML_DIM + (h + 1) * dh]
            vt = vt_ref[0, cc, h * ML_AUG:(h + 1) * ML_AUG, :]
            ct = c_scr[j]
            both = lax.dot_general(jnp.concatenate([k, ct.astype(BF16)], axis=0), q, (((1,), (1,)), ((), ())),
                                   preferred_element_type=F32)
            early.append((both, vt, k, ct))
        for j in range(N_CHAIN):
            _, vt, k, ct = early[j]
            wv = (vt.astype(F32) * w[j:j + 1, :]).astype(BF16)
            c_scr[j] = a_state[j:j + 1, :] * ct + jnp.dot(wv, k, preferred_element_type=F32)
        for j in range(N_CHAIN):
            d, h = divmod(j, ML_HEADS)
            cc = cf if d == 0 else cb
            mask = upper if d == 0 else lower
            both, vt, _, _ = early[j]
            log_d = jnp.where(mask, bc[j:j + 1, :] + r_cols[:, j:j + 1], -jnp.inf)
            m_row = jnp.maximum(inter[j:j + 1, :], jnp.max(log_d, axis=0, keepdims=True))
            a_inter = jnp.exp(inter[j:j + 1, :] - m_row)
            dmat = jnp.exp(log_d - m_row)
            pt = (both[:L] * dmat).astype(BF16)
            num_aug = a_inter * both[L:] + jnp.dot(vt, pt, preferred_element_type=F32)
            den = num_aug[dh:dh + 1, :]
            hs = hf if d == 0 else hb
            hs[cc, h * dh:(h + 1) * dh, :] = num_aug[:dh] / jnp.maximum(jnp.abs(den), jnp.exp(-m_row))

    def scan(qk_ref, vt_ref, gt_ref, hf, hb, nc):
        def body(i, carry):
            iteration(qk_ref, vt_ref, gt_ref, hf, hb, i, nc - 1 - i)
            return carry

        lax.fori_loop(0, nc, body, 0, unroll=min(2, nc))

    scan(qk_c_ref, vt_c_ref, gt_c_ref, hf_c, hb_c, nc_c)
    scan(qk_x_ref, vt_x_ref, gt_x_ref, hf_x, hb_x, nc_x)

    def finish(hf, hb, so_ref, out_ref, nc):
        def body(c, carry):
            r0 = pl.multiple_of(c * L, L)
            hsum = hf[c] + hb[c]
            for h in range(ML_HEADS):
                sl = slice(h * dh, (h + 1) * dh)
                z = hsum[sl, :]
                mu = jnp.mean(z, axis=0, keepdims=True)
                zc = z - mu
                var = jnp.mean(zc * zc, axis=0, keepdims=True)
                hn = (zc * lax.rsqrt(var + LN_EPS)).T * mlg_ref[:, sl]
                out_ref[0, pl.ds(r0, L), sl] = (so_ref[0, pl.ds(r0, L), sl].astype(F32) * hn).astype(BF16)
            return carry

        lax.fori_loop(0, nc, body, 0)

    if ctx_out:
        finish(hf_c, hb_c, so_c_ref, hn_c_ref, nc_c)
    finish(hf_x, hb_x, so_x_ref, hn_x_ref, nc_x)


def _mlstm_branch(pc, px, l, wts, ctx_out):
    bsz, t_c, _ = pc["qk"].shape
    t_x = px["qk"].shape[1]
    nc_c, nc_x = t_c // ML_CHUNK, t_x // ML_CHUNK
    seq = lambda t, n: pl.BlockSpec((1, t, n), lambda b: (b, 0, 0))
    chunked = lambda nc, n: pl.BlockSpec((1, nc, n, ML_CHUNK), lambda b: (b, 0, 0, 0))
    ins = lambda t, nc, so: [seq(t, 2 * ML_DIM), chunked(nc, ML_HEADS * ML_AUG), chunked(nc, 4 * ML_HEADS)] + (
        [seq(t, ML_DIM)] if so else [])
    args = lambda p, so: [p["qk"], p["vtm"], p["gt"]] + ([p["so"]] if so else [])
    hbuf = lambda nc: pltpu.VMEM((nc, ML_DIM, ML_CHUNK), F32)
    out_t = ([t_c] if ctx_out else []) + [t_x]
    res = pl.pallas_call(
        functools.partial(_mlstm_kernel, nc_c=nc_c, nc_x=nc_x, ctx_out=ctx_out),
        name="mlstm",
        grid=(bsz,),
        in_specs=ins(t_c, nc_c, ctx_out) + ins(t_x, nc_x, True) + [_layer_spec(l, (1, ML_DIM))],
        out_specs=[seq(t, ML_DIM) for t in out_t],
        out_shape=[jax.ShapeDtypeStruct((bsz, t, ML_DIM), BF16) for t in out_t],
        scratch_shapes=[hbuf(nc_c), hbuf(nc_c), hbuf(nc_x), hbuf(nc_x),
                        pltpu.VMEM((N_CHAIN, ML_AUG, ML_HEAD_DIM), F32),
                        pltpu.VMEM((N_CHAIN, LANES), F32)],
        compiler_params=pltpu.CompilerParams(
            dimension_semantics=("parallel",), vmem_limit_bytes=VMEM_LIMIT),
    )(*args(pc, ctx_out), *args(px, True), wts["ml_g"])
    return (res[0], res[1]) if ctx_out else (None, res[0])


ATT_KEYS = 256
ATT_QG = 256
ATT_LOOKAHEAD = 8


def _attn_kernel(*refs, n_seg):
    q_ref = refs[0]
    k_refs = refs[1:1 + n_seg]
    vt_refs = refs[1 + n_seg:1 + 2 * n_seg]
    o_ref = refs[1 + 2 * n_seg]
    qg = min(ATT_QG, q_ref.shape[1])
    n_qg = q_ref.shape[1] // qg
    chunks = [(k_ref, vt_ref, slice(j, j + min(ATT_KEYS, k_ref.shape[1])))
              for k_ref, vt_ref in zip(k_refs, vt_refs)
              for j in range(0, k_ref.shape[1], min(ATT_KEYS, k_ref.shape[1]))]
    steps = [(ci, h, g) for ci in range(len(chunks)) for h in range(MLA_HEADS) for g in range(n_qg)]

    def scores(ci, h, g):
        k_ref, _, ks = chunks[ci]
        hs = slice(h * HEAD_PAD, (h + 1) * HEAD_PAD)
        return lax.dot_general(k_ref[0, ks, hs], q_ref[0, g * qg:(g + 1) * qg, hs],
                               (((1,), (1,)), ((), ())), preferred_element_type=F32)

    state = {}
    pending = {}
    for idx in range(len(steps) + ATT_LOOKAHEAD):
        if idx < len(steps):
            pending[idx] = scores(*steps[idx])
        if idx < ATT_LOOKAHEAD:
            continue
        ci, h, g = steps[idx - ATT_LOOKAHEAD]
        s = pending.pop(idx - ATT_LOOKAHEAD)
        _, vt_ref, ks = chunks[ci]
        vtj = vt_ref[0, h * ATT_AUG:(h + 1) * ATT_AUG, ks]
        cm = jnp.max(s, axis=0, keepdims=True)
        if ci == 0:
            m_new = cm
            acc = jnp.dot(vtj, jnp.exp2(s - m_new).astype(BF16), preferred_element_type=F32)
        else:
            m_old, acc_old = state[h, g]
            m_new = jnp.maximum(m_old, cm)
            acc = acc_old * jnp.exp2(m_old - m_new) + jnp.dot(
                vtj, jnp.exp2(s - m_new).astype(BF16), preferred_element_type=F32)
        state[h, g] = (m_new, acc)
    outs = []
    for h in range(MLA_HEADS):
        accs = [state[h, g][1] for g in range(n_qg)]
        outs.append(jnp.concatenate([a[:V_HEAD] * (1.0 / a[V_HEAD:V_HEAD + 1]) for a in accs], axis=1))
    o_ref[0] = jnp.concatenate(outs, axis=0).T.astype(BF16)


def _attention(q, ks, vts, tq):
    bsz, t, hq = q.shape
    n_seg = len(ks)
    hv = MLA_HEADS * V_HEAD
    kspec = [pl.BlockSpec((1, k.shape[1], hq), lambda b, i: (b, 0, 0)) for k in ks]
    vspec = [pl.BlockSpec((1, MLA_HEADS * ATT_AUG, vt.shape[2]), lambda b, i: (b, 0, 0)) for vt in vts]
    return pl.pallas_call(
        functools.partial(_attn_kernel, n_seg=n_seg),
        name="attention",
        grid=(bsz, t // tq),
        in_specs=[pl.BlockSpec((1, tq, hq), lambda b, i: (b, i, 0))] + kspec + vspec,
        out_specs=pl.BlockSpec((1, tq, hv), lambda b, i: (b, i, 0)),
        out_shape=jax.ShapeDtypeStruct((bsz, t, hv), BF16),
        compiler_params=pltpu.CompilerParams(
            dimension_semantics=("parallel", "arbitrary"), vmem_limit_bytes=VMEM_LIMIT),
    )(q, *ks, *vts)


FF_CHUNK = 1024
MERGE_GROUPS = 2


def _merge_mlp_kernel(x_ref, hc_ref, hm_ref, oa_ref, gates_ref, g1_ref, sc2_ref, sh2_ref, g2_ref,
                      wc_ref, wm_ref, wa_ref, wo_ref, bo_ref, ln1g_ref, ln1b_ref,
                      w1_ref, b1_ref, w2_ref, b2_ref, ln2g_ref, ln2b_ref, o_ref):
    d = D_MODEL
    tm = x_ref.shape[1]
    rows = [slice(g * tm // MERGE_GROUPS, (g + 1) * tm // MERGE_GROUPS) for g in range(MERGE_GROUPS)]
    dot = functools.partial(jnp.dot, preferred_element_type=F32)
    yc = [dot(hc_ref[0, r, :], wc_ref[...]) for r in rows]
    ym = [dot(hm_ref[0, r, :], wm_ref[...]) for r in rows]
    ya = [dot(oa_ref[0, r, :], wa_ref[...]) for r in rows]
    mix = [(gates_ref[0, r, 0:d].astype(F32) * c + gates_ref[0, r, d:2 * d].astype(F32) * m
            + gates_ref[0, r, 2 * d:3 * d].astype(F32) * a).astype(BF16) for r, c, m, a in zip(rows, yc, ym, ya)]
    y = [dot(mx, wo_ref[...]) + bo_ref[...] for mx in mix]
    x1 = [_layer_norm(ALPHA * x_ref[0, r, :] + g1_ref[...] * yy, ln1g_ref[...], ln1b_ref[...])
          for r, yy in zip(rows, y)]
    u2 = [(xx * (1.0 + sc2_ref[...]) + sh2_ref[...]).astype(BF16) for xx in x1]
    acc = [jnp.zeros(xx.shape, F32) for xx in x1]
    for j in range(D_FF // FF_CHUNK):
        sl = slice(j * FF_CHUNK, (j + 1) * FF_CHUNK)
        hdn = [jnp.maximum(dot(u, w1_ref[:, sl]) + b1_ref[:, sl], 0.0) for u in u2]
        acc = [ac + dot((hd * hd).astype(BF16), w2_ref[sl, :]) for ac, hd in zip(acc, hdn)]
    for r, xx, ac in zip(rows, x1, acc):
        o_ref[0, r, :] = _layer_norm(ALPHA * xx + g2_ref[...] * (ac + b2_ref[...]), ln2g_ref[...], ln2b_ref[...])


def _merge_mlp(x, hc, hm, oa, gates, mod, l, mod_row, wts, tm):
    bsz, t, d = x.shape
    tok = lambda n: pl.BlockSpec((1, tm, n), lambda b, i: (b, i, 0))
    mods = [_mod_spec(l, k, mod_row) for k in (MOD_G1, MOD_SC2, MOD_SH2, MOD_G2)]
    return pl.pallas_call(
        _merge_mlp_kernel,
        name="merge_mlp",
        grid=(bsz, t // tm),
        in_specs=[tok(d), tok(CONV_DIM), tok(ML_DIM), tok(MLA_HEADS * V_HEAD), tok(3 * d)] + mods + [
            _layer_spec(l, (CONV_DIM, d)), _layer_spec(l, (ML_DIM, d)), _layer_spec(l, (MLA_HEADS * V_HEAD, d)),
            _layer_spec(l, (d, d)), _layer_spec(l, (1, d)), _layer_spec(l, (1, d)), _layer_spec(l, (1, d)),
            _layer_spec(l, (d, D_FF)), _layer_spec(l, (1, D_FF)), _layer_spec(l, (D_FF, d)), _layer_spec(l, (1, d)),
            _layer_spec(l, (1, d)), _layer_spec(l, (1, d))],
        out_specs=tok(d),
        out_shape=jax.ShapeDtypeStruct((bsz, t, d), F32),
        compiler_params=pltpu.CompilerParams(
            dimension_semantics=("parallel", "arbitrary"), vmem_limit_bytes=VMEM_LIMIT),
    )(x, hc, hm, oa, gates, mod, mod, mod, mod,
      wts["w_conv_out"], wts["w_ml_out"], wts["w_mla_out"], wts["w_out"], wts["b_out"], wts["ln1_g"], wts["ln1_b"],
      wts["w1"], wts["b1"], wts["w2"], wts["b2"], wts["ln2_g"], wts["ln2_b"])


def _rope_runs():
    runs = []
    for half in range(2):
        for a in range(2):
            start = a * 2 * ROPE_FREQ + half * ROPE_FREQ
            runs.append(slice(start, start + ROPE_FREQ))
    return runs


def _pack_weights(w_in, b_in, w_dw, b_dw, cn_g, cn_b, w_conv_out, ml_g, w_ml_out, qn_g, w_uq, kvn_g, w_ukv,
                  w_mla_out, w_out, b_out, ln1_g, ln1_b, w1, b1, w2, b2, ln2_g, ln2_b):
    nl = w_in.shape[0]

    def packed(z, dtype):
        a, qkv, o, g, cq, ckv, kr, gates = (z[..., _SRC[i]:_SRC[i + 1]].astype(dtype) for i in range(8))
        zeros = lambda n: jnp.zeros(z.shape[:-1] + (n,), dtype)
        kr_block = [zeros(QK_NOPE)] + [kr[..., s] for s in _rope_runs()] + [zeros(HEAD_PAD - QK_NOPE - QK_ROPE)]
        g_block = [g[..., i * ML_HEADS:(i + 1) * ML_HEADS] for i in (0, 2, 1, 3)] + [zeros(LANES - 4 * ML_HEADS)]
        return jnp.concatenate([a, qkv, o, cq, ckv] + kr_block + g_block + [gates], axis=-1)

    wq = w_uq.reshape(nl, Q_LORA, MLA_HEADS, QK_NOPE + QK_ROPE)
    wq_p = jnp.concatenate([wq[..., :QK_NOPE]] + [wq[..., QK_NOPE:][..., s] for s in _rope_runs()]
                           + [jnp.zeros(wq.shape[:-1] + (HEAD_PAD - QK_NOPE - QK_ROPE,), F32)], axis=-1)
    wkv = w_ukv.reshape(nl, KV_LORA, MLA_HEADS, QK_NOPE + V_HEAD)
    wkn_p = jnp.concatenate([wkv[..., :QK_NOPE], jnp.zeros(wkv.shape[:-1] + (HEAD_PAD - QK_NOPE,), F32)], axis=-1)
    wv_t = wkv[..., QK_NOPE:].reshape(nl, KV_LORA, MLA_HEADS * V_HEAD).transpose(0, 2, 1)
    ncg = CONV_DIM // LANES
    row = lambda z: z[:, None, :]
    return {
        "w_pack": packed(w_in, BF16), "b_pack": row(packed(b_in, F32)),
        "qn_g": row(qn_g), "w_uq": wq_p.reshape(nl, Q_LORA, MLA_HEADS * HEAD_PAD).astype(BF16),
        "kvn_g": row(kvn_g), "w_kn": wkn_p.reshape(nl, KV_LORA, MLA_HEADS * HEAD_PAD).astype(BF16),
        "w_v": wv_t.astype(BF16),
        "w_dw": jnp.pad(w_dw, ((0, 0), (0, 1), (0, 0))).reshape(nl, CONV_K + 1, ncg, LANES).transpose(0, 2, 1, 3),
        "b_dw": b_dw.reshape(nl, ncg, 1, LANES),
        "cn_g": row(cn_g), "cn_b": row(cn_b), "ml_g": row(ml_g),
        "w_conv_out": w_conv_out.astype(BF16), "w_ml_out": w_ml_out.astype(BF16),
        "w_mla_out": w_mla_out.astype(BF16), "w_out": w_out.astype(BF16), "b_out": row(b_out),
        "ln1_g": row(ln1_g), "ln1_b": row(ln1_b), "w1": w1.astype(BF16), "b1": row(b1),
        "w2": w2.astype(BF16), "b2": row(b2), "ln2_g": row(ln2_g), "ln2_b": row(ln2_b),
    }


def _rope_tables(n_tokens):
    rows = n_tokens // GRID_W
    rr, cc = np.meshgrid(np.arange(rows, dtype=np.float32), np.arange(GRID_W, dtype=np.float32), indexing="ij")
    inv = (np.float32(ROPE_THETA) ** (-np.arange(ROPE_FREQ, dtype=np.float32) / np.float32(ROPE_FREQ))).astype(np.float32)
    ang = np.stack([rr.reshape(-1), cc.reshape(-1)], -1)[..., None] * inv
    cos = np.cos(ang).astype(np.float32).reshape(n_tokens, ROPE_HALF)
    sin = np.sin(ang).astype(np.float32).reshape(n_tokens, ROPE_HALF)
    one = np.ones((n_tokens, QK_NOPE), np.float32)
    z_nope = np.zeros((n_tokens, QK_NOPE), np.float32)
    z_half = np.zeros((n_tokens, ROPE_HALF), np.float32)
    z_pad = np.zeros((n_tokens, HEAD_PAD - QK_NOPE - QK_ROPE), np.float32)
    tc = np.concatenate([one, cos, cos, z_pad], axis=1)
    tsm = np.concatenate([z_nope, -sin, z_half, z_pad], axis=1)
    tsp = np.concatenate([z_nope, z_half, sin, z_pad], axis=1)
    return tc, tsm, tsp


def _identity_tables(n_tokens):
    return (np.ones((n_tokens, HEAD_PAD), np.float32), np.zeros((n_tokens, HEAD_PAD), np.float32),
            np.zeros((n_tokens, HEAD_PAD), np.float32))


def kernel(x, c, ctx, c_ctx, w_mod, b_mod, w_in, b_in, w_dw, b_dw, conv_norm_g, conv_norm_b, w_conv_out,
           mlstm_norm_g, w_mlstm_out, q_norm_g, w_uq, kv_norm_g, w_ukv, w_mla_out, w_out, b_out,
           ln1_g, ln1_b, w1, b1, w2, b2, ln2_g, ln2_b):
    bsz, t_x, d = x.shape
    t_c = ctx.shape[1]
    n_rows = 24
    cc = jnp.concatenate([c, c_ctx[None, :], jnp.zeros((n_rows - bsz - 1, d), F32)], axis=0)
    mod = _modulation(cc, w_mod, b_mod).reshape(DEPTH, n_rows, N_MOD, 1, d)
    wts = _pack_weights(w_in, b_in, w_dw, b_dw, conv_norm_g, conv_norm_b, w_conv_out, mlstm_norm_g,
                        w_mlstm_out, q_norm_g, w_uq, kv_norm_g, w_ukv, w_mla_out, w_out, b_out,
                        ln1_g, ln1_b, w1, b1, w2, b2, ln2_g, ln2_b)
    rope_x = _rope_tables(t_x)
    rope_c = _identity_tables(t_c)
    ctx_row = bsz

    for l in range(DEPTH):
        with_ctx = l < DEPTH - 1
        px = _in_projection(x, mod, l, None, wts, rope_x, 512)
        pc = _in_projection(ctx, mod, l, ctx_row, wts, rope_c, 256, keys_only=not with_ctx)

        hm_c, hm_x = _mlstm_branch(pc, px, l, wts, with_ctx)
        oa_x = _attention(px["q"], [pc["k"], px["k"]], [pc["vta"], px["vta"]], 512)
        hc_x = _conv_branch(px["glu"], l, wts)
        x_new = _merge_mlp(x, hc_x, hm_x, oa_x, px["gates"], mod, l, None, wts, 512)
        if with_ctx:
            hc_c = _conv_branch(pc["glu"], l, wts)
            oa_c = _attention(pc["q"], [pc["k"]], [pc["vta"]], 256)
            ctx = _merge_mlp(ctx, hc_c, hm_c, oa_c, pc["gates"], mod, l, ctx_row, wts, 256)
        x = x_new
    return x
```

```python
import functools

import numpy as np
import jax
import jax.numpy as jnp
from jax import lax
from jax.experimental import pallas as pl
from jax.experimental.pallas import tpu as pltpu

F32 = jnp.float32
BF16 = jnp.bfloat16

D_MODEL = 1024
DEPTH = 2
GRID_W = 64
CONV_DIM = 512
CONV_K = 31
ML_HEADS = 4
ML_HEAD_DIM = 128
ML_DIM = ML_HEADS * ML_HEAD_DIM
ML_CHUNK = 256
MLA_HEADS = 8
QK_NOPE = 64
QK_ROPE = 32
V_HEAD = 64
Q_LORA = 768
KV_LORA = 256
ROPE_THETA = 10000.0
ROPE_FREQ = QK_ROPE // 4
MLA_SCALE = (QK_NOPE + QK_ROPE) ** -0.5
ML_SCALE = ML_HEAD_DIM ** -0.5
D_FF = 4 * D_MODEL
LN_EPS = 1e-5
ALPHA = (2 * DEPTH) ** 0.25
N_MOD = 6
MOD_SH1, MOD_SC1, MOD_G1, MOD_SH2, MOD_SC2, MOD_G2 = range(N_MOD)

LANES = 128
SUBLANES = 8
HEAD_PAD = LANES
ROPE_HALF = QK_ROPE // 2
ONES_ROWS = 16
ML_AUG = ML_HEAD_DIM + ONES_ROWS
ATT_AUG = V_HEAD + ONES_ROWS
Q_SCALE = MLA_SCALE * float(np.log2(np.e))
VMEM_LIMIT = 56 * 1024 * 1024

OFF_A = 0
OFF_QKV = OFF_A + 2 * CONV_DIM
OFF_O = OFF_QKV + 3 * ML_DIM
OFF_CQ = OFF_O + ML_DIM
OFF_CKV = OFF_CQ + Q_LORA
OFF_KR = OFF_CKV + KV_LORA
OFF_GIF = OFF_KR + HEAD_PAD
OFF_GATES = OFF_GIF + LANES
N_PACK = OFF_GATES + 3 * D_MODEL

_COLS = (2 * CONV_DIM, 3 * ML_DIM, ML_DIM, 4 * ML_HEADS, Q_LORA, KV_LORA, QK_ROPE, 3 * D_MODEL)
_SRC = tuple(int(s) for s in np.cumsum((0,) + _COLS))


MAX_TOKEN_TILE = 512


def _token_tile(t):
    tm = min(MAX_TOKEN_TILE, t)
    assert t % tm == 0 and tm % ML_CHUNK == 0
    return tm


def _layer_spec(l, shape):
    nd = len(shape)
    return pl.BlockSpec((None,) + tuple(shape), lambda *_: (l,) + (0,) * nd, pipeline_mode=pl.Buffered(1))


def _mod_spec(l, k, row=None):
    block = (None, None, None, 1, D_MODEL)
    if row is None:
        return pl.BlockSpec(block, lambda b, i: (l, b, k, 0, 0))
    return pl.BlockSpec(block, lambda b, i: (l, row, k, 0, 0))


def _sigmoid(z):
    return 1.0 / (1.0 + jnp.exp(-z))


def _log_sigmoid(z):
    return jnp.minimum(z, 0.0) - jnp.log1p(jnp.exp(-jnp.abs(z)))


def _layer_norm(z, g, b):
    mu = jnp.mean(z, axis=-1, keepdims=True)
    zc = z - mu
    var = jnp.mean(zc * zc, axis=-1, keepdims=True)
    return zc * lax.rsqrt(var + LN_EPS) * g + b


def _mod_kernel(c_ref, w_ref, b_ref, o_ref):
    c = c_ref[...]
    s = (c * _sigmoid(c)).astype(BF16)
    o_ref[0] = jnp.dot(s, w_ref[0].astype(BF16), preferred_element_type=F32) + b_ref[0]


def _modulation(cc, w_mod, b_mod):
    nl, d, n = w_mod.shape
    r = cc.shape[0]
    tn = 1024
    return pl.pallas_call(
        _mod_kernel,
        name="modulation",
        grid=(nl, n // tn),
        in_specs=[
            pl.BlockSpec((r, d), lambda l, j: (0, 0)),
            pl.BlockSpec((1, d, tn), lambda l, j: (l, 0, j)),
            pl.BlockSpec((1, 1, tn), lambda l, j: (l, 0, j)),
        ],
        out_specs=pl.BlockSpec((1, r, tn), lambda l, j: (l, 0, j)),
        out_shape=jax.ShapeDtypeStruct((nl, r, n), F32),
        compiler_params=pltpu.CompilerParams(
            dimension_semantics=("arbitrary", "arbitrary"), vmem_limit_bytes=VMEM_LIMIT),
    )(cc, w_mod, b_mod.reshape(nl, 1, n))


def _rope(z, tc, tsm, tsp):
    return z * tc + pltpu.roll(z, HEAD_PAD - ROPE_HALF, 1) * tsm + pltpu.roll(z, ROPE_HALF, 1) * tsp


def _inproj_kernel(x_ref, sc_ref, sh_ref, w_ref, b_ref, qng_ref, wuq_ref, kvg_ref, wkn_ref, wv_ref,
                   tc_ref, tsm_ref, tsp_ref, *out_refs, keys_only):
    if keys_only:
        qk_ref, vtm_ref, gt_ref, k_ref, vta_ref = out_refs
    else:
        glu_ref, qk_ref, vtm_ref, so_ref, gt_ref, gates_ref, q_ref, k_ref, vta_ref = out_refs
    u = (x_ref[0] * (1.0 + sc_ref[...]) + sh_ref[...]).astype(BF16)
    tm = u.shape[0]
    ones_rows = jnp.ones((ONES_ROWS, tm), BF16)

    def proj(off, n):
        return jnp.dot(u, w_ref[:, off:off + n], preferred_element_type=F32) + b_ref[:, off:off + n]

    if not keys_only:
        cq = proj(OFF_CQ, Q_LORA)
        nq = (cq * lax.rsqrt(jnp.mean(cq * cq, axis=-1, keepdims=True) + LN_EPS) * qng_ref[...]).astype(BF16)
    ckv = proj(OFF_CKV, KV_LORA)
    nkv = (ckv * lax.rsqrt(jnp.mean(ckv * ckv, axis=-1, keepdims=True) + LN_EPS) * kvg_ref[...]).astype(BF16)

    if not keys_only:
        a = proj(OFF_A, 2 * CONV_DIM)
        glu_ref[0] = (a[:, :CONV_DIM] * _sigmoid(a[:, CONV_DIM:])).astype(BF16)

    qk_ref[0, :, :ML_DIM] = proj(OFF_QKV, ML_DIM).astype(BF16)
    qk_ref[0, :, ML_DIM:] = (proj(OFF_QKV + ML_DIM, ML_DIM) * ML_SCALE).astype(BF16)
    vt = proj(OFF_QKV + 2 * ML_DIM, ML_DIM).T.astype(BF16)
    gt = proj(OFF_GIF, LANES).T
    for c in range(tm // ML_CHUNK):
        cs = slice(c * ML_CHUNK, (c + 1) * ML_CHUNK)
        gt_ref[0, c] = gt[:4 * ML_HEADS, cs]
        for h in range(ML_HEADS):
            vtm_ref[0, c, h * ML_AUG:h * ML_AUG + ML_HEAD_DIM, :] = vt[h * ML_HEAD_DIM:(h + 1) * ML_HEAD_DIM, cs]
            vtm_ref[0, c, h * ML_AUG + ML_HEAD_DIM:(h + 1) * ML_AUG, :] = ones_rows[:, cs]
    tc, tsm, tsp = tc_ref[...], tsm_ref[...], tsp_ref[...]
    if not keys_only:
        so_ref[0] = _sigmoid(proj(OFF_O, ML_DIM)).astype(BF16)
        for j in range(3):
            gates_ref[0, :, j * D_MODEL:(j + 1) * D_MODEL] = _sigmoid(
                proj(OFF_GATES + j * D_MODEL, D_MODEL)).astype(BF16)
        q = jnp.dot(nq, wuq_ref[...], preferred_element_type=F32)
        for h in range(MLA_HEADS):
            sl = slice(h * HEAD_PAD, (h + 1) * HEAD_PAD)
            q_ref[0, :, sl] = (_rope(q[:, sl], tc, tsm, tsp) * Q_SCALE).astype(BF16)

    kn = jnp.dot(nkv, wkn_ref[...], preferred_element_type=F32)
    kr = _rope(proj(OFF_KR, HEAD_PAD), tc, tsm, tsp)
    for h in range(MLA_HEADS):
        sl = slice(h * HEAD_PAD, (h + 1) * HEAD_PAD)
        k_ref[0, :, sl] = (kn[:, sl] + kr).astype(BF16)
    vta = lax.dot_general(wv_ref[...], nkv, (((1,), (1,)), ((), ())),
                          preferred_element_type=F32).astype(BF16)
    for h in range(MLA_HEADS):
        vta_ref[0, h * ATT_AUG:h * ATT_AUG + V_HEAD, :] = vta[h * V_HEAD:(h + 1) * V_HEAD, :]
        vta_ref[0, h * ATT_AUG + V_HEAD:(h + 1) * ATT_AUG, :] = ones_rows


def _in_projection(x, mod, l, mod_row, wts, tables, tm, keys_only=False):
    bsz, t, d = x.shape
    tok = lambda n: pl.BlockSpec((1, tm, n), lambda b, i: (b, i, 0))
    tab = pl.BlockSpec((tm, HEAD_PAD), lambda b, i: (i, 0))
    hq = MLA_HEADS * HEAD_PAD
    nc, cpt = t // ML_CHUNK, tm // ML_CHUNK
    seq = lambda n: (jax.ShapeDtypeStruct((bsz, t, n), BF16), tok(n))
    outputs = {
        "glu": seq(CONV_DIM),
        "qk": seq(2 * ML_DIM),
        "vtm": (jax.ShapeDtypeStruct((bsz, nc, ML_HEADS * ML_AUG, ML_CHUNK), BF16),
                pl.BlockSpec((1, cpt, ML_HEADS * ML_AUG, ML_CHUNK), lambda b, i: (b, i, 0, 0))),
        "so": seq(ML_DIM),
        "gt": (jax.ShapeDtypeStruct((bsz, nc, 4 * ML_HEADS, ML_CHUNK), F32),
               pl.BlockSpec((1, cpt, 4 * ML_HEADS, ML_CHUNK), lambda b, i: (b, i, 0, 0))),
        "gates": seq(3 * D_MODEL),
        "q": seq(hq),
        "k": seq(hq),
        "vta": (jax.ShapeDtypeStruct((bsz, MLA_HEADS * ATT_AUG, t), BF16),
                pl.BlockSpec((1, MLA_HEADS * ATT_AUG, tm), lambda b, i: (b, 0, i))),
    }
    if keys_only:
        outputs = {n: outputs[n] for n in ("qk", "vtm", "gt", "k", "vta")}
    out_shape = [v[0] for v in outputs.values()]
    out_specs = [v[1] for v in outputs.values()]
    res = pl.pallas_call(
        functools.partial(_inproj_kernel, keys_only=keys_only),
        name="in_projection",
        grid=(bsz, t // tm),
        in_specs=[tok(d), _mod_spec(l, MOD_SC1, mod_row), _mod_spec(l, MOD_SH1, mod_row),
                  _layer_spec(l, (d, N_PACK)), _layer_spec(l, (1, N_PACK)),
                  _layer_spec(l, (1, Q_LORA)), _layer_spec(l, (Q_LORA, hq)),
                  _layer_spec(l, (1, KV_LORA)), _layer_spec(l, (KV_LORA, hq)),
                  _layer_spec(l, (MLA_HEADS * V_HEAD, KV_LORA)),
                  tab, tab, tab],
        out_specs=out_specs,
        out_shape=out_shape,
        compiler_params=pltpu.CompilerParams(
            dimension_semantics=("parallel", "arbitrary"), vmem_limit_bytes=VMEM_LIMIT),
    )(x, mod, mod, wts["w_pack"], wts["b_pack"], wts["qn_g"], wts["w_uq"], wts["kvn_g"], wts["w_kn"], wts["w_v"],
      *tables)
    return dict(zip(outputs, res))


CONV_HALO = 16
CONV_TILE = 128


def _conv_kernel(h_ref, wdw_ref, bdw_ref, g_ref, b_ref, o_ref, hp, stage, cv, shifted, *, t):
    nt = t // CONV_TILE
    ncg = CONV_DIM // LANES
    first = CONV_HALO - CONV_K // 2
    span = CONV_TILE + (CONV_K // SUBLANES) * SUBLANES
    for cg in range(ncg):
        hp[cg, 0:CONV_HALO, :] = jnp.zeros((CONV_HALO, LANES), F32)
        hp[cg, t + CONV_HALO:t + 2 * CONV_HALO, :] = jnp.zeros((CONV_HALO, LANES), F32)

    def fill(i, carry):
        r0 = pl.multiple_of(i * CONV_TILE, CONV_TILE)
        hx = h_ref[0, pl.ds(r0, CONV_TILE), :].astype(F32)
        for cg in range(ncg):
            hp[cg, pl.ds(r0 + CONV_HALO, CONV_TILE), :] = hx[:, cg * LANES:(cg + 1) * LANES]
        return carry

    lax.fori_loop(0, nt, fill, 0)

    def tile(i, carry):
        r0 = pl.multiple_of(i * CONV_TILE, CONV_TILE)

        def group(cg, inner):
            stage[...] = hp[cg, pl.ds(r0, CONV_TILE + 2 * CONV_HALO), :]
            acc = jnp.zeros((CONV_TILE, LANES), F32)
            for r in range(SUBLANES):
                shifted[r] = stage[r:r + span, :]
                for k in range(CONV_K):
                    if (first + k) % SUBLANES == r:
                        a = first + k - r
                        acc = acc + shifted[r, a:a + CONV_TILE, :] * wdw_ref[cg, k:k + 1, :]
            cv[cg] = acc + bdw_ref[cg]
            return inner

        lax.fori_loop(0, ncg, group, 0)
        z = jnp.concatenate([cv[cg] for cg in range(ncg)], axis=1)
        z = _layer_norm(z, g_ref[...], b_ref[...])
        o_ref[0, pl.ds(r0, CONV_TILE), :] = (z * _sigmoid(z)).astype(BF16)
        return carry

    lax.fori_loop(0, nt, tile, 0)


def _conv_branch(glu, l, wts):
    bsz, t, _ = glu.shape
    ncg = CONV_DIM // LANES
    return pl.pallas_call(
        functools.partial(_conv_kernel, t=t),
        name="conv",
        grid=(bsz,),
        in_specs=[pl.BlockSpec((1, t, CONV_DIM), lambda b: (b, 0, 0)),
                  _layer_spec(l, (ncg, CONV_K + 1, LANES)), _layer_spec(l, (ncg, 1, LANES)),
                  _layer_spec(l, (1, CONV_DIM)), _layer_spec(l, (1, CONV_DIM))],
        out_specs=pl.BlockSpec((1, t, CONV_DIM), lambda b: (b, 0, 0)),
        out_shape=jax.ShapeDtypeStruct((bsz, t, CONV_DIM), BF16),
        scratch_shapes=[pltpu.VMEM((ncg, t + 2 * CONV_HALO, LANES), F32),
                        pltpu.VMEM((CONV_TILE + 2 * CONV_HALO, LANES), F32),
                        pltpu.VMEM((ncg, CONV_TILE, LANES), F32),
                        pltpu.VMEM((SUBLANES, CONV_TILE + (CONV_K // SUBLANES) * SUBLANES, LANES), F32)],
        compiler_params=pltpu.CompilerParams(
            dimension_semantics=("parallel",), vmem_limit_bytes=VMEM_LIMIT),
    )(glu, wts["w_dw"], wts["b_dw"], wts["cn_g"], wts["cn_b"])


N_CHAIN = 2 * ML_HEADS


def _mlstm_kernel(*refs, nc_c, nc_x, ctx_out):
    refs = list(refs)
    qk_c_ref, vt_c_ref, gt_c_ref = refs[:3]
    so_c_ref = refs[3] if ctx_out else None
    qk_x_ref, vt_x_ref, gt_x_ref, so_x_ref, mlg_ref = refs[3 + ctx_out:8 + ctx_out]
    outs = refs[8 + ctx_out:9 + 2 * ctx_out]
    hn_c_ref, hn_x_ref = (outs[0] if ctx_out else None), outs[-1]
    hf_c, hb_c, hf_x, hb_x, c_scr, m_scr = refs[9 + 2 * ctx_out:]
    L = ML_CHUNK
    dh = ML_HEAD_DIM
    row = lax.broadcasted_iota(jnp.int32, (L, L), 0)
    col = lax.broadcasted_iota(jnp.int32, (L, L), 1)
    upper = row <= col
    lower = row >= col
    tri2 = jnp.concatenate([upper.astype(F32), lower.astype(F32)], axis=1).astype(BF16)
    is_fwd = lax.broadcasted_iota(jnp.int32, (N_CHAIN, L), 0) < ML_HEADS

    c_scr[...] = jnp.zeros(c_scr.shape, F32)
    m_scr[...] = jnp.zeros(m_scr.shape, F32)

    def iteration(qk_ref, vt_ref, gt_ref, hf, hb, cf, cb):
        gf = gt_ref[0, cf]
        gb = gt_ref[0, cb]
        li = jnp.where(is_fwd, gf[:N_CHAIN], gb[:N_CHAIN])
        lf = _log_sigmoid(jnp.where(is_fwd, gf[N_CHAIN:], gb[N_CHAIN:]))
        hi = lf.astype(BF16).astype(F32)
        mid = (lf - hi).astype(BF16).astype(F32)
        lo = lf - hi - mid
        parts = jnp.dot(jnp.concatenate([hi, mid, lo], axis=0).astype(BF16), tri2, preferred_element_type=F32)
        cum2 = parts[:N_CHAIN] + parts[N_CHAIN:2 * N_CHAIN] + parts[2 * N_CHAIN:]
        bc = jnp.where(is_fwd, cum2[:, :L], cum2[:, L:])
        r = li - bc
        b_last = jnp.sum(lf, axis=1, keepdims=True)
        m = m_scr[:, 0:1]
        w_src = b_last + r
        m_new = jnp.maximum(b_last + m, jnp.max(w_src, axis=1, keepdims=True))
        a_state = jnp.exp(b_last + m - m_new)
        w = jnp.exp(w_src - m_new)
        inter = bc + m
        r_cols = jnp.concatenate([r, jnp.zeros((L - N_CHAIN, L), F32)], axis=0).T
        m_scr[...] = jnp.broadcast_to(m_new, m_scr.shape)
        early = []
        for j in range(N_CHAIN):
            d, h = divmod(j, ML_HEADS)
            cc = cf if d == 0 else cb
            r0 = pl.multiple_of(cc * L, L)
            q = qk_ref[0, pl.ds(r0, L), h * dh:(h + 1) * dh]
            k = qk_ref[0, pl.ds(r0, L), ML_DIM + h * dh:ML_DIM + (h + 1) * dh]
            vt = vt_ref[0, cc, h * ML_AUG:(h + 1) * ML_AUG, :]
            ct = c_scr[j]
            both = lax.dot_general(jnp.concatenate([k, ct.astype(BF16)], axis=0), q, (((1,), (1,)), ((), ())),
                                   preferred_element_type=F32)
            early.append((both, vt, k, ct))
        for j in range(N_CHAIN):
            _, vt, k, ct = early[j]
            wv = (vt.astype(F32) * w[j:j + 1, :]).astype(BF16)
            c_scr[j] = a_state[j:j + 1, :] * ct + jnp.dot(wv, k, preferred_element_type=F32)
        for j in range(N_CHAIN):
            d, h = divmod(j, ML_HEADS)
            cc = cf if d == 0 else cb
            mask = upper if d == 0 else lower
            both, vt, _, _ = early[j]
            log_d = jnp.where(mask, bc[j:j + 1, :] + r_cols[:, j:j + 1], -jnp.inf)
            m_row = jnp.maximum(inter[j:j + 1, :], jnp.max(log_d, axis=0, keepdims=True))
            a_inter = jnp.exp(inter[j:j + 1, :] - m_row)
            dmat = jnp.exp(log_d - m_row)
            pt = (both[:L] * dmat).astype(BF16)
            num_aug = a_inter * both[L:] + jnp.dot(vt, pt, preferred_element_type=F32)
            den = num_aug[dh:dh + 1, :]
            hs = hf if d == 0 else hb
            hs[cc, h * dh:(h + 1) * dh, :] = num_aug[:dh] / jnp.maximum(jnp.abs(den), jnp.exp(-m_row))

    def scan(qk_ref, vt_ref, gt_ref, hf, hb, nc):
        def body(i, carry):
            iteration(qk_ref, vt_ref, gt_ref, hf, hb, i, nc - 1 - i)
            return carry

        lax.fori_loop(0, nc, body, 0, unroll=min(2, nc))

    scan(qk_c_ref, vt_c_ref, gt_c_ref, hf_c, hb_c, nc_c)
    scan(qk_x_ref, vt_x_ref, gt_x_ref, hf_x, hb_x, nc_x)

    def finish(hf, hb, so_ref, out_ref, nc):
        def body(c, carry):
            r0 = pl.multiple_of(c * L, L)
            hsum = hf[c] + hb[c]
            for h in range(ML_HEADS):
                sl = slice(h * dh, (h + 1) * dh)
                z = hsum[sl, :]
                mu = jnp.mean(z, axis=0, keepdims=True)
                zc = z - mu
                var = jnp.mean(zc * zc, axis=0, keepdims=True)
                hn = (zc * lax.rsqrt(var + LN_EPS)).T * mlg_ref[:, sl]
                out_ref[0, pl.ds(r0, L), sl] = (so_ref[0, pl.ds(r0, L), sl].astype(F32) * hn).astype(BF16)
            return carry

        lax.fori_loop(0, nc, body, 0)

    if ctx_out:
        finish(hf_c, hb_c, so_c_ref, hn_c_ref, nc_c)
    finish(hf_x, hb_x, so_x_ref, hn_x_ref, nc_x)


def _mlstm_branch(pc, px, l, wts, ctx_out):
    bsz, t_c, _ = pc["qk"].shape
    t_x = px["qk"].shape[1]
    nc_c, nc_x = t_c // ML_CHUNK, t_x // ML_CHUNK
    seq = lambda t, n: pl.BlockSpec((1, t, n), lambda b: (b, 0, 0))
    chunked = lambda nc, n: pl.BlockSpec((1, nc, n, ML_CHUNK), lambda b: (b, 0, 0, 0))
    ins = lambda t, nc, so: [seq(t, 2 * ML_DIM), chunked(nc, ML_HEADS * ML_AUG), chunked(nc, 4 * ML_HEADS)] + (
        [seq(t, ML_DIM)] if so else [])
    args = lambda p, so: [p["qk"], p["vtm"], p["gt"]] + ([p["so"]] if so else [])
    hbuf = lambda nc: pltpu.VMEM((nc, ML_DIM, ML_CHUNK), F32)
    out_t = ([t_c] if ctx_out else []) + [t_x]
    res = pl.pallas_call(
        functools.partial(_mlstm_kernel, nc_c=nc_c, nc_x=nc_x, ctx_out=ctx_out),
        name="mlstm",
        grid=(bsz,),
        in_specs=ins(t_c, nc_c, ctx_out) + ins(t_x, nc_x, True) + [_layer_spec(l, (1, ML_DIM))],
        out_specs=[seq(t, ML_DIM) for t in out_t],
        out_shape=[jax.ShapeDtypeStruct((bsz, t, ML_DIM), BF16) for t in out_t],
        scratch_shapes=[hbuf(nc_c), hbuf(nc_c), hbuf(nc_x), hbuf(nc_x),
                        pltpu.VMEM((N_CHAIN, ML_AUG, ML_HEAD_DIM), F32),
                        pltpu.VMEM((N_CHAIN, LANES), F32)],
        compiler_params=pltpu.CompilerParams(
            dimension_semantics=("parallel",), vmem_limit_bytes=VMEM_LIMIT),
    )(*args(pc, ctx_out), *args(px, True), wts["ml_g"])
    return (res[0], res[1]) if ctx_out else (None, res[0])


ATT_KEYS = 256
ATT_QG = 256
ATT_LOOKAHEAD = 8


def _attn_kernel(*refs, n_seg):
    q_ref = refs[0]
    k_refs = refs[1:1 + n_seg]
    vt_refs = refs[1 + n_seg:1 + 2 * n_seg]
    o_ref = refs[1 + 2 * n_seg]
    qg = min(ATT_QG, q_ref.shape[1])
    n_qg = q_ref.shape[1] // qg
    chunks = [(k_ref, vt_ref, slice(j, j + min(ATT_KEYS, k_ref.shape[1])))
              for k_ref, vt_ref in zip(k_refs, vt_refs)
              for j in range(0, k_ref.shape[1], min(ATT_KEYS, k_ref.shape[1]))]
    steps = [(ci, h, g) for ci in range(len(chunks)) for h in range(MLA_HEADS) for g in range(n_qg)]

    def scores(ci, h, g):
        k_ref, _, ks = chunks[ci]
        hs = slice(h * HEAD_PAD, (h + 1) * HEAD_PAD)
        return lax.dot_general(k_ref[0, ks, hs], q_ref[0, g * qg:(g + 1) * qg, hs],
                               (((1,), (1,)), ((), ())), preferred_element_type=F32)

    state = {}
    pending = {}
    for idx in range(len(steps) + ATT_LOOKAHEAD):
        if idx < len(steps):
            pending[idx] = scores(*steps[idx])
        if idx < ATT_LOOKAHEAD:
            continue
        ci, h, g = steps[idx - ATT_LOOKAHEAD]
        s = pending.pop(idx - ATT_LOOKAHEAD)
        _, vt_ref, ks = chunks[ci]
        vtj = vt_ref[0, h * ATT_AUG:(h + 1) * ATT_AUG, ks]
        cm = jnp.max(s, axis=0, keepdims=True)
        if ci == 0:
            m_new = cm
            acc = jnp.dot(vtj, jnp.exp2(s - m_new).astype(BF16), preferred_element_type=F32)
        else:
            m_old, acc_old = state[h, g]
            m_new = jnp.maximum(m_old, cm)
            acc = acc_old * jnp.exp2(m_old - m_new) + jnp.dot(
                vtj, jnp.exp2(s - m_new).astype(BF16), preferred_element_type=F32)
        state[h, g] = (m_new, acc)
    outs = []
    for h in range(MLA_HEADS):
        accs = [state[h, g][1] for g in range(n_qg)]
        outs.append(jnp.concatenate([a[:V_HEAD] * (1.0 / a[V_HEAD:V_HEAD + 1]) for a in accs], axis=1))
    o_ref[0] = jnp.concatenate(outs, axis=0).T.astype(BF16)


def _attention(q, ks, vts, tq):
    bsz, t, hq = q.shape
    n_seg = len(ks)
    hv = MLA_HEADS * V_HEAD
    kspec = [pl.BlockSpec((1, k.shape[1], hq), lambda b, i: (b, 0, 0)) for k in ks]
    vspec = [pl.BlockSpec((1, MLA_HEADS * ATT_AUG, vt.shape[2]), lambda b, i: (b, 0, 0)) for vt in vts]
    return pl.pallas_call(
        functools.partial(_attn_kernel, n_seg=n_seg),
        name="attention",
        grid=(bsz, t // tq),
        in_specs=[pl.BlockSpec((1, tq, hq), lambda b, i: (b, i, 0))] + kspec + vspec,
        out_specs=pl.BlockSpec((1, tq, hv), lambda b, i: (b, i, 0)),
        out_shape=jax.ShapeDtypeStruct((bsz, t, hv), BF16),
        compiler_params=pltpu.CompilerParams(
            dimension_semantics=("parallel", "arbitrary"), vmem_limit_bytes=VMEM_LIMIT),
    )(q, *ks, *vts)


FF_CHUNK = 1024
MERGE_GROUPS = 2


def _merge_mlp_kernel(x_ref, hc_ref, hm_ref, oa_ref, gates_ref, g1_ref, sc2_ref, sh2_ref, g2_ref,
                      wc_ref, wm_ref, wa_ref, wo_ref, bo_ref, ln1g_ref, ln1b_ref,
                      w1_ref, b1_ref, w2_ref, b2_ref, ln2g_ref, ln2b_ref, o_ref):
    d = D_MODEL
    tm = x_ref.shape[1]
    rows = [slice(g * tm // MERGE_GROUPS, (g + 1) * tm // MERGE_GROUPS) for g in range(MERGE_GROUPS)]
    dot = functools.partial(jnp.dot, preferred_element_type=F32)
    yc = [dot(hc_ref[0, r, :], wc_ref[...]) for r in rows]
    ym = [dot(hm_ref[0, r, :], wm_ref[...]) for r in rows]
    ya = [dot(oa_ref[0, r, :], wa_ref[...]) for r in rows]
    mix = [(gates_ref[0, r, 0:d].astype(F32) * c + gates_ref[0, r, d:2 * d].astype(F32) * m
            + gates_ref[0, r, 2 * d:3 * d].astype(F32) * a).astype(BF16) for r, c, m, a in zip(rows, yc, ym, ya)]
    y = [dot(mx, wo_ref[...]) + bo_ref[...] for mx in mix]
    x1 = [_layer_norm(ALPHA * x_ref[0, r, :] + g1_ref[...] * yy, ln1g_ref[...], ln1b_ref[...])
          for r, yy in zip(rows, y)]
    u2 = [(xx * (1.0 + sc2_ref[...]) + sh2_ref[...]).astype(BF16) for xx in x1]
    acc = [jnp.zeros(xx.shape, F32) for xx in x1]
    for j in range(D_FF // FF_CHUNK):
        sl = slice(j * FF_CHUNK, (j + 1) * FF_CHUNK)
        hdn = [jnp.maximum(dot(u, w1_ref[:, sl]) + b1_ref[:, sl], 0.0) for u in u2]
        acc = [ac + dot((hd * hd).astype(BF16), w2_ref[sl, :]) for ac, hd in zip(acc, hdn)]
    for r, xx, ac in zip(rows, x1, acc):
        o_ref[0, r, :] = _layer_norm(ALPHA * xx + g2_ref[...] * (ac + b2_ref[...]), ln2g_ref[...], ln2b_ref[...])


def _merge_mlp(x, hc, hm, oa, gates, mod, l, mod_row, wts, tm):
    bsz, t, d = x.shape
    tok = lambda n: pl.BlockSpec((1, tm, n), lambda b, i: (b, i, 0))
    mods = [_mod_spec(l, k, mod_row) for k in (MOD_G1, MOD_SC2, MOD_SH2, MOD_G2)]
    return pl.pallas_call(
        _merge_mlp_kernel,
        name="merge_mlp",
        grid=(bsz, t // tm),
        in_specs=[tok(d), tok(CONV_DIM), tok(ML_DIM), tok(MLA_HEADS * V_HEAD), tok(3 * d)] + mods + [
            _layer_spec(l, (CONV_DIM, d)), _layer_spec(l, (ML_DIM, d)), _layer_spec(l, (MLA_HEADS * V_HEAD, d)),
            _layer_spec(l, (d, d)), _layer_spec(l, (1, d)), _layer_spec(l, (1, d)), _layer_spec(l, (1, d)),
            _layer_spec(l, (d, D_FF)), _layer_spec(l, (1, D_FF)), _layer_spec(l, (D_FF, d)), _layer_spec(l, (1, d)),
            _layer_spec(l, (1, d)), _layer_spec(l, (1, d))],
        out_specs=tok(d),
        out_shape=jax.ShapeDtypeStruct((bsz, t, d), F32),
        compiler_params=pltpu.CompilerParams(
            dimension_semantics=("parallel", "arbitrary"), vmem_limit_bytes=VMEM_LIMIT),
    )(x, hc, hm, oa, gates, mod, mod, mod, mod,
      wts["w_conv_out"], wts["w_ml_out"], wts["w_mla_out"], wts["w_out"], wts["b_out"], wts["ln1_g"], wts["ln1_b"],
      wts["w1"], wts["b1"], wts["w2"], wts["b2"], wts["ln2_g"], wts["ln2_b"])


def _rope_runs():
    runs = []
    for half in range(2):
        for a in range(2):
            start = a * 2 * ROPE_FREQ + half * ROPE_FREQ
            runs.append(slice(start, start + ROPE_FREQ))
    return runs


def _pack_weights(w_in, b_in, w_dw, b_dw, cn_g, cn_b, w_conv_out, ml_g, w_ml_out, qn_g, w_uq, kvn_g, w_ukv,
                  w_mla_out, w_out, b_out, ln1_g, ln1_b, w1, b1, w2, b2, ln2_g, ln2_b):
    nl = w_in.shape[0]

    def packed(z, dtype):
        a, qkv, o, g, cq, ckv, kr, gates = (z[..., _SRC[i]:_SRC[i + 1]].astype(dtype) for i in range(8))
        zeros = lambda n: jnp.zeros(z.shape[:-1] + (n,), dtype)
        kr_block = [zeros(QK_NOPE)] + [kr[..., s] for s in _rope_runs()] + [zeros(HEAD_PAD - QK_NOPE - QK_ROPE)]
        g_block = [g[..., i * ML_HEADS:(i + 1) * ML_HEADS] for i in (0, 2, 1, 3)] + [zeros(LANES - 4 * ML_HEADS)]
        return jnp.concatenate([a, qkv, o, cq, ckv] + kr_block + g_block + [gates], axis=-1)

    wq = w_uq.reshape(nl, Q_LORA, MLA_HEADS, QK_NOPE + QK_ROPE)
    wq_p = jnp.concatenate([wq[..., :QK_NOPE]] + [wq[..., QK_NOPE:][..., s] for s in _rope_runs()]
                           + [jnp.zeros(wq.shape[:-1] + (HEAD_PAD - QK_NOPE - QK_ROPE,), F32)], axis=-1)
    wkv = w_ukv.reshape(nl, KV_LORA, MLA_HEADS, QK_NOPE + V_HEAD)
    wkn_p = jnp.concatenate([wkv[..., :QK_NOPE], jnp.zeros(wkv.shape[:-1] + (HEAD_PAD - QK_NOPE,), F32)], axis=-1)
    wv_t = wkv[..., QK_NOPE:].reshape(nl, KV_LORA, MLA_HEADS * V_HEAD).transpose(0, 2, 1)
    ncg = CONV_DIM // LANES
    row = lambda z: z[:, None, :]
    return {
        "w_pack": packed(w_in, BF16), "b_pack": row(packed(b_in, F32)),
        "qn_g": row(qn_g), "w_uq": wq_p.reshape(nl, Q_LORA, MLA_HEADS * HEAD_PAD).astype(BF16),
        "kvn_g": row(kvn_g), "w_kn": wkn_p.reshape(nl, KV_LORA, MLA_HEADS * HEAD_PAD).astype(BF16),
        "w_v": wv_t.astype(BF16),
        "w_dw": jnp.pad(w_dw, ((0, 0), (0, 1), (0, 0))).reshape(nl, CONV_K + 1, ncg, LANES).transpose(0, 2, 1, 3),
        "b_dw": b_dw.reshape(nl, ncg, 1, LANES),
        "cn_g": row(cn_g), "cn_b": row(cn_b), "ml_g": row(ml_g),
        "w_conv_out": w_conv_out.astype(BF16), "w_ml_out": w_ml_out.astype(BF16),
        "w_mla_out": w_mla_out.astype(BF16), "w_out": w_out.astype(BF16), "b_out": row(b_out),
        "ln1_g": row(ln1_g), "ln1_b": row(ln1_b), "w1": w1.astype(BF16), "b1": row(b1),
        "w2": w2.astype(BF16), "b2": row(b2), "ln2_g": row(ln2_g), "ln2_b": row(ln2_b),
    }


def _rope_tables(n_tokens):
    rows = n_tokens // GRID_W
    rr, cc = np.meshgrid(np.arange(rows, dtype=np.float32), np.arange(GRID_W, dtype=np.float32), indexing="ij")
    inv = (np.float32(ROPE_THETA) ** (-np.arange(ROPE_FREQ, dtype=np.float32) / np.float32(ROPE_FREQ))).astype(np.float32)
    ang = np.stack([rr.reshape(-1), cc.reshape(-1)], -1)[..., None] * inv
    cos = np.cos(ang).astype(np.float32).reshape(n_tokens, ROPE_HALF)
    sin = np.sin(ang).astype(np.float32).reshape(n_tokens, ROPE_HALF)
    one = np.ones((n_tokens, QK_NOPE), np.float32)
    z_nope = np.zeros((n_tokens, QK_NOPE), np.float32)
    z_half = np.zeros((n_tokens, ROPE_HALF), np.float32)
    z_pad = np.zeros((n_tokens, HEAD_PAD - QK_NOPE - QK_ROPE), np.float32)
    tc = np.concatenate([one, cos, cos, z_pad], axis=1)
    tsm = np.concatenate([z_nope, -sin, z_half, z_pad], axis=1)
    tsp = np.concatenate([z_nope, z_half, sin, z_pad], axis=1)
    return tc, tsm, tsp


def _identity_tables(n_tokens):
    return (np.ones((n_tokens, HEAD_PAD), np.float32), np.zeros((n_tokens, HEAD_PAD), np.float32),
            np.zeros((n_tokens, HEAD_PAD), np.float32))


def kernel(x, c, ctx, c_ctx, w_mod, b_mod, w_in, b_in, w_dw, b_dw, conv_norm_g, conv_norm_b, w_conv_out,
           mlstm_norm_g, w_mlstm_out, q_norm_g, w_uq, kv_norm_g, w_ukv, w_mla_out, w_out, b_out,
           ln1_g, ln1_b, w1, b1, w2, b2, ln2_g, ln2_b):
    bsz, t_x, d = x.shape
    t_c = ctx.shape[1]
    n_rows = 24
    cc = jnp.concatenate([c, c_ctx[None, :], jnp.zeros((n_rows - bsz - 1, d), F32)], axis=0)
    mod = _modulation(cc, w_mod, b_mod).reshape(DEPTH, n_rows, N_MOD, 1, d)
    wts = _pack_weights(w_in, b_in, w_dw, b_dw, conv_norm_g, conv_norm_b, w_conv_out, mlstm_norm_g,
                        w_mlstm_out, q_norm_g, w_uq, kv_norm_g, w_ukv, w_mla_out, w_out, b_out,
                        ln1_g, ln1_b, w1, b1, w2, b2, ln2_g, ln2_b)
    rope_x = _rope_tables(t_x)
    rope_c = _identity_tables(t_c)
    ctx_row = bsz
    tm_x, tm_c = _token_tile(t_x), _token_tile(t_c)

    for l in range(DEPTH):
        with_ctx = l < DEPTH - 1
        px = _in_projection(x, mod, l, None, wts, rope_x, tm_x)
        pc = _in_projection(ctx, mod, l, ctx_row, wts, rope_c, tm_c, keys_only=not with_ctx)

        hm_c, hm_x = _mlstm_branch(pc, px, l, wts, with_ctx)
        oa_x = _attention(px["q"], [pc["k"], px["k"]], [pc["vta"], px["vta"]], tm_x)
        hc_x = _conv_branch(px["glu"], l, wts)
        x_new = _merge_mlp(x, hc_x, hm_x, oa_x, px["gates"], mod, l, None, wts, tm_x)
        if with_ctx:
            hc_c = _conv_branch(pc["glu"], l, wts)
            oa_c = _attention(pc["q"], [pc["k"]], [pc["vta"]], tm_c)
            ctx = _merge_mlp(ctx, hc_c, hm_c, oa_c, pc["gates"], mod, l, ctx_row, wts, tm_c)
        x = x_new
    return x
```

```python
import functools

import numpy as np
import jax
import jax.numpy as jnp
from jax import lax
from jax.experimental import pallas as pl
from jax.experimental.pallas import tpu as pltpu

F32 = jnp.float32
BF16 = jnp.bfloat16

D_MODEL = 1024
DEPTH = 2
GRID_W = 64
CONV_DIM = 512
CONV_K = 31
ML_HEADS = 4
ML_HEAD_DIM = 128
ML_DIM = ML_HEADS * ML_HEAD_DIM
ML_CHUNK = 256
MLA_HEADS = 8
QK_NOPE = 64
QK_ROPE = 32
V_HEAD = 64
Q_LORA = 768
KV_LORA = 256
ROPE_THETA = 10000.0
ROPE_FREQ = QK_ROPE // 4
MLA_SCALE = (QK_NOPE + QK_ROPE) ** -0.5
ML_SCALE = ML_HEAD_DIM ** -0.5
D_FF = 4 * D_MODEL
LN_EPS = 1e-5
ALPHA = (2 * DEPTH) ** 0.25
N_MOD = 6
MOD_SH1, MOD_SC1, MOD_G1, MOD_SH2, MOD_SC2, MOD_G2 = range(N_MOD)

LANES = 128
SUBLANES = 8
HEAD_PAD = LANES
ROPE_HALF = QK_ROPE // 2
ONES_ROWS = 16
ML_AUG = ML_HEAD_DIM + ONES_ROWS
ATT_AUG = V_HEAD + ONES_ROWS
Q_SCALE = MLA_SCALE * float(np.log2(np.e))
VMEM_LIMIT = 56 * 1024 * 1024

OFF_A = 0
OFF_QKV = OFF_A + 2 * CONV_DIM
OFF_O = OFF_QKV + 3 * ML_DIM
OFF_CQ = OFF_O + ML_DIM
OFF_CKV = OFF_CQ + Q_LORA
OFF_KR = OFF_CKV + KV_LORA
OFF_GIF = OFF_KR + HEAD_PAD
OFF_GATES = OFF_GIF + LANES
N_PACK = OFF_GATES + 3 * D_MODEL

_COLS = (2 * CONV_DIM, 3 * ML_DIM, ML_DIM, 4 * ML_HEADS, Q_LORA, KV_LORA, QK_ROPE, 3 * D_MODEL)
_SRC = tuple(int(s) for s in np.cumsum((0,) + _COLS))


def _layer_spec(l, shape):
    nd = len(shape)
    return pl.BlockSpec((None,) + tuple(shape), lambda *_: (l,) + (0,) * nd, pipeline_mode=pl.Buffered(1))


def _mod_spec(l, k, row=None):
    block = (None, None, None, 1, D_MODEL)
    if row is None:
        return pl.BlockSpec(block, lambda b, i: (l, b, k, 0, 0))
    return pl.BlockSpec(block, lambda b, i: (l, row, k, 0, 0))


def _sigmoid(z):
    return 1.0 / (1.0 + jnp.exp(-z))


def _log_sigmoid(z):
    return jnp.minimum(z, 0.0) - jnp.log1p(jnp.exp(-jnp.abs(z)))


def _layer_norm(z, g, b):
    mu = jnp.mean(z, axis=-1, keepdims=True)
    zc = z - mu
    var = jnp.mean(zc * zc, axis=-1, keepdims=True)
    return zc * lax.rsqrt(var + LN_EPS) * g + b


def _mod_kernel(c_ref, w_ref, b_ref, o_ref):
    c = c_ref[...]
    s = (c * _sigmoid(c)).astype(BF16)
    o_ref[0] = jnp.dot(s, w_ref[0].astype(BF16), preferred_element_type=F32) + b_ref[0]


def _modulation(cc, w_mod, b_mod):
    nl, d, n = w_mod.shape
    r = cc.shape[0]
    tn = 1024
    return pl.pallas_call(
        _mod_kernel,
        name="modulation",
        grid=(nl, n // tn),
        in_specs=[
            pl.BlockSpec((r, d), lambda l, j: (0, 0)),
            pl.BlockSpec((1, d, tn), lambda l, j: (l, 0, j)),
            pl.BlockSpec((1, 1, tn), lambda l, j: (l, 0, j)),
        ],
        out_specs=pl.BlockSpec((1, r, tn), lambda l, j: (l, 0, j)),
        out_shape=jax.ShapeDtypeStruct((nl, r, n), F32),
        compiler_params=pltpu.CompilerParams(
            dimension_semantics=("arbitrary", "arbitrary"), vmem_limit_bytes=VMEM_LIMIT),
    )(cc, w_mod, b_mod.reshape(nl, 1, n))


def _rope(z, tc, tsm, tsp):
    return z * tc + pltpu.roll(z, HEAD_PAD - ROPE_HALF, 1) * tsm + pltpu.roll(z, ROPE_HALF, 1) * tsp


def _inproj_kernel(x_ref, sc_ref, sh_ref, w_ref, b_ref, qng_ref, wuq_ref, kvg_ref, wkn_ref, wv_ref,
                   tc_ref, tsm_ref, tsp_ref, *out_refs, keys_only):
    if keys_only:
        qk_ref, vtm_ref, gt_ref, k_ref, vta_ref = out_refs
    else:
        glu_ref, qk_ref, vtm_ref, so_ref, gt_ref, gates_ref, q_ref, k_ref, vta_ref = out_refs
    u = (x_ref[0] * (1.0 + sc_ref[...]) + sh_ref[...]).astype(BF16)
    tm = u.shape[0]
    ones_rows = jnp.ones((ONES_ROWS, tm), BF16)

    def proj(off, n):
        return jnp.dot(u, w_ref[:, off:off + n], preferred_element_type=F32) + b_ref[:, off:off + n]

    if not keys_only:
        cq = proj(OFF_CQ, Q_LORA)
        nq = (cq * lax.rsqrt(jnp.mean(cq * cq, axis=-1, keepdims=True) + LN_EPS) * qng_ref[...]).astype(BF16)
    ckv = proj(OFF_CKV, KV_LORA)
    nkv = (ckv * lax.rsqrt(jnp.mean(ckv * ckv, axis=-1, keepdims=True) + LN_EPS) * kvg_ref[...]).astype(BF16)

    if not keys_only:
        a = proj(OFF_A, 2 * CONV_DIM)
        glu_ref[0] = (a[:, :CONV_DIM] * _sigmoid(a[:, CONV_DIM:])).astype(BF16)

    qk_ref[0, :, :ML_DIM] = proj(OFF_QKV, ML_DIM).astype(BF16)
    qk_ref[0, :, ML_DIM:] = (proj(OFF_QKV + ML_DIM, ML_DIM) * ML_SCALE).astype(BF16)
    vt = proj(OFF_QKV + 2 * ML_DIM, ML_DIM).T.astype(BF16)
    gt = proj(OFF_GIF, LANES).T
    for c in range(tm // ML_CHUNK):
        cs = slice(c * ML_CHUNK, (c + 1) * ML_CHUNK)
        gt_ref[0, c] = gt[:4 * ML_HEADS, cs]
        for h in range(ML_HEADS):
            vtm_ref[0, c, h * ML_AUG:h * ML_AUG + ML_HEAD_DIM, :] = vt[h * ML_HEAD_DIM:(h + 1) * ML_HEAD_DIM, cs]
            vtm_ref[0, c, h * ML_AUG + ML_HEAD_DIM:(h + 1) * ML_AUG, :] = ones_rows[:, cs]
    tc, tsm, tsp = tc_ref[...], tsm_ref[...], tsp_ref[...]
    if not keys_only:
        so_ref[0] = _sigmoid(proj(OFF_O, ML_DIM)).astype(BF16)
        for j in range(3):
            gates_ref[0, :, j * D_MODEL:(j + 1) * D_MODEL] = _sigmoid(
                proj(OFF_GATES + j * D_MODEL, D_MODEL)).astype(BF16)
        q = jnp.dot(nq, wuq_ref[...], preferred_element_type=F32)
        for h in range(MLA_HEADS):
            sl = slice(h * HEAD_PAD, (h + 1) * HEAD_PAD)
            q_ref[0, :, sl] = (_rope(q[:, sl], tc, tsm, tsp) * Q_SCALE).astype(BF16)

    kn = jnp.dot(nkv, wkn_ref[...], preferred_element_type=F32)
    kr = _rope(proj(OFF_KR, HEAD_PAD), tc, tsm, tsp)
    for h in range(MLA_HEADS):
        sl = slice(h * HEAD_PAD, (h + 1) * HEAD_PAD)
        k_ref[0, :, sl] = (kn[:, sl] + kr).astype(BF16)
    vta = lax.dot_general(wv_ref[...], nkv, (((1,), (1,)), ((), ())),
                          preferred_element_type=F32).astype(BF16)
    for h in range(MLA_HEADS):
        vta_ref[0, h * ATT_AUG:h * ATT_AUG + V_HEAD, :] = vta[h * V_HEAD:(h + 1) * V_HEAD, :]
        vta_ref[0, h * ATT_AUG + V_HEAD:(h + 1) * ATT_AUG, :] = ones_rows


def _in_projection(x, mod, l, mod_row, wts, tables, tm, keys_only=False):
    bsz, t, d = x.shape
    tok = lambda n: pl.BlockSpec((1, tm, n), lambda b, i: (b, i, 0))
    tab = pl.BlockSpec((tm, HEAD_PAD), lambda b, i: (i, 0))
    hq = MLA_HEADS * HEAD_PAD
    nc, cpt = t // ML_CHUNK, tm // ML_CHUNK
    seq = lambda n: (jax.ShapeDtypeStruct((bsz, t, n), BF16), tok(n))
    outputs = {
        "glu": seq(CONV_DIM),
        "qk": seq(2 * ML_DIM),
        "vtm": (jax.ShapeDtypeStruct((bsz, nc, ML_HEADS * ML_AUG, ML_CHUNK), BF16),
                pl.BlockSpec((1, cpt, ML_HEADS * ML_AUG, ML_CHUNK), lambda b, i: (b, i, 0, 0))),
        "so": seq(ML_DIM),
        "gt": (jax.ShapeDtypeStruct((bsz, nc, 4 * ML_HEADS, ML_CHUNK), F32),
               pl.BlockSpec((1, cpt, 4 * ML_HEADS, ML_CHUNK), lambda b, i: (b, i, 0, 0))),
        "gates": seq(3 * D_MODEL),
        "q": seq(hq),
        "k": seq(hq),
        "vta": (jax.ShapeDtypeStruct((bsz, MLA_HEADS * ATT_AUG, t), BF16),
                pl.BlockSpec((1, MLA_HEADS * ATT_AUG, tm), lambda b, i: (b, 0, i))),
    }
    if keys_only:
        outputs = {n: outputs[n] for n in ("qk", "vtm", "gt", "k", "vta")}
    out_shape = [v[0] for v in outputs.values()]
    out_specs = [v[1] for v in outputs.values()]
    res = pl.pallas_call(
        functools.partial(_inproj_kernel, keys_only=keys_only),
        name="in_projection",
        grid=(bsz, t // tm),
        in_specs=[tok(d), _mod_spec(l, MOD_SC1, mod_row), _mod_spec(l, MOD_SH1, mod_row),
                  _layer_spec(l, (d, N_PACK)), _layer_spec(l, (1, N_PACK)),
                  _layer_spec(l, (1, Q_LORA)), _layer_spec(l, (Q_LORA, hq)),
                  _layer_spec(l, (1, KV_LORA)), _layer_spec(l, (KV_LORA, hq)),
                  _layer_spec(l, (MLA_HEADS * V_HEAD, KV_LORA)),
                  tab, tab, tab],
        out_specs=out_specs,
        out_shape=out_shape,
        compiler_params=pltpu.CompilerParams(
            dimension_semantics=("parallel", "arbitrary"), vmem_limit_bytes=VMEM_LIMIT),
    )(x, mod, mod, wts["w_pack"], wts["b_pack"], wts["qn_g"], wts["w_uq"], wts["kvn_g"], wts["w_kn"], wts["w_v"],
      *tables)
    return dict(zip(outputs, res))


CONV_HALO = 16
CONV_TILE = 256


def _conv_kernel(h_ref, wdw_ref, bdw_ref, g_ref, b_ref, o_ref, hp, stage, cv, shifted, *, t):
    nt = t // CONV_TILE
    ncg = CONV_DIM // LANES
    first = CONV_HALO - CONV_K // 2
    span = CONV_TILE + (CONV_K // SUBLANES) * SUBLANES
    for cg in range(ncg):
        hp[cg, 0:CONV_HALO, :] = jnp.zeros((CONV_HALO, LANES), F32)
        hp[cg, t + CONV_HALO:t + 2 * CONV_HALO, :] = jnp.zeros((CONV_HALO, LANES), F32)

    def fill(i, carry):
        r0 = pl.multiple_of(i * CONV_TILE, CONV_TILE)
        hx = h_ref[0, pl.ds(r0, CONV_TILE), :].astype(F32)
        for cg in range(ncg):
            hp[cg, pl.ds(r0 + CONV_HALO, CONV_TILE), :] = hx[:, cg * LANES:(cg + 1) * LANES]
        return carry

    lax.fori_loop(0, nt, fill, 0)

    def tile(i, carry):
        r0 = pl.multiple_of(i * CONV_TILE, CONV_TILE)

        def group(cg, inner):
            stage[...] = hp[cg, pl.ds(r0, CONV_TILE + 2 * CONV_HALO), :]
            acc = jnp.zeros((CONV_TILE, LANES), F32)
            for r in range(SUBLANES):
                shifted[r] = stage[r:r + span, :]
                for k in range(CONV_K):
                    if (first + k) % SUBLANES == r:
                        a = first + k - r
                        acc = acc + shifted[r, a:a + CONV_TILE, :] * wdw_ref[cg, k:k + 1, :]
            cv[cg] = acc + bdw_ref[cg]
            return inner

        lax.fori_loop(0, ncg, group, 0)
        z = jnp.concatenate([cv[cg] for cg in range(ncg)], axis=1)
        z = _layer_norm(z, g_ref[...], b_ref[...])
        o_ref[0, pl.ds(r0, CONV_TILE), :] = (z * _sigmoid(z)).astype(BF16)
        return carry

    lax.fori_loop(0, nt, tile, 0)


def _conv_branch(glu, l, wts):
    bsz, t, _ = glu.shape
    ncg = CONV_DIM // LANES
    return pl.pallas_call(
        functools.partial(_conv_kernel, t=t),
        name="conv",
        grid=(bsz,),
        in_specs=[pl.BlockSpec((1, t, CONV_DIM), lambda b: (b, 0, 0)),
                  _layer_spec(l, (ncg, CONV_K + 1, LANES)), _layer_spec(l, (ncg, 1, LANES)),
                  _layer_spec(l, (1, CONV_DIM)), _layer_spec(l, (1, CONV_DIM))],
        out_specs=pl.BlockSpec((1, t, CONV_DIM), lambda b: (b, 0, 0)),
        out_shape=jax.ShapeDtypeStruct((bsz, t, CONV_DIM), BF16),
        scratch_shapes=[pltpu.VMEM((ncg, t + 2 * CONV_HALO, LANES), F32),
                        pltpu.VMEM((CONV_TILE + 2 * CONV_HALO, LANES), F32),
                        pltpu.VMEM((ncg, CONV_TILE, LANES), F32),
                        pltpu.VMEM((SUBLANES, CONV_TILE + (CONV_K // SUBLANES) * SUBLANES, LANES), F32)],
        compiler_params=pltpu.CompilerParams(
            dimension_semantics=("parallel",), vmem_limit_bytes=VMEM_LIMIT),
    )(glu, wts["w_dw"], wts["b_dw"], wts["cn_g"], wts["cn_b"])


N_CHAIN = 2 * ML_HEADS


def _mlstm_kernel(*refs, nc_c, nc_x, ctx_out):
    refs = list(refs)
    qk_c_ref, vt_c_ref, gt_c_ref = refs[:3]
    so_c_ref = refs[3] if ctx_out else None
    qk_x_ref, vt_x_ref, gt_x_ref, so_x_ref, mlg_ref = refs[3 + ctx_out:8 + ctx_out]
    outs = refs[8 + ctx_out:9 + 2 * ctx_out]
    hn_c_ref, hn_x_ref = (outs[0] if ctx_out else None), outs[-1]
    hf_c, hb_c, hf_x, hb_x, c_scr, m_scr = refs[9 + 2 * ctx_out:]
    L = ML_CHUNK
    dh = ML_HEAD_DIM
    row = lax.broadcasted_iota(jnp.int32, (L, L), 0)
    col = lax.broadcasted_iota(jnp.int32, (L, L), 1)
    upper = row <= col
    lower = row >= col
    tri2 = jnp.concatenate([upper.astype(F32), lower.astype(F32)], axis=1).astype(BF16)
    is_fwd = lax.broadcasted_iota(jnp.int32, (N_CHAIN, L), 0) < ML_HEADS

    c_scr[...] = jnp.zeros(c_scr.shape, F32)
    m_scr[...] = jnp.zeros(m_scr.shape, F32)

    def iteration(qk_ref, vt_ref, gt_ref, hf, hb, cf, cb):
        gf = gt_ref[0, cf]
        gb = gt_ref[0, cb]
        li = jnp.where(is_fwd, gf[:N_CHAIN], gb[:N_CHAIN])
        lf = _log_sigmoid(jnp.where(is_fwd, gf[N_CHAIN:], gb[N_CHAIN:]))
        hi = lf.astype(BF16).astype(F32)
        mid = (lf - hi).astype(BF16).astype(F32)
        lo = lf - hi - mid
        parts = jnp.dot(jnp.concatenate([hi, mid, lo], axis=0).astype(BF16), tri2, preferred_element_type=F32)
        cum2 = parts[:N_CHAIN] + parts[N_CHAIN:2 * N_CHAIN] + parts[2 * N_CHAIN:]
        bc = jnp.where(is_fwd, cum2[:, :L], cum2[:, L:])
        r = li - bc
        b_last = jnp.sum(lf, axis=1, keepdims=True)
        m = m_scr[:, 0:1]
        w_src = b_last + r
        m_new = jnp.maximum(b_last + m, jnp.max(w_src, axis=1, keepdims=True))
        a_state = jnp.exp(b_last + m - m_new)
        w = jnp.exp(w_src - m_new)
        inter = bc + m
        r_cols = jnp.concatenate([r, jnp.zeros((L - N_CHAIN, L), F32)], axis=0).T
        m_scr[...] = jnp.broadcast_to(m_new, m_scr.shape)
        early = []
        for j in range(N_CHAIN):
            d, h = divmod(j, ML_HEADS)
            cc = cf if d == 0 else cb
            r0 = pl.multiple_of(cc * L, L)
            q = qk_ref[0, pl.ds(r0, L), h * dh:(h + 1) * dh]
            k = qk_ref[0, pl.ds(r0, L), ML_DIM + h * dh:ML_DIM + (h + 1) * dh]
            vt = vt_ref[0, cc, h * ML_AUG:(h + 1) * ML_AUG, :]
            ct = c_scr[j]
            both = lax.dot_general(jnp.concatenate([k, ct.astype(BF16)], axis=0), q, (((1,), (1,)), ((), ())),
                                   preferred_element_type=F32)
            early.append((both, vt, k, ct))
        for j in range(N_CHAIN):
            _, vt, k, ct = early[j]
            wv = (vt.astype(F32) * w[j:j + 1, :]).astype(BF16)
            c_scr[j] = a_state[j:j + 1, :] * ct + jnp.dot(wv, k, preferred_element_type=F32)
        for j in range(N_CHAIN):
            d, h = divmod(j, ML_HEADS)
            cc = cf if d == 0 else cb
            mask = upper if d == 0 else lower
            both, vt, _, _ = early[j]
            log_d = jnp.where(mask, bc[j:j + 1, :] + r_cols[:, j:j + 1], -jnp.inf)
            m_row = jnp.maximum(inter[j:j + 1, :], jnp.max(log_d, axis=0, keepdims=True))
            a_inter = jnp.exp(inter[j:j + 1, :] - m_row)
            dmat = jnp.exp(log_d - m_row)
            pt = (both[:L] * dmat).astype(BF16)
            num_aug = a_inter * both[L:] + jnp.dot(vt, pt, preferred_element_type=F32)
            den = num_aug[dh:dh + 1, :]
            hs = hf if d == 0 else hb
            hs[cc, h * dh:(h + 1) * dh, :] = num_aug[:dh] / jnp.maximum(jnp.abs(den), jnp.exp(-m_row))

    def scan(qk_ref, vt_ref, gt_ref, hf, hb, nc):
        def body(i, carry):
            iteration(qk_ref, vt_ref, gt_ref, hf, hb, i, nc - 1 - i)
            return carry

        lax.fori_loop(0, nc, body, 0, unroll=min(2, nc))

    scan(qk_c_ref, vt_c_ref, gt_c_ref, hf_c, hb_c, nc_c)
    scan(qk_x_ref, vt_x_ref, gt_x_ref, hf_x, hb_x, nc_x)

    def finish(hf, hb, so_ref, out_ref, nc):
        def body(c, carry):
            r0 = pl.multiple_of(c * L, L)
            hsum = hf[c] + hb[c]
            for h in range(ML_HEADS):
                sl = slice(h * dh, (h + 1) * dh)
                z = hsum[sl, :]
                mu = jnp.mean(z, axis=0, keepdims=True)
                zc = z - mu
                var = jnp.mean(zc * zc, axis=0, keepdims=True)
                hn = (zc * lax.rsqrt(var + LN_EPS)).T * mlg_ref[:, sl]
                out_ref[0, pl.ds(r0, L), sl] = (so_ref[0, pl.ds(r0, L), sl].astype(F32) * hn).astype(BF16)
            return carry

        lax.fori_loop(0, nc, body, 0)

    if ctx_out:
        finish(hf_c, hb_c, so_c_ref, hn_c_ref, nc_c)
    finish(hf_x, hb_x, so_x_ref, hn_x_ref, nc_x)


def _mlstm_branch(pc, px, l, wts, ctx_out):
    bsz, t_c, _ = pc["qk"].shape
    t_x = px["qk"].shape[1]
    nc_c, nc_x = t_c // ML_CHUNK, t_x // ML_CHUNK
    seq = lambda t, n: pl.BlockSpec((1, t, n), lambda b: (b, 0, 0))
    chunked = lambda nc, n: pl.BlockSpec((1, nc, n, ML_CHUNK), lambda b: (b, 0, 0, 0))
    ins = lambda t, nc, so: [seq(t, 2 * ML_DIM), chunked(nc, ML_HEADS * ML_AUG), chunked(nc, 4 * ML_HEADS)] + (
        [seq(t, ML_DIM)] if so else [])
    args = lambda p, so: [p["qk"], p["vtm"], p["gt"]] + ([p["so"]] if so else [])
    hbuf = lambda nc: pltpu.VMEM((nc, ML_DIM, ML_CHUNK), F32)
    out_t = ([t_c] if ctx_out else []) + [t_x]
    res = pl.pallas_call(
        functools.partial(_mlstm_kernel, nc_c=nc_c, nc_x=nc_x, ctx_out=ctx_out),
        name="mlstm",
        grid=(bsz,),
        in_specs=ins(t_c, nc_c, ctx_out) + ins(t_x, nc_x, True) + [_layer_spec(l, (1, ML_DIM))],
        out_specs=[seq(t, ML_DIM) for t in out_t],
        out_shape=[jax.ShapeDtypeStruct((bsz, t, ML_DIM), BF16) for t in out_t],
        scratch_shapes=[hbuf(nc_c), hbuf(nc_c), hbuf(nc_x), hbuf(nc_x),
                        pltpu.VMEM((N_CHAIN, ML_AUG, ML_HEAD_DIM), F32),
                        pltpu.VMEM((N_CHAIN, LANES), F32)],
        compiler_params=pltpu.CompilerParams(
            dimension_semantics=("parallel",), vmem_limit_bytes=VMEM_LIMIT),
    )(*args(pc, ctx_out), *args(px, True), wts["ml_g"])
    return (res[0], res[1]) if ctx_out else (None, res[0])


ATT_KEYS = 256
ATT_QG = 256
ATT_LOOKAHEAD = 8


def _attn_kernel(*refs, n_seg):
    q_ref = refs[0]
    k_refs = refs[1:1 + n_seg]
    vt_refs = refs[1 + n_seg:1 + 2 * n_seg]
    o_ref = refs[1 + 2 * n_seg]
    qg = min(ATT_QG, q_ref.shape[1])
    n_qg = q_ref.shape[1] // qg
    chunks = [(k_ref, vt_ref, slice(j, j + min(ATT_KEYS, k_ref.shape[1])))
              for k_ref, vt_ref in zip(k_refs, vt_refs)
              for j in range(0, k_ref.shape[1], min(ATT_KEYS, k_ref.shape[1]))]
    steps = [(ci, h, g) for ci in range(len(chunks)) for h in range(MLA_HEADS) for g in range(n_qg)]

    def scores(ci, h, g):
        k_ref, _, ks = chunks[ci]
        hs = slice(h * HEAD_PAD, (h + 1) * HEAD_PAD)
        return lax.dot_general(k_ref[0, ks, hs], q_ref[0, g * qg:(g + 1) * qg, hs],
                               (((1,), (1,)), ((), ())), preferred_element_type=F32)

    state = {}
    pending = {}
    for idx in range(len(steps) + ATT_LOOKAHEAD):
        if idx < len(steps):
            pending[idx] = scores(*steps[idx])
        if idx < ATT_LOOKAHEAD:
            continue
        ci, h, g = steps[idx - ATT_LOOKAHEAD]
        s = pending.pop(idx - ATT_LOOKAHEAD)
        _, vt_ref, ks = chunks[ci]
        vtj = vt_ref[0, h * ATT_AUG:(h + 1) * ATT_AUG, ks]
        cm = jnp.max(s, axis=0, keepdims=True)
        if ci == 0:
            m_new = cm
            acc = jnp.dot(vtj, jnp.exp2(s - m_new).astype(BF16), preferred_element_type=F32)
        else:
            m_old, acc_old = state[h, g]
            m_new = jnp.maximum(m_old, cm)
            acc = acc_old * jnp.exp2(m_old - m_new) + jnp.dot(
                vtj, jnp.exp2(s - m_new).astype(BF16), preferred_element_type=F32)
        state[h, g] = (m_new, acc)
    outs = []
    for h in range(MLA_HEADS):
        accs = [state[h, g][1] for g in range(n_qg)]
        outs.append(jnp.concatenate([a[:V_HEAD] * (1.0 / a[V_HEAD:V_HEAD + 1]) for a in accs], axis=1))
    o_ref[0] = jnp.concatenate(outs, axis=0).T.astype(BF16)


def _attention(q, ks, vts, tq):
    bsz, t, hq = q.shape
    n_seg = len(ks)
    hv = MLA_HEADS * V_HEAD
    kspec = [pl.BlockSpec((1, k.shape[1], hq), lambda b, i: (b, 0, 0)) for k in ks]
    vspec = [pl.BlockSpec((1, MLA_HEADS * ATT_AUG, vt.shape[2]), lambda b, i: (b, 0, 0)) for vt in vts]
    return pl.pallas_call(
        functools.partial(_attn_kernel, n_seg=n_seg),
        name="attention",
        grid=(bsz, t // tq),
        in_specs=[pl.BlockSpec((1, tq, hq), lambda b, i: (b, i, 0))] + kspec + vspec,
        out_specs=pl.BlockSpec((1, tq, hv), lambda b, i: (b, i, 0)),
        out_shape=jax.ShapeDtypeStruct((bsz, t, hv), BF16),
        compiler_params=pltpu.CompilerParams(
            dimension_semantics=("parallel", "arbitrary"), vmem_limit_bytes=VMEM_LIMIT),
    )(q, *ks, *vts)


FF_CHUNK = 1024
MERGE_GROUPS = 2


def _merge_mlp_kernel(x_ref, hc_ref, hm_ref, oa_ref, gates_ref, g1_ref, sc2_ref, sh2_ref, g2_ref,
                      wc_ref, wm_ref, wa_ref, wo_ref, bo_ref, ln1g_ref, ln1b_ref,
                      w1_ref, b1_ref, w2_ref, b2_ref, ln2g_ref, ln2b_ref, o_ref):
    d = D_MODEL
    tm = x_ref.shape[1]
    rows = [slice(g * tm // MERGE_GROUPS, (g + 1) * tm // MERGE_GROUPS) for g in range(MERGE_GROUPS)]
    dot = functools.partial(jnp.dot, preferred_element_type=F32)
    yc = [dot(hc_ref[0, r, :], wc_ref[...]) for r in rows]
    ym = [dot(hm_ref[0, r, :], wm_ref[...]) for r in rows]
    ya = [dot(oa_ref[0, r, :], wa_ref[...]) for r in rows]
    mix = [(gates_ref[0, r, 0:d].astype(F32) * c + gates_ref[0, r, d:2 * d].astype(F32) * m
            + gates_ref[0, r, 2 * d:3 * d].astype(F32) * a).astype(BF16) for r, c, m, a in zip(rows, yc, ym, ya)]
    y = [dot(mx, wo_ref[...]) + bo_ref[...] for mx in mix]
    x1 = [_layer_norm(ALPHA * x_ref[0, r, :] + g1_ref[...] * yy, ln1g_ref[...], ln1b_ref[...])
          for r, yy in zip(rows, y)]
    u2 = [(xx * (1.0 + sc2_ref[...]) + sh2_ref[...]).astype(BF16) for xx in x1]
    acc = [jnp.zeros(xx.shape, F32) for xx in x1]
    for j in range(D_FF // FF_CHUNK):
        sl = slice(j * FF_CHUNK, (j + 1) * FF_CHUNK)
        hdn = [jnp.maximum(dot(u, w1_ref[:, sl]) + b1_ref[:, sl], 0.0) for u in u2]
        acc = [ac + dot((hd * hd).astype(BF16), w2_ref[sl, :]) for ac, hd in zip(acc, hdn)]
    for r, xx, ac in zip(rows, x1, acc):
        o_ref[0, r, :] = _layer_norm(ALPHA * xx + g2_ref[...] * (ac + b2_ref[...]), ln2g_ref[...], ln2b_ref[...])


def _merge_mlp(x, hc, hm, oa, gates, mod, l, mod_row, wts, tm):
    bsz, t, d = x.shape
    tok = lambda n: pl.BlockSpec((1, tm, n), lambda b, i: (b, i, 0))
    mods = [_mod_spec(l, k, mod_row) for k in (MOD_G1, MOD_SC2, MOD_SH2, MOD_G2)]
    return pl.pallas_call(
        _merge_mlp_kernel,
        name="merge_mlp",
        grid=(bsz, t // tm),
        in_specs=[tok(d), tok(CONV_DIM), tok(ML_DIM), tok(MLA_HEADS * V_HEAD), tok(3 * d)] + mods + [
            _layer_spec(l, (CONV_DIM, d)), _layer_spec(l, (ML_DIM, d)), _layer_spec(l, (MLA_HEADS * V_HEAD, d)),
            _layer_spec(l, (d, d)), _layer_spec(l, (1, d)), _layer_spec(l, (1, d)), _layer_spec(l, (1, d)),
            _layer_spec(l, (d, D_FF)), _layer_spec(l, (1, D_FF)), _layer_spec(l, (D_FF, d)), _layer_spec(l, (1, d)),
            _layer_spec(l, (1, d)), _layer_spec(l, (1, d))],
        out_specs=tok(d),
        out_shape=jax.ShapeDtypeStruct((bsz, t, d), F32),
        compiler_params=pltpu.CompilerParams(
            dimension_semantics=("parallel", "arbitrary"), vmem_limit_bytes=VMEM_LIMIT),
    )(x, hc, hm, oa, gates, mod, mod, mod, mod,
      wts["w_conv_out"], wts["w_ml_out"], wts["w_mla_out"], wts["w_out"], wts["b_out"], wts["ln1_g"], wts["ln1_b"],
      wts["w1"], wts["b1"], wts["w2"], wts["b2"], wts["ln2_g"], wts["ln2_b"])


def _rope_runs():
    runs = []
    for half in range(2):
        for a in range(2):
            start = a * 2 * ROPE_FREQ + half * ROPE_FREQ
            runs.append(slice(start, start + ROPE_FREQ))
    return runs


def _pack_weights(w_in, b_in, w_dw, b_dw, cn_g, cn_b, w_conv_out, ml_g, w_ml_out, qn_g, w_uq, kvn_g, w_ukv,
                  w_mla_out, w_out, b_out, ln1_g, ln1_b, w1, b1, w2, b2, ln2_g, ln2_b):
    nl = w_in.shape[0]

    def packed(z, dtype):
        a, qkv, o, g, cq, ckv, kr, gates = (z[..., _SRC[i]:_SRC[i + 1]].astype(dtype) for i in range(8))
        zeros = lambda n: jnp.zeros(z.shape[:-1] + (n,), dtype)
        kr_block = [zeros(QK_NOPE)] + [kr[..., s] for s in _rope_runs()] + [zeros(HEAD_PAD - QK_NOPE - QK_ROPE)]
        g_block = [g[..., i * ML_HEADS:(i + 1) * ML_HEADS] for i in (0, 2, 1, 3)] + [zeros(LANES - 4 * ML_HEADS)]
        return jnp.concatenate([a, qkv, o, cq, ckv] + kr_block + g_block + [gates], axis=-1)

    wq = w_uq.reshape(nl, Q_LORA, MLA_HEADS, QK_NOPE + QK_ROPE)
    wq_p = jnp.concatenate([wq[..., :QK_NOPE]] + [wq[..., QK_NOPE:][..., s] for s in _rope_runs()]
                           + [jnp.zeros(wq.shape[:-1] + (HEAD_PAD - QK_NOPE - QK_ROPE,), F32)], axis=-1)
    wkv = w_ukv.reshape(nl, KV_LORA, MLA_HEADS, QK_NOPE + V_HEAD)
    wkn_p = jnp.concatenate([wkv[..., :QK_NOPE], jnp.zeros(wkv.shape[:-1] + (HEAD_PAD - QK_NOPE,), F32)], axis=-1)
    wv_t = wkv[..., QK_NOPE:].reshape(nl, KV_LORA, MLA_HEADS * V_HEAD).transpose(0, 2, 1)
    ncg = CONV_DIM // LANES
    row = lambda z: z[:, None, :]
    return {
        "w_pack": packed(w_in, BF16), "b_pack": row(packed(b_in, F32)),
        "qn_g": row(qn_g), "w_uq": wq_p.reshape(nl, Q_LORA, MLA_HEADS * HEAD_PAD).astype(BF16),
        "kvn_g": row(kvn_g), "w_kn": wkn_p.reshape(nl, KV_LORA, MLA_HEADS * HEAD_PAD).astype(BF16),
        "w_v": wv_t.astype(BF16),
        "w_dw": jnp.pad(w_dw, ((0, 0), (0, 1), (0, 0))).reshape(nl, CONV_K + 1, ncg, LANES).transpose(0, 2, 1, 3),
        "b_dw": b_dw.reshape(nl, ncg, 1, LANES),
        "cn_g": row(cn_g), "cn_b": row(cn_b), "ml_g": row(ml_g),
        "w_conv_out": w_conv_out.astype(BF16), "w_ml_out": w_ml_out.astype(BF16),
        "w_mla_out": w_mla_out.astype(BF16), "w_out": w_out.astype(BF16), "b_out": row(b_out),
        "ln1_g": row(ln1_g), "ln1_b": row(ln1_b), "w1": w1.astype(BF16), "b1": row(b1),
        "w2": w2.astype(BF16), "b2": row(b2), "ln2_g": row(ln2_g), "ln2_b": row(ln2_b),
    }


def _rope_tables(n_tokens):
    rows = n_tokens // GRID_W
    rr, cc = np.meshgrid(np.arange(rows, dtype=np.float32), np.arange(GRID_W, dtype=np.float32), indexing="ij")
    inv = (np.float32(ROPE_THETA) ** (-np.arange(ROPE_FREQ, dtype=np.float32) / np.float32(ROPE_FREQ))).astype(np.float32)
    ang = np.stack([rr.reshape(-1), cc.reshape(-1)], -1)[..., None] * inv
    cos = np.cos(ang).astype(np.float32).reshape(n_tokens, ROPE_HALF)
    sin = np.sin(ang).astype(np.float32).reshape(n_tokens, ROPE_HALF)
    one = np.ones((n_tokens, QK_NOPE), np.float32)
    z_nope = np.zeros((n_tokens, QK_NOPE), np.float32)
    z_half = np.zeros((n_tokens, ROPE_HALF), np.float32)
    z_pad = np.zeros((n_tokens, HEAD_PAD - QK_NOPE - QK_ROPE), np.float32)
    tc = np.concatenate([one, cos, cos, z_pad], axis=1)
    tsm = np.concatenate([z_nope, -sin, z_half, z_pad], axis=1)
    tsp = np.concatenate([z_nope, z_half, sin, z_pad], axis=1)
    return tc, tsm, tsp


def _identity_tables(n_tokens):
    return (np.ones((n_tokens, HEAD_PAD), np.float32), np.zeros((n_tokens, HEAD_PAD), np.float32),
            np.zeros((n_tokens, HEAD_PAD), np.float32))


def kernel(x, c, ctx, c_ctx, w_mod, b_mod, w_in, b_in, w_dw, b_dw, conv_norm_g, conv_norm_b, w_conv_out,
           mlstm_norm_g, w_mlstm_out, q_norm_g, w_uq, kv_norm_g, w_ukv, w_mla_out, w_out, b_out,
           ln1_g, ln1_b, w1, b1, w2, b2, ln2_g, ln2_b):
    bsz, t_x, d = x.shape
    t_c = ctx.shape[1]
    n_rows = 24
    cc = jnp.concatenate([c, c_ctx[None, :], jnp.zeros((n_rows - bsz - 1, d), F32)], axis=0)
    mod = _modulation(cc, w_mod, b_mod).reshape(DEPTH, n_rows, N_MOD, 1, d)
    wts = _pack_weights(w_in, b_in, w_dw, b_dw, conv_norm_g, conv_norm_b, w_conv_out, mlstm_norm_g,
                        w_mlstm_out, q_norm_g, w_uq, kv_norm_g, w_ukv, w_mla_out, w_out, b_out,
                        ln1_g, ln1_b, w1, b1, w2, b2, ln2_g, ln2_b)
    rope_x = _rope_tables(t_x)
    rope_c = _identity_tables(t_c)
    ctx_row = bsz

    for l in range(DEPTH):
        with_ctx = l < DEPTH - 1
        px = _in_projection(x, mod, l, None, wts, rope_x, 512)
        pc = _in_projection(ctx, mod, l, ctx_row, wts, rope_c, 256, keys_only=not with_ctx)

        hm_c, hm_x = _mlstm_branch(pc, px, l, wts, with_ctx)
        oa_x = _attention(px["q"], [pc["k"], px["k"]], [pc["vta"], px["vta"]], 512)
        hc_x = _conv_branch(px["glu"], l, wts)
        x_new = _merge_mlp(x, hc_x, hm_x, oa_x, px["gates"], mod, l, None, wts, 512)
        if with_ctx:
            hc_c = _conv_branch(pc["glu"], l, wts)
            oa_c = _attention(pc["q"], [pc["k"]], [pc["vta"]], 256)
            ctx = _merge_mlp(ctx, hc_c, hm_c, oa_c, pc["gates"], mod, l, ctx_row, wts, 256)
        x = x_new
    return x
```
